```python
import math
import jax, jax.numpy as jnp
from jax import lax
import numpy as np

D_MODEL = 1024
BATCH = 2
SEQ = 8192
DEPTH = 2

N_EVEN = (DEPTH + 1) // 2
N_ODD = DEPTH // 2

A_HEADS = 8
A_HEAD_DIM = D_MODEL // 16
A_WIDTH = A_HEADS * A_HEAD_DIM
DILATED_CONFIGS = ((128, 1), (512, 4), (2048, 16))
ROT_DIM = A_HEAD_DIM // 4
ROPE_THETA = 500000.0

POOL_WINDOWS = (2, 4, 8, 16)
POOL_GROUPS = 4
POOL_GROUP_DIM = D_MODEL // 8
POOL_WIDTH = POOL_GROUPS * POOL_GROUP_DIM

EVEN_IN = 3 * A_WIDTH + POOL_WIDTH
EVEN_MIX = A_WIDTH + POOL_WIDTH

DN_HEADS = 8
DN_HEAD_DIM = D_MODEL // 8
DN_WIDTH = DN_HEADS * DN_HEAD_DIM
CONV_WIDTH = 4
CHUNK = 64
ODD_IN = 4 * DN_WIDTH + 2 * DN_HEADS

FFN_DIM = D_MODEL * 7 // 2
N_EXPERTS = 8
TOP_K = 2
EPS = 1e-6

kernel_name = 'hybrid_dilated_pool_deltanet_moe_block'


def _rmsnorm(x, g):
    xf = x.astype(jnp.float32)
    y = xf * lax.rsqrt(jnp.mean(xf * xf, axis=-1, keepdims=True) + EPS) * g
    return y.astype(x.dtype)


def _partial_rope(t, positions):
    half = ROT_DIM // 2
    inv_freq = ROPE_THETA ** (-jnp.arange(0, ROT_DIM, 2, dtype=jnp.float32) / ROT_DIM)
    ang = positions.astype(jnp.float32)[..., None] * inv_freq
    cos, sin = jnp.cos(ang)[:, :, None, :], jnp.sin(ang)[:, :, None, :]
    x1 = t[..., :half].astype(jnp.float32)
    x2 = t[..., half:ROT_DIM].astype(jnp.float32)
    rot = jnp.concatenate([x1 * cos - x2 * sin, x2 * cos + x1 * sin], axis=-1)
    return jnp.concatenate([rot.astype(t.dtype), t[..., ROT_DIM:]], axis=-1)


def _dilated_branch(q, k, v, window, dilation):
    bsz, nh, seq, hd = q.shape
    blk = window // dilation
    span = blk * dilation
    length = -(-seq // span) * span
    n_sub = length // dilation
    nb = n_sub // blk

    def to_blocks(t):
        t = jnp.pad(t, ((0, 0), (0, 0), (0, length - seq), (0, 0)))
        t = t.reshape(bsz, nh, n_sub, dilation, hd).transpose(0, 1, 3, 2, 4)
        return t.reshape(bsz, nh, dilation, nb, blk, hd)

    def with_prev(t):
        prev = jnp.pad(t, ((0, 0), (0, 0), (0, 0), (1, 0), (0, 0), (0, 0)))[:, :, :, :-1]
        return jnp.concatenate([prev, t], axis=4)

    qb = to_blocks(q)
    kk = with_prev(to_blocks(k))
    vv = with_prev(to_blocks(v))
    s = jnp.einsum('bhrnqd,bhrnkd->bhrnqk', qb, kk).astype(jnp.float32) * hd ** -0.5
    qi = jnp.arange(blk)[:, None] + blk
    kj = jnp.arange(2 * blk)[None, :]
    rel = qi - kj
    band = (rel >= 0) & (rel <= blk)
    has_prev = (jnp.arange(nb) > 0)[:, None, None] | (kj >= blk)[None]
    mask = band[None] & has_prev
    s = jnp.where(mask, s, -jnp.inf)
    m = jnp.max(s, axis=-1, keepdims=True)
    p = jnp.exp(s - m)
    den = jnp.sum(p, axis=-1)
    o = jnp.einsum('bhrnqk,bhrnkd->bhrnqd', p, vv.astype(jnp.float32)) / den[..., None]
    lse = m[..., 0] + jnp.log(den)
    o = o.reshape(bsz, nh, dilation, n_sub, hd).transpose(0, 1, 3, 2, 4)
    o = o.reshape(bsz, nh, length, hd)[:, :, :seq]
    lse = lse.reshape(bsz, nh, dilation, n_sub).transpose(0, 1, 3, 2)
    lse = lse.reshape(bsz, nh, length)[:, :, :seq]
    return o, lse


def _dilated_mixture(q, k, v):
    outs, lses = [], []
    for window, dilation in DILATED_CONFIGS:
        o, lse = _dilated_branch(q, k, v, window, dilation)
        outs.append(o)
        lses.append(lse)
    w = jax.nn.softmax(jnp.stack(lses), axis=0)
    return jnp.sum(w[..., None] * jnp.stack(outs), axis=0)


def _multiscale_pool(u, pool_w, pool_scale):
    bsz, seq, _ = u.shape
    ug = u.reshape(bsz, seq, POOL_GROUPS, POOL_GROUP_DIM).astype(jnp.float32)
    cs = jnp.pad(jnp.cumsum(ug, axis=1), ((0, 0), (1, 0), (0, 0), (0, 0)))
    win = jnp.array(POOL_WINDOWS, dtype=jnp.int32)
    t = jnp.arange(seq, dtype=jnp.int32)[:, None]
    lo = jnp.maximum(t + 1 - win[None, :], 0)
    cnt = jnp.minimum(t + 1, win[None, :]).astype(jnp.float32)
    lower = cs[:, lo, jnp.arange(POOL_GROUPS)[None, :], :]
    pooled = (cs[:, 1:] - lower) / cnt[None, :, :, None] - ug
    mixed = jnp.einsum('bsgc,gcd->bsgd', pooled, pool_w)
    mixed = mixed * pool_scale.reshape(POOL_GROUPS, POOL_GROUP_DIM)
    return mixed.reshape(bsz, seq, POOL_WIDTH).astype(u.dtype)


def _even_mixer(h, positions, w_in, pool_w, pool_scale, w_out):
    bsz, seq, _ = h.shape
    proj = h @ w_in
    q, k, v, u = jnp.split(proj, [A_WIDTH, 2 * A_WIDTH, 3 * A_WIDTH], axis=-1)
    q = _partial_rope(q.reshape(bsz, seq, A_HEADS, A_HEAD_DIM), positions)
    k = _partial_rope(k.reshape(bsz, seq, A_HEADS, A_HEAD_DIM), positions)
    v = v.reshape(bsz, seq, A_HEADS, A_HEAD_DIM)
    o_a = _dilated_mixture(q.transpose(0, 2, 1, 3), k.transpose(0, 2, 1, 3), v.transpose(0, 2, 1, 3))
    o_a = o_a.transpose(0, 2, 1, 3).reshape(bsz, seq, A_WIDTH).astype(h.dtype)
    o_b = _multiscale_pool(u, pool_w, pool_scale)
    return jnp.concatenate([o_a, o_b], axis=-1) @ w_out


def _causal_depthwise_conv(x, w):
    ch = x.shape[-1]
    return lax.conv_general_dilated(
        x, w[:, None, :], window_strides=(1,), padding=[(w.shape[0] - 1, 0)],
        dimension_numbers=('NWC', 'WIO', 'NWC'), feature_group_count=ch)


def _chunk_gated_delta(q, k, v, g, beta):
    bsz, nh, seq, dk = q.shape
    dv = v.shape[-1]
    n = seq // CHUNK
    q = q.reshape(bsz, nh, n, CHUNK, dk) * dk ** -0.5
    k = k.reshape(bsz, nh, n, CHUNK, dk)
    v = v.reshape(bsz, nh, n, CHUNK, dv)
    g = g.reshape(bsz, nh, n, CHUNK)
    beta = beta.reshape(bsz, nh, n, CHUNK)
    gc = jnp.cumsum(g, axis=-1)
    idx = jnp.arange(CHUNK)
    causal = idx[:, None] >= idx[None, :]
    strict = idx[:, None] > idx[None, :]
    decay = jnp.exp(jnp.where(causal, gc[..., :, None] - gc[..., None, :], -jnp.inf))
    kb = k * beta[..., None]
    lmat = jnp.where(strict, jnp.einsum('bhnid,bhnjd->bhnij', kb, k) * decay, 0.0)
    eye = jnp.eye(CHUNK, dtype=jnp.float32)
    rhs = jnp.concatenate([v * beta[..., None], kb * jnp.exp(gc)[..., None]], axis=-1)
    sol = lax.linalg.triangular_solve(lmat + eye, rhs, left_side=True, lower=True,
                                      unit_diagonal=True)
    u, w = sol[..., :dv], sol[..., dv:]
    attn = jnp.einsum('bhnid,bhnjd->bhnij', q, k) * decay
    q_g = q * jnp.exp(gc)[..., None]
    k_g = k * jnp.exp(gc[..., -1:] - gc)[..., None]
    chunk_decay = jnp.exp(gc[..., -1])

    def step(state, xs):
        q_c, k_c, u_c, w_c, a_c, d_c = xs
        v_new = u_c - jnp.einsum('bhck,bhkv->bhcv', w_c, state)
        o_c = jnp.einsum('bhck,bhkv->bhcv', q_c, state) + jnp.einsum('bhij,bhjv->bhiv', a_c, v_new)
        state = state * d_c[..., None, None] + jnp.einsum('bhck,bhcv->bhkv', k_c, v_new)
        return state, o_c

    front = lambda t: jnp.moveaxis(t, 2, 0)
    state0 = jnp.zeros((bsz, nh, dk, dv), jnp.float32)
    _, o = lax.scan(step, state0, (front(q_g), front(k_g), front(u), front(w), front(attn),
                                   front(chunk_decay)))
    return jnp.moveaxis(o, 0, 2).reshape(bsz, nh, seq, dv)


def _odd_mixer(h, w_in, conv_w, a_log, dt_bias, norm_g, w_out):
    bsz, seq, _ = h.shape
    proj = h @ w_in
    qkv, gate, beta_logit, a_logit = jnp.split(
        proj, [3 * DN_WIDTH, 4 * DN_WIDTH, 4 * DN_WIDTH + DN_HEADS], axis=-1)
    qkv = jax.nn.silu(_causal_depthwise_conv(qkv, conv_w))
    q, k, v = jnp.split(qkv, 3, axis=-1)
    heads = lambda t: t.reshape(bsz, seq, DN_HEADS, DN_HEAD_DIM).transpose(0, 2, 1, 3).astype(jnp.float32)
    q, k, v = heads(q), heads(k), heads(v)
    q = q * lax.rsqrt(jnp.sum(q * q, axis=-1, keepdims=True) + EPS)
    k = k * lax.rsqrt(jnp.sum(k * k, axis=-1, keepdims=True) + EPS)
    beta = jax.nn.sigmoid(beta_logit.astype(jnp.float32)).transpose(0, 2, 1)
    g = -(jnp.exp(a_log.astype(jnp.float32))
          * jax.nn.softplus(a_logit.astype(jnp.float32) + dt_bias)).transpose(0, 2, 1)
    o = _chunk_gated_delta(q, k, v, g, beta).transpose(0, 2, 1, 3)
    o = o * lax.rsqrt(jnp.mean(o * o, axis=-1, keepdims=True) + EPS) * norm_g
    o = o * jax.nn.silu(gate.astype(jnp.float32).reshape(bsz, seq, DN_HEADS, DN_HEAD_DIM))
    return o.reshape(bsz, seq, DN_WIDTH).astype(h.dtype) @ w_out


def _swiglu(h, w_gate, w_up, w_down):
    return (jax.nn.silu(h @ w_gate) * (h @ w_up)) @ w_down


def _moe(h, router_w, w_gate, w_up, w_down):
    bsz, seq, d = h.shape
    t = h.reshape(bsz * seq, d)
    logits = (t @ router_w).astype(jnp.float32)
    top_val, top_idx = lax.top_k(logits, TOP_K)
    top_w = jax.nn.softmax(top_val, axis=-1)
    combine = jnp.sum(jax.nn.one_hot(top_idx, N_EXPERTS, dtype=jnp.float32) * top_w[..., None], axis=1)
    out = jnp.zeros((bsz * seq, d), jnp.float32)
    for e in range(N_EXPERTS):
        out = out + combine[:, e:e + 1] * _swiglu(t, w_gate[e], w_up[e], w_down[e]).astype(jnp.float32)
    return out.reshape(bsz, seq, d).astype(h.dtype)


def setup_inputs(seed: int = 0) -> dict:
    key = jax.random.key(seed)
    ks = iter(jax.random.split(key, 40))
    nrm = lambda shape, scale: jax.random.normal(next(ks), shape, jnp.float32) * scale
    gain = lambda shape: 1.0 + 0.1 * jax.random.normal(next(ks), shape, jnp.float32)
    x = nrm((BATCH, SEQ, D_MODEL), 1.0)
    c = nrm((BATCH, D_MODEL), 1.0)
    offs = jax.random.randint(next(ks), (BATCH, 1), 0, 4096, dtype=jnp.int32)
    positions = offs + jnp.arange(SEQ, dtype=jnp.int32)[None, :]
    ada_w = nrm((DEPTH, D_MODEL, 6 * D_MODEL), 0.5 * D_MODEL ** -0.5)
    ada_b = nrm((DEPTH, 6 * D_MODEL), 0.02)
    mix_pre_g = gain((DEPTH, D_MODEL))
    mix_post_g = gain((DEPTH, D_MODEL))
    ffn_pre_g = gain((DEPTH, D_MODEL))
    ffn_post_g = gain((DEPTH, D_MODEL))
    even_w_in = nrm((N_EVEN, D_MODEL, EVEN_IN), D_MODEL ** -0.5)
    even_pool_w = nrm((N_EVEN, POOL_GROUPS, POOL_GROUP_DIM, POOL_GROUP_DIM), POOL_GROUP_DIM ** -0.5)
    even_pool_scale = 0.1 + 0.02 * jax.random.normal(next(ks), (N_EVEN, POOL_WIDTH), jnp.float32)
    even_w_out = nrm((N_EVEN, EVEN_MIX, D_MODEL), EVEN_MIX ** -0.5)
    even_ffn_w_gate = nrm((N_EVEN, D_MODEL, FFN_DIM), D_MODEL ** -0.5)
    even_ffn_w_up = nrm((N_EVEN, D_MODEL, FFN_DIM), D_MODEL ** -0.5)
    even_ffn_w_down = nrm((N_EVEN, FFN_DIM, D_MODEL), FFN_DIM ** -0.5)
    odd_w_in = nrm((N_ODD, D_MODEL, ODD_IN), D_MODEL ** -0.5)
    odd_conv_w = nrm((N_ODD, CONV_WIDTH, 3 * DN_WIDTH), CONV_WIDTH ** -0.5)
    odd_a_log = jnp.log(jax.random.uniform(next(ks), (N_ODD, DN_HEADS), jnp.float32, 1.0, 16.0))
    dt = jnp.exp(jax.random.uniform(next(ks), (N_ODD, DN_HEADS), jnp.float32,
                                    math.log(1e-3), math.log(1e-1)))
    odd_dt_bias = dt + jnp.log(-jnp.expm1(-dt))
    odd_norm_g = gain((N_ODD, DN_HEAD_DIM))
    odd_w_out = nrm((N_ODD, DN_WIDTH, D_MODEL), DN_WIDTH ** -0.5)
    odd_router_w = nrm((N_ODD, D_MODEL, N_EXPERTS), D_MODEL ** -0.5)
    odd_moe_w_gate = nrm((N_ODD, N_EXPERTS, D_MODEL, FFN_DIM), D_MODEL ** -0.5)
    odd_moe_w_up = nrm((N_ODD, N_EXPERTS, D_MODEL, FFN_DIM), D_MODEL ** -0.5)
    odd_moe_w_down = nrm((N_ODD, N_EXPERTS, FFN_DIM, D_MODEL), FFN_DIM ** -0.5)
    return {'x': x, 'c': c, 'positions': positions, 'ada_w': ada_w, 'ada_b': ada_b,
            'mix_pre_g': mix_pre_g, 'mix_post_g': mix_post_g,
            'ffn_pre_g': ffn_pre_g, 'ffn_post_g': ffn_post_g,
            'even_w_in': even_w_in, 'even_pool_w': even_pool_w,
            'even_pool_scale': even_pool_scale, 'even_w_out': even_w_out,
            'even_ffn_w_gate': even_ffn_w_gate, 'even_ffn_w_up': even_ffn_w_up,
            'even_ffn_w_down': even_ffn_w_down,
            'odd_w_in': odd_w_in, 'odd_conv_w': odd_conv_w, 'odd_a_log': odd_a_log,
            'odd_dt_bias': odd_dt_bias, 'odd_norm_g': odd_norm_g, 'odd_w_out': odd_w_out,
            'odd_router_w': odd_router_w, 'odd_moe_w_gate': odd_moe_w_gate,
            'odd_moe_w_up': odd_moe_w_up, 'odd_moe_w_down': odd_moe_w_down}


def reference(x, c, positions, ada_w, ada_b, mix_pre_g, mix_post_g, ffn_pre_g, ffn_post_g,
              even_w_in, even_pool_w, even_pool_scale, even_w_out,
              even_ffn_w_gate, even_ffn_w_up, even_ffn_w_down,
              odd_w_in, odd_conv_w, odd_a_log, odd_dt_bias, odd_norm_g, odd_w_out,
              odd_router_w, odd_moe_w_gate, odd_moe_w_up, odd_moe_w_down):
    c_act = jax.nn.silu(c)
    for layer in range(DEPTH):
        j = layer // 2
        mod = c_act @ ada_w[layer] + ada_b[layer]
        sh1, sc1, gt1, sh2, sc2, gt2 = jnp.split(mod, 6, axis=-1)
        h = _rmsnorm(x, mix_pre_g[layer]) * (1.0 + sc1[:, None, :]) + sh1[:, None, :]
        if layer % 2 == 0:
            y = _even_mixer(h, positions, even_w_in[j], even_pool_w[j], even_pool_scale[j], even_w_out[j])
        else:
            y = _odd_mixer(h, odd_w_in[j], odd_conv_w[j], odd_a_log[j], odd_dt_bias[j],
                           odd_norm_g[j], odd_w_out[j])
        x = x + gt1[:, None, :] * _rmsnorm(y, mix_post_g[layer])
        h = _rmsnorm(x, ffn_pre_g[layer]) * (1.0 + sc2[:, None, :]) + sh2[:, None, :]
        if layer % 2 == 0:
            y = _swiglu(h, even_ffn_w_gate[j], even_ffn_w_up[j], even_ffn_w_down[j])
        else:
            y = _moe(h, odd_router_w[j], odd_moe_w_gate[j], odd_moe_w_up[j], odd_moe_w_down[j])
        x = x + gt2[:, None, :] * _rmsnorm(y, ffn_post_g[layer])
    return x
```

```python
import functools

import numpy as np
import jax
import jax.numpy as jnp
from jax import lax
from jax.experimental import pallas as pl
from jax.experimental.pallas import tpu as pltpu

F32 = jnp.float32
BF16 = jnp.bfloat16
HIGHEST = lax.Precision.HIGHEST

D_MODEL = 1024
DEPTH = 2
A_HEADS = 8
A_HEAD_DIM = 64
A_WIDTH = 512
DILATIONS = (1, 4, 16)
BAND = 128
ROT_DIM = 16
ROPE_THETA = 500000.0
POOL_WINDOWS = (2, 4, 8, 16)
POOL_GROUP_DIM = 128
POOL_WIDTH = 512
POOL_HALO = 16
DN_HEADS = 8
DN_HEAD_DIM = 128
DN_WIDTH = 1024
CONV_WIDTH = 4
CONV_HALO = 8
CHUNK = 64
FFN_DIM = 3584
N_EXPERTS = 8
EPS = 1e-6
LANES = 128
NEG_INF = float("-inf")

VMEM_LIMIT = 56 * 1024 * 1024


def _params(*sem):
    return pltpu.CompilerParams(dimension_semantics=sem, vmem_limit_bytes=VMEM_LIMIT)


def _dot(a, b, precision=None):
    return jnp.dot(a, b, preferred_element_type=F32, precision=precision)


def _dot_nt(a, b, precision=None):
    return lax.dot_general(a, b, (((1,), (1,)), ((), ())), preferred_element_type=F32,
                           precision=precision)


def _dot_tn(a, b, precision=None):
    return lax.dot_general(a, b, (((0,), (0,)), ((), ())), preferred_element_type=F32,
                           precision=precision)


def _silu(x):
    return x * jax.nn.sigmoid(x)


def _norm_mod(x, g, sc, sh):
    ms = jnp.mean(x * x, axis=-1, keepdims=True)
    return x * lax.rsqrt(ms + EPS) * g * (1.0 + sc) + sh


def _post_residual(x, y, g, gt):
    ms = jnp.mean(y * y, axis=-1, keepdims=True)
    return x + gt * (y * lax.rsqrt(ms + EPS) * g)


def _mod_body(c_ref, w_ref, b_ref, o_ref):
    c = c_ref[...]
    o_ref[...] = _dot(_silu(c), w_ref[...], precision=HIGHEST) + b_ref[...]


def _adaln_mod(c, ada_w, ada_b):
    bsz = c.shape[0]
    rows = 8
    c_pad = jnp.zeros((rows, D_MODEL), F32).at[:bsz].set(c)
    tn = 1536
    out = pl.pallas_call(
        _mod_body,
        grid=(DEPTH, 6 * D_MODEL // tn),
        in_specs=[pl.BlockSpec((rows, D_MODEL), lambda l, n: (0, 0)),
                  pl.BlockSpec((None, D_MODEL, tn), lambda l, n: (l, 0, n)),
                  pl.BlockSpec((None, 1, tn), lambda l, n: (l, 0, n))],
        out_specs=pl.BlockSpec((None, rows, tn), lambda l, n: (l, 0, n)),
        out_shape=jax.ShapeDtypeStruct((DEPTH, rows, 6 * D_MODEL), F32),
        compiler_params=_params("arbitrary", "arbitrary"),
        name="adaln_mod",
    )(c_pad, ada_w, ada_b.reshape(DEPTH, 1, 6 * D_MODEL))
    return out[:, :bsz]


def _vec_rows(rows, bsz):
    rows = [jnp.broadcast_to(r.astype(F32), (bsz, D_MODEL)) for r in rows]
    rows = rows + [jnp.zeros((bsz, D_MODEL), F32)] * (8 - len(rows))
    return jnp.stack(rows, axis=1)


def _rope(t, cosv, sin_lo, sin_hi):
    return (t * cosv + pltpu.roll(t, LANES - ROT_DIM // 2, axis=1) * sin_lo
            + pltpu.roll(t, ROT_DIM // 2, axis=1) * sin_hi)


def _in0_body(x_ref, pos_ref, vec_ref, w_ref, tab_ref, pw_ref, ps_ref,
              q_ref, k_ref, v_ref, ob_ref, ubuf):
    i = pl.program_id(1)
    tm = x_ref.shape[0]

    @pl.when(i == 0)
    def _():
        ubuf[0:POOL_HALO, :] = jnp.zeros((POOL_HALO, POOL_WIDTH), F32)

    h = _norm_mod(x_ref[...], vec_ref[0:1, :], vec_ref[1:2, :], vec_ref[2:3, :]).astype(BF16)

    ang = pos_ref[...].astype(F32) * tab_ref[0:1, :]
    cosv = jnp.cos(ang)
    sinv = jnp.sin(ang)
    sin_lo = -sinv * tab_ref[1:2, :]
    sin_hi = sinv * tab_ref[2:3, :]

    pq = _dot(h, w_ref[:, 0:A_WIDTH])
    pk = _dot(h, w_ref[:, A_WIDTH:2 * A_WIDTH])
    for t in range(A_WIDTH // LANES):
        cols = slice(t * LANES, (t + 1) * LANES)
        q_ref[:, cols] = (_rope(pq[:, cols], cosv, sin_lo, sin_hi)
                          * (A_HEAD_DIM ** -0.5)).astype(BF16)
        k_ref[:, cols] = _rope(pk[:, cols], cosv, sin_lo, sin_hi).astype(BF16)
    v_ref[...] = _dot(h, w_ref[:, 2 * A_WIDTH:3 * A_WIDTH]).astype(BF16)

    ubuf[POOL_HALO:POOL_HALO + tm, :] = _dot(h, w_ref[:, 3 * A_WIDTH:3 * A_WIDTH + POOL_WIDTH])
    tpos = i * tm + lax.broadcasted_iota(jnp.int32, (tm, 1), 0)
    for g, win in enumerate(POOL_WINDOWS):
        cols = slice(g * POOL_GROUP_DIM, (g + 1) * POOL_GROUP_DIM)
        cur = ubuf[POOL_HALO:POOL_HALO + tm, cols]
        acc = cur
        for j in range(1, win):
            acc = acc + ubuf[POOL_HALO - j:POOL_HALO - j + tm, cols]
        cnt = jnp.minimum(tpos + 1, win).astype(F32)
        pooled = acc / cnt - cur
        mixed = _dot(pooled.astype(BF16), pw_ref[g]) * ps_ref[0:1, cols]
        ob_ref[:, cols] = mixed.astype(BF16)
    ubuf[0:POOL_HALO, :] = ubuf[tm:tm + POOL_HALO, :]


def _rope_table():
    lane = np.arange(LANES) % A_HEAD_DIM
    half = ROT_DIM // 2
    inv_freq = ROPE_THETA ** (-jnp.arange(0, ROT_DIM, 2, dtype=F32) / ROT_DIM)
    freq = jnp.where(jnp.asarray(lane < ROT_DIM), inv_freq[jnp.asarray(lane % half)], 0.0)
    lo = jnp.asarray((lane < half).astype(np.float32))
    hi = jnp.asarray(((lane >= half) & (lane < ROT_DIM)).astype(np.float32))
    return jnp.stack([freq, lo, hi] + [jnp.zeros((LANES,), F32)] * 5)


def _in0(x, positions, vec, w_in, pool_w, pool_scale, tm=512):
    bsz, seq, _ = x.shape
    n_out = 3 * A_WIDTH + POOL_WIDTH
    tab = _rope_table()
    seq_spec = lambda width: pl.BlockSpec((None, tm, width), lambda b, i: (b, i, 0))
    full = lambda shape: pl.BlockSpec(shape, lambda b, i: (0,) * len(shape))
    out_sds = jax.ShapeDtypeStruct((bsz, seq, A_WIDTH), BF16)
    return pl.pallas_call(
        _in0_body,
        grid=(bsz, seq // tm),
        in_specs=[seq_spec(D_MODEL), seq_spec(1),
                  pl.BlockSpec((None, 8, D_MODEL), lambda b, i: (b, 0, 0)),
                  full((D_MODEL, n_out)), full((8, LANES)),
                  full((len(POOL_WINDOWS), POOL_GROUP_DIM, POOL_GROUP_DIM)), full((1, POOL_WIDTH))],
        out_specs=[seq_spec(A_WIDTH)] * 4,
        out_shape=[out_sds] * 4,
        scratch_shapes=[pltpu.VMEM((tm + POOL_HALO, POOL_WIDTH), F32)],
        compiler_params=_params("arbitrary", "arbitrary"),
        name="l0_in_proj",
    )(x, positions.reshape(bsz, seq, 1), vec, w_in.astype(BF16), tab,
      pool_w.astype(BF16), pool_scale.reshape(1, POOL_WIDTH))


def _attn_body(q_ref, kp_ref, kc_ref, vp_ref, vc_ref, o_ref, lse_ref):
    i = pl.program_id(2)
    qi = lax.broadcasted_iota(jnp.int32, (BAND, BAND), 0)
    kj = lax.broadcasted_iota(jnp.int32, (BAND, BAND), 1)
    mask_prev = (kj >= qi) & (i > 0)
    mask_cur = kj <= qi
    lane = lax.broadcasted_iota(jnp.int32, (BAND, LANES), 1)
    first = lane < A_HEAD_DIM
    lse_tile = jnp.zeros((BAND, LANES), F32)
    for g in range(A_WIDTH // LANES):
        cols = slice(g * LANES, (g + 1) * LANES)
        qp = q_ref[:, cols]
        kp, kc, vp, vc = kp_ref[:, cols], kc_ref[:, cols], vp_ref[:, cols], vc_ref[:, cols]
        outs = []
        for half in range(2):
            sel = first if half == 0 else jnp.logical_not(first)
            qh = jnp.where(sel, qp, jnp.zeros_like(qp))
            sp = jnp.where(mask_prev, _dot_nt(qh, kp), NEG_INF)
            sc = jnp.where(mask_cur, _dot_nt(qh, kc), NEG_INF)
            m = jnp.maximum(jnp.max(sp, axis=-1, keepdims=True), jnp.max(sc, axis=-1, keepdims=True))
            pp = jnp.exp(sp - m)
            pc = jnp.exp(sc - m)
            den = jnp.sum(pp, axis=-1, keepdims=True) + jnp.sum(pc, axis=-1, keepdims=True)
            o = (_dot(pp.astype(BF16), vp) + _dot(pc.astype(BF16), vc)) / den
            outs.append(o)
            lse = m + jnp.log(den)
            lse_tile = jnp.where(lane == 2 * g + half, lse, lse_tile)
        o_ref[:, cols] = jnp.where(first, outs[0], outs[1]).astype(BF16)
    lse_ref[...] = lse_tile


def _banded_attention(q, k, v, dilation):
    bsz, seq, _ = q.shape
    n_sub = seq // dilation
    fold = lambda t: t.reshape(bsz, n_sub, dilation * A_WIDTH)
    cur = pl.BlockSpec((None, BAND, A_WIDTH), lambda b, r, i: (b, i, r))
    prev = pl.BlockSpec((None, BAND, A_WIDTH), lambda b, r, i: (b, jnp.maximum(i - 1, 0), r))
    o, lse = pl.pallas_call(
        _attn_body,
        grid=(bsz, dilation, n_sub // BAND),
        in_specs=[cur, prev, cur, prev, cur],
        out_specs=[cur, pl.BlockSpec((None, BAND, LANES), lambda b, r, i: (b, i, r))],
        out_shape=[jax.ShapeDtypeStruct((bsz, n_sub, dilation * A_WIDTH), BF16),
                   jax.ShapeDtypeStruct((bsz, n_sub, dilation * LANES), F32)],
        compiler_params=_params("arbitrary", "arbitrary", "arbitrary"),
        name=f"dilated_attn_d{dilation}",
    )(fold(q), fold(k), fold(k), fold(v), fold(v))
    return o.reshape(bsz, seq, A_WIDTH), lse.reshape(bsz, seq, LANES)


def _out0_body(o1_ref, o4_ref, o16_ref, l1_ref, l4_ref, l16_ref, ob_ref, x_ref, vec_ref, w_ref,
               out_ref):
    tm = x_ref.shape[0]
    la, lb, lc = l1_ref[...], l4_ref[...], l16_ref[...]
    m = jnp.maximum(jnp.maximum(la, lb), lc)
    ea, eb, ec = jnp.exp(la - m), jnp.exp(lb - m), jnp.exp(lc - m)
    tot = ea + eb + ec
    weights = (ea / tot, eb / tot, ec / tot)
    lane = lax.broadcasted_iota(jnp.int32, (tm, LANES), 1)
    first = lane < A_HEAD_DIM
    pieces = []
    for g in range(A_WIDTH // LANES):
        cols = slice(g * LANES, (g + 1) * LANES)
        acc = jnp.zeros((tm, LANES), F32)
        for wt, o_ref in zip(weights, (o1_ref, o4_ref, o16_ref)):
            w_pair = jnp.where(first,
                               jnp.broadcast_to(wt[:, 2 * g:2 * g + 1], (tm, LANES)),
                               jnp.broadcast_to(wt[:, 2 * g + 1:2 * g + 2], (tm, LANES)))
            acc = acc + w_pair * o_ref[:, cols].astype(F32)
        pieces.append(acc.astype(BF16))
    o_a = jnp.concatenate(pieces, axis=-1)
    y = _dot(o_a, w_ref[0:A_WIDTH, :]) + _dot(ob_ref[...], w_ref[A_WIDTH:A_WIDTH + POOL_WIDTH, :])
    out_ref[...] = _post_residual(x_ref[...], y, vec_ref[0:1, :], vec_ref[1:2, :])


def _out0(os, lses, ob, x, vec, w_out, tm=512):
    bsz, seq, _ = x.shape
    seq_spec = lambda width: pl.BlockSpec((None, tm, width), lambda b, i: (b, i, 0))
    return pl.pallas_call(
        _out0_body,
        grid=(bsz, seq // tm),
        in_specs=[seq_spec(A_WIDTH)] * 3 + [seq_spec(LANES)] * 3 + [seq_spec(POOL_WIDTH),
                  seq_spec(D_MODEL), pl.BlockSpec((None, 8, D_MODEL), lambda b, i: (b, 0, 0)),
                  pl.BlockSpec((A_WIDTH + POOL_WIDTH, D_MODEL), lambda b, i: (0, 0))],
        out_specs=seq_spec(D_MODEL),
        out_shape=jax.ShapeDtypeStruct((bsz, seq, D_MODEL), F32),
        compiler_params=_params("arbitrary", "arbitrary"),
        name="l0_out_proj",
    )(*os, *lses, ob, x, vec, w_out.astype(BF16))


def _ffn_body(x_ref, vec_ref, wg_ref, wu_ref, wd_ref, out_ref, hbuf, acc):
    f = pl.program_id(1)

    @pl.when(f == 0)
    def _():
        hbuf[...] = _norm_mod(x_ref[...], vec_ref[0:1, :], vec_ref[1:2, :],
                              vec_ref[2:3, :]).astype(BF16)
        acc[...] = jnp.zeros_like(acc)

    h = hbuf[...]
    act = _silu(_dot(h, wg_ref[...])) * _dot(h, wu_ref[...])
    acc[...] += _dot(act.astype(BF16), wd_ref[...])

    @pl.when(f == pl.num_programs(1) - 1)
    def _():
        out_ref[...] = _post_residual(x_ref[...], acc[...], vec_ref[3:4, :], vec_ref[4:5, :])


def _ffn(x, vec, w_gate, w_up, w_down, tm=1024, tf=512):
    bsz, seq, _ = x.shape
    tiles_per_seq = seq // tm
    xt = x.reshape(bsz * seq, D_MODEL)
    row = pl.BlockSpec((tm, D_MODEL), lambda i, f: (i, 0))
    out = pl.pallas_call(
        _ffn_body,
        grid=(bsz * seq // tm, FFN_DIM // tf),
        in_specs=[row, pl.BlockSpec((None, 8, D_MODEL), lambda i, f: (i // tiles_per_seq, 0, 0)),
                  pl.BlockSpec((D_MODEL, tf), lambda i, f: (0, f)),
                  pl.BlockSpec((D_MODEL, tf), lambda i, f: (0, f)),
                  pl.BlockSpec((tf, D_MODEL), lambda i, f: (f, 0))],
        out_specs=row,
        out_shape=jax.ShapeDtypeStruct((bsz * seq, D_MODEL), F32),
        scratch_shapes=[pltpu.VMEM((tm, D_MODEL), BF16), pltpu.VMEM((tm, D_MODEL), F32)],
        compiler_params=_params("arbitrary", "arbitrary"),
        name="l0_swiglu",
    )(xt, vec, w_gate.astype(BF16), w_up.astype(BF16), w_down.astype(BF16))
    return out.reshape(bsz, seq, D_MODEL)


def _split3(a):
    a1 = a.astype(BF16)
    r1 = a - a1.astype(F32)
    a2 = r1.astype(BF16)
    a3 = (r1 - a2.astype(F32)).astype(BF16)
    return a1, a2, a3


def _in1_body(x_ref, vec_ref, w_ref, wba_ref, cw_ref, hp_ref,
              q_ref, k_ref, v_ref, gate_ref, bg_ref, cbuf):
    i = pl.program_id(1)
    tm = x_ref.shape[0]

    @pl.when(i == 0)
    def _():
        cbuf[:, 0:CONV_HALO, :] = jnp.zeros((3, CONV_HALO, DN_WIDTH), F32)

    hf = _norm_mod(x_ref[...], vec_ref[0:1, :], vec_ref[1:2, :], vec_ref[2:3, :])
    h = hf.astype(BF16)

    for idx, dst in enumerate((q_ref, k_ref, v_ref)):
        cols = slice(idx * DN_WIDTH, (idx + 1) * DN_WIDTH)
        cbuf[idx, CONV_HALO:CONV_HALO + tm, :] = _dot(h, w_ref[:, cols])
        y = jnp.zeros((tm, DN_WIDTH), F32)
        for j in range(CONV_WIDTH):
            off = CONV_HALO - (CONV_WIDTH - 1) + j
            y = y + cw_ref[j:j + 1, cols] * cbuf[idx, off:off + tm, :]
        y = _silu(y)
        cbuf[idx, 0:CONV_HALO, :] = cbuf[idx, tm:tm + CONV_HALO, :]
        if idx < 2:
            scale = DN_HEAD_DIM ** -0.5 if idx == 0 else 1.0
            for hd in range(DN_HEADS):
                hc = slice(hd * DN_HEAD_DIM, (hd + 1) * DN_HEAD_DIM)
                blk = y[:, hc]
                ss = jnp.sum(blk * blk, axis=-1, keepdims=True)
                dst[:, hc] = (blk * (lax.rsqrt(ss + EPS) * scale)).astype(BF16)
        else:
            dst[...] = y.astype(BF16)

    gate_ref[...] = _silu(_dot(h, w_ref[:, 3 * DN_WIDTH:4 * DN_WIDTH])).astype(BF16)

    ba = _dot(hf, wba_ref[...], precision=HIGHEST)
    beta = jax.nn.sigmoid(ba)
    z = ba + hp_ref[1:2, :]
    softplus = jnp.maximum(z, 0.0) + jnp.log1p(jnp.exp(-jnp.abs(z)))
    g = -jnp.exp(hp_ref[0:1, :]) * softplus
    r = lax.broadcasted_iota(jnp.int32, (tm, tm), 0)
    c = lax.broadcasted_iota(jnp.int32, (tm, tm), 1)
    tri = jnp.where((r // CHUNK == c // CHUNK) & (c <= r), 1.0, 0.0).astype(BF16)
    g1, g2, g3 = _split3(g)
    gc = _dot(tri, g1) + _dot(tri, g2) + _dot(tri, g3)
    lane = lax.broadcasted_iota(jnp.int32, (tm, LANES), 1)
    bg_ref[...] = jnp.where(lane < DN_HEADS, beta, gc)


def _in1(x, vec, w_in, conv_w, a_log, dt_bias, tm=256):
    bsz, seq, _ = x.shape
    w_main = w_in[:, :4 * DN_WIDTH].astype(BF16)
    w_ba = jnp.zeros((D_MODEL, LANES), F32).at[:, :2 * DN_HEADS].set(w_in[:, 4 * DN_WIDTH:])
    hp = jnp.zeros((8, LANES), F32)
    hp = hp.at[0, DN_HEADS:2 * DN_HEADS].set(a_log).at[1, DN_HEADS:2 * DN_HEADS].set(dt_bias)
    seq_spec = lambda width: pl.BlockSpec((None, tm, width), lambda b, i: (b, i, 0))
    full = lambda shape: pl.BlockSpec(shape, lambda b, i: (0,) * len(shape))
    wide = jax.ShapeDtypeStruct((bsz, seq, DN_WIDTH), BF16)
    return pl.pallas_call(
        _in1_body,
        grid=(bsz, seq // tm),
        in_specs=[seq_spec(D_MODEL), pl.BlockSpec((None, 8, D_MODEL), lambda b, i: (b, 0, 0)),
                  full((D_MODEL, 4 * DN_WIDTH)), full((D_MODEL, LANES)),
                  full((CONV_WIDTH, 3 * DN_WIDTH)), full((8, LANES))],
        out_specs=[seq_spec(DN_WIDTH)] * 4 + [seq_spec(LANES)],
        out_shape=[wide] * 4 + [jax.ShapeDtypeStruct((bsz, seq, LANES), F32)],
        scratch_shapes=[pltpu.VMEM((3, tm + CONV_HALO, DN_WIDTH), F32)],
        compiler_params=_params("arbitrary", "arbitrary"),
        name="l1_in_proj",
    )(x, vec, w_main, w_ba, conv_w, hp)


def _unit_lower_inverse(lmat, precision):
    n = lmat.shape[0]
    r = lax.broadcasted_iota(jnp.int32, (n, n), 0)
    c = lax.broadcasted_iota(jnp.int32, (n, n), 1)
    inv = jnp.where(r == c, 1.0, 0.0).astype(F32)
    size = 1
    while size < n:
        below = (r // (2 * size) == c // (2 * size)) & (r % (2 * size) >= size) & (c % (2 * size) < size)
        off = jnp.where(below, lmat, 0.0)
        inv = inv - _dot(_dot(inv, off, precision), inv, precision)
        size *= 2
    return inv


def _delta_body(q_ref, k_ref, v_ref, gate_ref, bg_ref, gcr_ref, ng_ref, o_ref, state, *,
                inv_precision):
    i = pl.program_id(1)
    n_chunks = q_ref.shape[0] // CHUNK

    @pl.when(i == 0)
    def _():
        state[...] = jnp.zeros_like(state)

    r = lax.broadcasted_iota(jnp.int32, (CHUNK, CHUNK), 0)
    c = lax.broadcasted_iota(jnp.int32, (CHUNK, CHUNK), 1)
    causal = r >= c
    strict = r > c

    def chunk_step(ci, carry):
        rows = pl.ds(pl.multiple_of(ci * CHUNK, CHUNK), CHUNK)
        bg = bg_ref[rows, :]
        gcr_all = gcr_ref[ci]
        for hd in range(DN_HEADS):
            hc = slice(hd * DN_HEAD_DIM, (hd + 1) * DN_HEAD_DIM)
            q = q_ref[rows, hc].astype(F32)
            k = k_ref[rows, hc].astype(F32)
            v = v_ref[rows, hc].astype(F32)
            beta = bg[:, hd:hd + 1]
            gcc = bg[:, DN_HEADS + hd:DN_HEADS + hd + 1]
            gcr = gcr_all[hd:hd + 1, :]
            decay = jnp.exp(jnp.where(causal, gcc - gcr, NEG_INF))
            kb = k * beta
            lmat = jnp.where(strict, _dot_nt(kb, k) * decay, 0.0)
            inv = _unit_lower_inverse(lmat, inv_precision)
            eg = jnp.exp(gcc)
            u = _dot(inv, v * beta)
            w = _dot(inv, kb * eg)
            attn = _dot_nt(q, k) * decay
            g_last = gcc[CHUNK - 1:CHUNK, :]
            s = state[hd]
            v_new = u - _dot(w, s)
            o = _dot(q * eg, s) + _dot(attn, v_new)
            state[hd] = s * jnp.exp(g_last) + _dot_tn(k * jnp.exp(g_last - gcc), v_new)
            ms = jnp.mean(o * o, axis=-1, keepdims=True)
            o = o * lax.rsqrt(ms + EPS) * ng_ref[0:1, :] * gate_ref[rows, hc].astype(F32)
            o_ref[rows, hc] = o.astype(BF16)
        return carry

    lax.fori_loop(0, n_chunks, chunk_step, 0)


def _delta(q, k, v, gate, bg, norm_g, block=256, inv_precision=HIGHEST):
    bsz, seq, _ = q.shape
    n = seq // CHUNK
    gc_rows = bg[:, :, DN_HEADS:2 * DN_HEADS].reshape(bsz, n, CHUNK, DN_HEADS).transpose(0, 1, 3, 2)
    seq_spec = lambda width: pl.BlockSpec((None, block, width), lambda b, i: (b, i, 0))
    return pl.pallas_call(
        functools.partial(_delta_body, inv_precision=inv_precision),
        grid=(bsz, seq // block),
        in_specs=[seq_spec(DN_WIDTH)] * 4 + [seq_spec(LANES),
                  pl.BlockSpec((None, block // CHUNK, DN_HEADS, CHUNK), lambda b, i: (b, i, 0, 0)),
                  pl.BlockSpec((1, DN_HEAD_DIM), lambda b, i: (0, 0))],
        out_specs=seq_spec(DN_WIDTH),
        out_shape=jax.ShapeDtypeStruct((bsz, seq, DN_WIDTH), BF16),
        scratch_shapes=[pltpu.VMEM((DN_HEADS, DN_HEAD_DIM, DN_HEAD_DIM), F32)],
        compiler_params=_params("arbitrary", "arbitrary"),
        name="gated_delta",
    )(q, k, v, gate, bg, gc_rows, norm_g.reshape(1, DN_HEAD_DIM))


def _out1_body(a_ref, x_ref, vec_ref, w_ref, out_ref):
    y = _dot(a_ref[...], w_ref[...])
    out_ref[...] = _post_residual(x_ref[...], y, vec_ref[0:1, :], vec_ref[1:2, :])


def _out1(a, x, vec, w_out, tm=512):
    bsz, seq, _ = x.shape
    seq_spec = lambda width: pl.BlockSpec((None, tm, width), lambda b, i: (b, i, 0))
    return pl.pallas_call(
        _out1_body,
        grid=(bsz, seq // tm),
        in_specs=[seq_spec(DN_WIDTH), seq_spec(D_MODEL),
                  pl.BlockSpec((None, 8, D_MODEL), lambda b, i: (b, 0, 0)),
                  pl.BlockSpec((DN_WIDTH, D_MODEL), lambda b, i: (0, 0))],
        out_specs=seq_spec(D_MODEL),
        out_shape=jax.ShapeDtypeStruct((bsz, seq, D_MODEL), F32),
        compiler_params=_params("arbitrary", "arbitrary"),
        name="l1_out_proj",
    )(a, x, vec, w_out.astype(BF16))


def _route(hf, rw):
    tm = hf.shape[0]
    lane = lax.broadcasted_iota(jnp.int32, (tm, LANES), 1)
    logits = jnp.where(lane < N_EXPERTS, _dot(hf, rw, precision=HIGHEST), NEG_INF)
    m1 = jnp.max(logits, axis=-1, keepdims=True)
    i1 = jnp.min(jnp.where(logits == m1, lane, LANES), axis=-1, keepdims=True)
    rest = jnp.where(lane == i1, NEG_INF, logits)
    m2 = jnp.max(rest, axis=-1, keepdims=True)
    i2 = jnp.min(jnp.where(rest == m2, lane, LANES), axis=-1, keepdims=True)
    e2 = jnp.exp(m2 - m1)
    w1 = 1.0 / (1.0 + e2)
    w2 = e2 / (1.0 + e2)
    return jnp.where(lane == i1, w1, 0.0) + jnp.where(lane == i2, w2, 0.0)


def _moe_body(x_ref, vec_ref, rw_ref, wg_ref, wu_ref, wd_ref, out_ref, hbuf, cw, acc):
    e = pl.program_id(1)
    f = pl.program_id(2)
    tm = x_ref.shape[0]

    @pl.when((e == 0) & (f == 0))
    def _():
        hf = _norm_mod(x_ref[...], vec_ref[0:1, :], vec_ref[1:2, :], vec_ref[2:3, :])
        hbuf[...] = hf.astype(BF16)
        cw[...] = _route(hf, rw_ref[...])
        acc[...] = jnp.zeros_like(acc)

    lane = lax.broadcasted_iota(jnp.int32, (tm, LANES), 1)
    ce = jnp.sum(jnp.where(lane == e, cw[...], 0.0), axis=-1, keepdims=True)
    h = hbuf[...]
    act = _silu(_dot(h, wg_ref[...])) * _dot(h, wu_ref[...]) * ce
    acc[...] += _dot(act.astype(BF16), wd_ref[...])

    @pl.when((e == pl.num_programs(1) - 1) & (f == pl.num_programs(2) - 1))
    def _():
        out_ref[...] = _post_residual(x_ref[...], acc[...], vec_ref[3:4, :], vec_ref[4:5, :])


def _moe(x, vec, router_w, w_gate, w_up, w_down, tm=1024, tf=512):
    bsz, seq, _ = x.shape
    tiles_per_seq = seq // tm
    xt = x.reshape(bsz * seq, D_MODEL)
    rw = jnp.zeros((D_MODEL, LANES), F32).at[:, :N_EXPERTS].set(router_w)
    row = pl.BlockSpec((tm, D_MODEL), lambda i, e, f: (i, 0))
    out = pl.pallas_call(
        _moe_body,
        grid=(bsz * seq // tm, N_EXPERTS, FFN_DIM // tf),
        in_specs=[row, pl.BlockSpec((None, 8, D_MODEL), lambda i, e, f: (i // tiles_per_seq, 0, 0)),
                  pl.BlockSpec((D_MODEL, LANES), lambda i, e, f: (0, 0)),
                  pl.BlockSpec((None, D_MODEL, tf), lambda i, e, f: (e, 0, f)),
                  pl.BlockSpec((None, D_MODEL, tf), lambda i, e, f: (e, 0, f)),
                  pl.BlockSpec((None, tf, D_MODEL), lambda i, e, f: (e, f, 0))],
        out_specs=row,
        out_shape=jax.ShapeDtypeStruct((bsz * seq, D_MODEL), F32),
        scratch_shapes=[pltpu.VMEM((tm, D_MODEL), BF16), pltpu.VMEM((tm, LANES), F32),
                        pltpu.VMEM((tm, D_MODEL), F32)],
        compiler_params=_params("arbitrary", "arbitrary", "arbitrary"),
        name="l1_moe",
    )(xt, vec, rw, w_gate.astype(BF16), w_up.astype(BF16), w_down.astype(BF16))
    return out.reshape(bsz, seq, D_MODEL)


def kernel(x, c, positions, ada_w, ada_b, mix_pre_g, mix_post_g, ffn_pre_g, ffn_post_g, even_w_in, even_pool_w, even_pool_scale, even_w_out, even_ffn_w_gate, even_ffn_w_up, even_ffn_w_down, odd_w_in, odd_conv_w, odd_a_log, odd_dt_bias, odd_norm_g, odd_w_out, odd_router_w, odd_moe_w_gate, odd_moe_w_up, odd_moe_w_down):
    bsz = x.shape[0]
    mod = _adaln_mod(c, ada_w, ada_b)
    sh1, sc1, gt1, sh2, sc2, gt2 = (mod[:, :, n * D_MODEL:(n + 1) * D_MODEL] for n in range(6))

    vec = _vec_rows([mix_pre_g[0], sc1[0], sh1[0]], bsz)
    q, k, v, o_b = _in0(x, positions, vec, even_w_in[0], even_pool_w[0], even_pool_scale[0])
    branches = [_banded_attention(q, k, v, d) for d in DILATIONS]
    vec = _vec_rows([mix_post_g[0], gt1[0]], bsz)
    x = _out0([o for o, _ in branches], [l for _, l in branches], o_b, x, vec, even_w_out[0])
    vec = _vec_rows([ffn_pre_g[0], sc2[0], sh2[0], ffn_post_g[0], gt2[0]], bsz)
    x = _ffn(x, vec, even_ffn_w_gate[0], even_ffn_w_up[0], even_ffn_w_down[0])

    vec = _vec_rows([mix_pre_g[1], sc1[1], sh1[1]], bsz)
    q, k, v, gate, bg = _in1(x, vec, odd_w_in[0], odd_conv_w[0], odd_a_log[0], odd_dt_bias[0])
    o = _delta(q, k, v, gate, bg, odd_norm_g[0])
    vec = _vec_rows([mix_post_g[1], gt1[1]], bsz)
    x = _out1(o, x, vec, odd_w_out[0])
    vec = _vec_rows([ffn_pre_g[1], sc2[1], sh2[1], ffn_post_g[1], gt2[1]], bsz)
    x = _moe(x, vec, odd_router_w[0], odd_moe_w_gate[0], odd_moe_w_up[0], odd_moe_w_down[0])
    return x
```

```python
import functools

import numpy as np
import jax
import jax.numpy as jnp
from jax import lax
from jax.experimental import pallas as pl
from jax.experimental.pallas import tpu as pltpu

F32 = jnp.float32
BF16 = jnp.bfloat16
HIGHEST = lax.Precision.HIGHEST

D_MODEL = 1024
DEPTH = 2
A_HEADS = 8
A_HEAD_DIM = 64
A_WIDTH = 512
DILATIONS = (1, 4, 16)
BAND = 128
ROT_DIM = 16
ROPE_THETA = 500000.0
POOL_WINDOWS = (2, 4, 8, 16)
POOL_GROUP_DIM = 128
POOL_WIDTH = 512
POOL_HALO = 16
DN_HEADS = 8
DN_HEAD_DIM = 128
DN_WIDTH = 1024
CONV_WIDTH = 4
CONV_HALO = 8
CHUNK = 64
FFN_DIM = 3584
N_EXPERTS = 8
EPS = 1e-6
LANES = 128
NEG_INF = float("-inf")

VMEM_LIMIT = 56 * 1024 * 1024


def _params(*sem):
    return pltpu.CompilerParams(dimension_semantics=sem, vmem_limit_bytes=VMEM_LIMIT)


def _dot(a, b, precision=None):
    return jnp.dot(a, b, preferred_element_type=F32, precision=precision)


def _dot_nt(a, b, precision=None):
    return lax.dot_general(a, b, (((1,), (1,)), ((), ())), preferred_element_type=F32,
                           precision=precision)


def _dot_tn(a, b, precision=None):
    return lax.dot_general(a, b, (((0,), (0,)), ((), ())), preferred_element_type=F32,
                           precision=precision)


def _silu(x):
    return x * jax.nn.sigmoid(x)


def _norm_mod(x, g, sc, sh):
    ms = jnp.mean(x * x, axis=-1, keepdims=True)
    return x * lax.rsqrt(ms + EPS) * g * (1.0 + sc) + sh


def _post_residual(x, y, g, gt):
    ms = jnp.mean(y * y, axis=-1, keepdims=True)
    return x + gt * (y * lax.rsqrt(ms + EPS) * g)


def _mod_body(c_ref, w_ref, b_ref, o_ref):
    c = c_ref[...]
    o_ref[...] = _dot(_silu(c), w_ref[...], precision=HIGHEST) + b_ref[...]


def _adaln_mod(c, ada_w, ada_b):
    bsz = c.shape[0]
    rows = 8
    c_pad = jnp.zeros((rows, D_MODEL), F32).at[:bsz].set(c)
    tn = 1536
    out = pl.pallas_call(
        _mod_body,
        grid=(DEPTH, 6 * D_MODEL // tn),
        in_specs=[pl.BlockSpec((rows, D_MODEL), lambda l, n: (0, 0)),
                  pl.BlockSpec((None, D_MODEL, tn), lambda l, n: (l, 0, n)),
                  pl.BlockSpec((None, 1, tn), lambda l, n: (l, 0, n))],
        out_specs=pl.BlockSpec((None, rows, tn), lambda l, n: (l, 0, n)),
        out_shape=jax.ShapeDtypeStruct((DEPTH, rows, 6 * D_MODEL), F32),
        compiler_params=_params("arbitrary", "arbitrary"),
        name="adaln_mod",
    )(c_pad, ada_w, ada_b.reshape(DEPTH, 1, 6 * D_MODEL))
    return out[:, :bsz]


def _vec_rows(rows, bsz):
    rows = [jnp.broadcast_to(r.astype(F32), (bsz, D_MODEL)) for r in rows]
    rows = rows + [jnp.zeros((bsz, D_MODEL), F32)] * (8 - len(rows))
    return jnp.stack(rows, axis=1)


def _rope(t, cosv, sin_lo, sin_hi):
    return (t * cosv + pltpu.roll(t, LANES - ROT_DIM // 2, axis=1) * sin_lo
            + pltpu.roll(t, ROT_DIM // 2, axis=1) * sin_hi)


def _store_folded(nat, refs):
    tm = nat.shape[1]
    for d, ref in zip(DILATIONS, refs):
        for r in range(d):
            rows = pl.ds(r, tm // d, stride=d) if d > 1 else slice(None)
            for t in range(A_WIDTH // LANES):
                col = r * A_WIDTH + t * LANES
                ref[:, col:col + LANES] = nat[t, rows, :].astype(BF16)


def _in0_body(x_ref, pos_ref, vec_ref, w_ref, tab_ref, pw_ref, ps_ref,
              q1_ref, q4_ref, q16_ref, k1_ref, k4_ref, k16_ref, v1_ref, v4_ref, v16_ref, ob_ref,
              ubuf, nat):
    i = pl.program_id(1)
    tm = x_ref.shape[0]

    @pl.when(i == 0)
    def _():
        ubuf[0:POOL_HALO, :] = jnp.zeros((POOL_HALO, POOL_WIDTH), F32)

    h = _norm_mod(x_ref[...], vec_ref[0:1, :], vec_ref[1:2, :], vec_ref[2:3, :]).astype(BF16)

    ang = pos_ref[...].astype(F32) * tab_ref[0:1, :]
    cosv = jnp.cos(ang)
    sinv = jnp.sin(ang)
    sin_lo = -sinv * tab_ref[1:2, :]
    sin_hi = sinv * tab_ref[2:3, :]

    pq = _dot(h, w_ref[:, 0:A_WIDTH])
    for t in range(A_WIDTH // LANES):
        cols = slice(t * LANES, (t + 1) * LANES)
        nat[t] = _rope(pq[:, cols], cosv, sin_lo, sin_hi) * (A_HEAD_DIM ** -0.5)
    _store_folded(nat, (q1_ref, q4_ref, q16_ref))
    pk = _dot(h, w_ref[:, A_WIDTH:2 * A_WIDTH])
    for t in range(A_WIDTH // LANES):
        cols = slice(t * LANES, (t + 1) * LANES)
        nat[t] = _rope(pk[:, cols], cosv, sin_lo, sin_hi)
    _store_folded(nat, (k1_ref, k4_ref, k16_ref))
    pv = _dot(h, w_ref[:, 2 * A_WIDTH:3 * A_WIDTH])
    for t in range(A_WIDTH // LANES):
        nat[t] = pv[:, t * LANES:(t + 1) * LANES]
    _store_folded(nat, (v1_ref, v4_ref, v16_ref))

    ubuf[POOL_HALO:POOL_HALO + tm, :] = _dot(h, w_ref[:, 3 * A_WIDTH:3 * A_WIDTH + POOL_WIDTH])
    tpos = i * tm + lax.broadcasted_iota(jnp.int32, (tm, 1), 0)
    for g, win in enumerate(POOL_WINDOWS):
        cols = slice(g * POOL_GROUP_DIM, (g + 1) * POOL_GROUP_DIM)
        cur = ubuf[POOL_HALO:POOL_HALO + tm, cols]
        acc = cur
        for j in range(1, win):
            acc = acc + ubuf[POOL_HALO - j:POOL_HALO - j + tm, cols]
        cnt = jnp.minimum(tpos + 1, win).astype(F32)
        pooled = acc / cnt - cur
        mixed = _dot(pooled.astype(BF16), pw_ref[g]) * ps_ref[0:1, cols]
        ob_ref[:, cols] = mixed.astype(BF16)
    ubuf[0:POOL_HALO, :] = ubuf[tm:tm + POOL_HALO, :]


def _rope_table():
    lane = np.arange(LANES) % A_HEAD_DIM
    half = ROT_DIM // 2
    inv_freq = ROPE_THETA ** (-jnp.arange(0, ROT_DIM, 2, dtype=F32) / ROT_DIM)
    freq = jnp.where(jnp.asarray(lane < ROT_DIM), inv_freq[jnp.asarray(lane % half)], 0.0)
    lo = jnp.asarray((lane < half).astype(np.float32))
    hi = jnp.asarray(((lane >= half) & (lane < ROT_DIM)).astype(np.float32))
    return jnp.stack([freq, lo, hi] + [jnp.zeros((LANES,), F32)] * 5)


def _in0(x, positions, vec, w_in, pool_w, pool_scale, tm=512):
    bsz, seq, _ = x.shape
    n_out = 3 * A_WIDTH + POOL_WIDTH
    tab = _rope_table()
    seq_spec = lambda width: pl.BlockSpec((None, tm, width), lambda b, i: (b, i, 0))
    full = lambda shape: pl.BlockSpec(shape, lambda b, i: (0,) * len(shape))
    folded_spec = lambda d: pl.BlockSpec((None, tm // d, d * A_WIDTH), lambda b, i: (b, i, 0))
    folded_sds = lambda d: jax.ShapeDtypeStruct((bsz, seq // d, d * A_WIDTH), BF16)
    outs = pl.pallas_call(
        _in0_body,
        grid=(bsz, seq // tm),
        in_specs=[seq_spec(D_MODEL), seq_spec(1),
                  pl.BlockSpec((None, 8, D_MODEL), lambda b, i: (b, 0, 0)),
                  full((D_MODEL, n_out)), full((8, LANES)),
                  full((len(POOL_WINDOWS), POOL_GROUP_DIM, POOL_GROUP_DIM)), full((1, POOL_WIDTH))],
        out_specs=[folded_spec(d) for d in DILATIONS] * 3 + [seq_spec(POOL_WIDTH)],
        out_shape=[folded_sds(d) for d in DILATIONS] * 3 + [folded_sds(1)],
        scratch_shapes=[pltpu.VMEM((tm + POOL_HALO, POOL_WIDTH), F32),
                        pltpu.VMEM((A_WIDTH // LANES, tm, LANES), F32)],
        compiler_params=_params("arbitrary", "arbitrary"),
        name="l0_in_proj",
    )(x, positions.reshape(bsz, seq, 1), vec, w_in.astype(BF16), tab,
      pool_w.astype(BF16), pool_scale.reshape(1, POOL_WIDTH))
    n = len(DILATIONS)
    return outs[0:n], outs[n:2 * n], outs[2 * n:3 * n], outs[3 * n]


def _attn_body(q_ref, kp_ref, kc_ref, vp_ref, vc_ref, o_ref, lse_ref):
    i = pl.program_id(2)
    qi = lax.broadcasted_iota(jnp.int32, (BAND, BAND), 0)
    kj = lax.broadcasted_iota(jnp.int32, (BAND, BAND), 1)
    mask_prev = (kj >= qi) & (i > 0)
    mask_cur = kj <= qi
    lane = lax.broadcasted_iota(jnp.int32, (BAND, LANES), 1)
    first = lane < A_HEAD_DIM
    pairs = range(A_WIDTH // LANES)
    heads = [(g, half) for g in pairs for half in range(2)]
    cols = [slice(g * LANES, (g + 1) * LANES) for g in pairs]
    qp = [q_ref[:, cs] for cs in cols]
    kp, kc = [kp_ref[:, cs] for cs in cols], [kc_ref[:, cs] for cs in cols]
    vp, vc = [vp_ref[:, cs] for cs in cols], [vc_ref[:, cs] for cs in cols]
    qh = [jnp.where(first if half == 0 else jnp.logical_not(first), qp[g], jnp.zeros_like(qp[g]))
          for g, half in heads]
    sp = [jnp.where(mask_prev, _dot_nt(q, kp[g]), NEG_INF) for q, (g, _) in zip(qh, heads)]
    sc = [jnp.where(mask_cur, _dot_nt(q, kc[g]), NEG_INF) for q, (g, _) in zip(qh, heads)]
    m = [jnp.maximum(jnp.max(a, axis=-1, keepdims=True), jnp.max(b, axis=-1, keepdims=True))
         for a, b in zip(sp, sc)]
    pp = [jnp.exp(a - mm) for a, mm in zip(sp, m)]
    pc = [jnp.exp(a - mm) for a, mm in zip(sc, m)]
    den = [jnp.sum(a, axis=-1, keepdims=True) + jnp.sum(b, axis=-1, keepdims=True)
           for a, b in zip(pp, pc)]
    outs = [(_dot(a.astype(BF16), vp[g]) + _dot(b.astype(BF16), vc[g])) / d
            for a, b, d, (g, _) in zip(pp, pc, den, heads)]
    lse_tile = jnp.zeros((BAND, LANES), F32)
    for idx, (mm, d) in enumerate(zip(m, den)):
        lse_tile = jnp.where(lane == idx, mm + jnp.log(d), lse_tile)
    for g in pairs:
        o_ref[:, cols[g]] = jnp.where(first, outs[2 * g], outs[2 * g + 1]).astype(BF16)
    lse_ref[...] = lse_tile


def _banded_attention(q, k, v, dilation):
    bsz, n_sub, _ = q.shape
    cur = pl.BlockSpec((None, BAND, A_WIDTH), lambda b, r, i: (b, i, r))
    prev = pl.BlockSpec((None, BAND, A_WIDTH), lambda b, r, i: (b, jnp.maximum(i - 1, 0), r))
    o, lse = pl.pallas_call(
        _attn_body,
        grid=(bsz, dilation, n_sub // BAND),
        in_specs=[cur, prev, cur, prev, cur],
        out_specs=[cur, pl.BlockSpec((None, BAND, LANES), lambda b, r, i: (b, i, r))],
        out_shape=[jax.ShapeDtypeStruct((bsz, n_sub, dilation * A_WIDTH), BF16),
                   jax.ShapeDtypeStruct((bsz, n_sub, dilation * LANES), F32)],
        compiler_params=_params("arbitrary", "arbitrary", "arbitrary"),
        name=f"dilated_attn_d{dilation}",
    )(q, k, k, v, v)
    return o, lse


def _out0_body(o1_ref, o4_ref, o16_ref, l1_ref, l4_ref, l16_ref, ob_ref, x_ref, vec_ref, w_ref,
               out_ref, o_nat, l_nat):
    tm = x_ref.shape[0]
    for slot, (d, o_ref, l_ref) in enumerate(zip(DILATIONS, (o1_ref, o4_ref, o16_ref),
                                                  (l1_ref, l4_ref, l16_ref))):
        for r in range(d):
            rows = pl.ds(r, tm // d, stride=d) if d > 1 else slice(None)
            for t in range(A_WIDTH // LANES):
                col = r * A_WIDTH + t * LANES
                o_nat[slot, t, rows, :] = o_ref[:, col:col + LANES].astype(F32)
            l_nat[slot, rows, :] = l_ref[:, r * LANES:(r + 1) * LANES]
    la, lb, lc = l_nat[0], l_nat[1], l_nat[2]
    m = jnp.maximum(jnp.maximum(la, lb), lc)
    ea, eb, ec = jnp.exp(la - m), jnp.exp(lb - m), jnp.exp(lc - m)
    tot = ea + eb + ec
    weights = (ea / tot, eb / tot, ec / tot)
    lane = lax.broadcasted_iota(jnp.int32, (tm, LANES), 1)
    first = lane < A_HEAD_DIM
    pieces = []
    for g in range(A_WIDTH // LANES):
        cols = slice(g * LANES, (g + 1) * LANES)
        acc = jnp.zeros((tm, LANES), F32)
        for slot, wt in enumerate(weights):
            w_pair = jnp.where(first,
                               jnp.broadcast_to(wt[:, 2 * g:2 * g + 1], (tm, LANES)),
                               jnp.broadcast_to(wt[:, 2 * g + 1:2 * g + 2], (tm, LANES)))
            acc = acc + w_pair * o_nat[slot, g]
        pieces.append(acc.astype(BF16))
    o_a = jnp.concatenate(pieces, axis=-1)
    y = _dot(o_a, w_ref[0:A_WIDTH, :]) + _dot(ob_ref[...], w_ref[A_WIDTH:A_WIDTH + POOL_WIDTH, :])
    out_ref[...] = _post_residual(x_ref[...], y, vec_ref[0:1, :], vec_ref[1:2, :])


def _out0(os, lses, ob, x, vec, w_out, tm=512):
    bsz, seq, _ = x.shape
    seq_spec = lambda width: pl.BlockSpec((None, tm, width), lambda b, i: (b, i, 0))
    folded = lambda width: [pl.BlockSpec((None, tm // d, d * width), lambda b, i: (b, i, 0))
                            for d in DILATIONS]
    n = len(DILATIONS)
    return pl.pallas_call(
        _out0_body,
        grid=(bsz, seq // tm),
        in_specs=folded(A_WIDTH) + folded(LANES) + [seq_spec(POOL_WIDTH),
                  seq_spec(D_MODEL), pl.BlockSpec((None, 8, D_MODEL), lambda b, i: (b, 0, 0)),
                  pl.BlockSpec((A_WIDTH + POOL_WIDTH, D_MODEL), lambda b, i: (0, 0))],
        out_specs=seq_spec(D_MODEL),
        out_shape=jax.ShapeDtypeStruct((bsz, seq, D_MODEL), F32),
        scratch_shapes=[pltpu.VMEM((n, A_WIDTH // LANES, tm, LANES), F32),
                        pltpu.VMEM((n, tm, LANES), F32)],
        compiler_params=_params("arbitrary", "arbitrary"),
        name="l0_out_proj",
    )(*os, *lses, ob, x, vec, w_out.astype(BF16))


def _ffn_body(x_ref, vec_ref, wg_ref, wu_ref, wd_ref, out_ref, hbuf, acc):
    f = pl.program_id(1)

    @pl.when(f == 0)
    def _():
        hbuf[...] = _norm_mod(x_ref[...], vec_ref[0:1, :], vec_ref[1:2, :],
                              vec_ref[2:3, :]).astype(BF16)
        acc[...] = jnp.zeros_like(acc)

    h = hbuf[...]
    act = _silu(_dot(h, wg_ref[...])) * _dot(h, wu_ref[...])
    acc[...] += _dot(act.astype(BF16), wd_ref[...])

    @pl.when(f == pl.num_programs(1) - 1)
    def _():
        out_ref[...] = _post_residual(x_ref[...], acc[...], vec_ref[3:4, :], vec_ref[4:5, :])


def _ffn(x, vec, w_gate, w_up, w_down, tm=1024, tf=512):
    bsz, seq, _ = x.shape
    tiles_per_seq = seq // tm
    xt = x.reshape(bsz * seq, D_MODEL)
    row = pl.BlockSpec((tm, D_MODEL), lambda i, f: (i, 0))
    out = pl.pallas_call(
        _ffn_body,
        grid=(bsz * seq // tm, FFN_DIM // tf),
        in_specs=[row, pl.BlockSpec((None, 8, D_MODEL), lambda i, f: (i // tiles_per_seq, 0, 0)),
                  pl.BlockSpec((D_MODEL, tf), lambda i, f: (0, f)),
                  pl.BlockSpec((D_MODEL, tf), lambda i, f: (0, f)),
                  pl.BlockSpec((tf, D_MODEL), lambda i, f: (f, 0))],
        out_specs=row,
        out_shape=jax.ShapeDtypeStruct((bsz * seq, D_MODEL), F32),
        scratch_shapes=[pltpu.VMEM((tm, D_MODEL), BF16), pltpu.VMEM((tm, D_MODEL), F32)],
        compiler_params=_params("arbitrary", "arbitrary"),
        name="l0_swiglu",
    )(xt, vec, w_gate.astype(BF16), w_up.astype(BF16), w_down.astype(BF16))
    return out.reshape(bsz, seq, D_MODEL)


def _split3(a):
    a1 = a.astype(BF16)
    r1 = a - a1.astype(F32)
    a2 = r1.astype(BF16)
    a3 = (r1 - a2.astype(F32)).astype(BF16)
    return a1, a2, a3


def _in1_body(x_ref, vec_ref, w_ref, wba_ref, cw_ref, hp_ref,
              q_ref, k_ref, v_ref, gate_ref, bg_ref, cbuf):
    i = pl.program_id(1)
    tm = x_ref.shape[0]

    @pl.when(i == 0)
    def _():
        cbuf[:, 0:CONV_HALO, :] = jnp.zeros((3, CONV_HALO, DN_WIDTH), F32)

    hf = _norm_mod(x_ref[...], vec_ref[0:1, :], vec_ref[1:2, :], vec_ref[2:3, :])
    h = hf.astype(BF16)

    for idx, dst in enumerate((q_ref, k_ref, v_ref)):
        cols = slice(idx * DN_WIDTH, (idx + 1) * DN_WIDTH)
        cbuf[idx, CONV_HALO:CONV_HALO + tm, :] = _dot(h, w_ref[:, cols])
        y = jnp.zeros((tm, DN_WIDTH), F32)
        for j in range(CONV_WIDTH):
            off = CONV_HALO - (CONV_WIDTH - 1) + j
            y = y + cw_ref[j:j + 1, cols] * cbuf[idx, off:off + tm, :]
        y = _silu(y)
        cbuf[idx, 0:CONV_HALO, :] = cbuf[idx, tm:tm + CONV_HALO, :]
        if idx < 2:
            scale = DN_HEAD_DIM ** -0.5 if idx == 0 else 1.0
            for hd in range(DN_HEADS):
                hc = slice(hd * DN_HEAD_DIM, (hd + 1) * DN_HEAD_DIM)
                blk = y[:, hc]
                ss = jnp.sum(blk * blk, axis=-1, keepdims=True)
                dst[:, hc] = (blk * (lax.rsqrt(ss + EPS) * scale)).astype(BF16)
        else:
            dst[...] = y.astype(BF16)

    gate_ref[...] = _silu(_dot(h, w_ref[:, 3 * DN_WIDTH:4 * DN_WIDTH])).astype(BF16)

    ba = _dot(hf, wba_ref[...], precision=HIGHEST)
    beta = jax.nn.sigmoid(ba)
    z = ba + hp_ref[1:2, :]
    softplus = jnp.maximum(z, 0.0) + jnp.log1p(jnp.exp(-jnp.abs(z)))
    g = -jnp.exp(hp_ref[0:1, :]) * softplus
    r = lax.broadcasted_iota(jnp.int32, (tm, tm), 0)
    c = lax.broadcasted_iota(jnp.int32, (tm, tm), 1)
    tri = jnp.where((r // CHUNK == c // CHUNK) & (c <= r), 1.0, 0.0).astype(BF16)
    g1, g2, g3 = _split3(g)
    gc = _dot(tri, g1) + _dot(tri, g2) + _dot(tri, g3)
    lane = lax.broadcasted_iota(jnp.int32, (tm, LANES), 1)
    bg_ref[...] = jnp.where(lane < DN_HEADS, beta, gc)


def _in1(x, vec, w_in, conv_w, a_log, dt_bias, tm=256):
    bsz, seq, _ = x.shape
    w_main = w_in[:, :4 * DN_WIDTH].astype(BF16)
    w_ba = jnp.zeros((D_MODEL, LANES), F32).at[:, :2 * DN_HEADS].set(w_in[:, 4 * DN_WIDTH:])
    hp = jnp.zeros((8, LANES), F32)
    hp = hp.at[0, DN_HEADS:2 * DN_HEADS].set(a_log).at[1, DN_HEADS:2 * DN_HEADS].set(dt_bias)
    seq_spec = lambda width: pl.BlockSpec((None, tm, width), lambda b, i: (b, i, 0))
    full = lambda shape: pl.BlockSpec(shape, lambda b, i: (0,) * len(shape))
    wide = jax.ShapeDtypeStruct((bsz, seq, DN_WIDTH), BF16)
    return pl.pallas_call(
        _in1_body,
        grid=(bsz, seq // tm),
        in_specs=[seq_spec(D_MODEL), pl.BlockSpec((None, 8, D_MODEL), lambda b, i: (b, 0, 0)),
                  full((D_MODEL, 4 * DN_WIDTH)), full((D_MODEL, LANES)),
                  full((CONV_WIDTH, 3 * DN_WIDTH)), full((8, LANES))],
        out_specs=[seq_spec(DN_WIDTH)] * 4 + [seq_spec(LANES)],
        out_shape=[wide] * 4 + [jax.ShapeDtypeStruct((bsz, seq, LANES), F32)],
        scratch_shapes=[pltpu.VMEM((3, tm + CONV_HALO, DN_WIDTH), F32)],
        compiler_params=_params("arbitrary", "arbitrary"),
        name="l1_in_proj",
    )(x, vec, w_main, w_ba, conv_w, hp)


def _unit_lower_inverses(lmats):
    n = lmats[0].shape[0]
    r = lax.broadcasted_iota(jnp.int32, (n, n), 0)
    c = lax.broadcasted_iota(jnp.int32, (n, n), 1)

    def below(size):
        return (r // (2 * size) == c // (2 * size)) & (r % (2 * size) >= size) & (c % (2 * size) < size)

    eye = jnp.where(r == c, 1.0, 0.0).astype(F32)
    first = below(1)
    invs = [eye - jnp.where(first, lm, 0.0) for lm in lmats]
    size = 2
    while size < n:
        mask = below(size)
        offs = [jnp.where(mask, lm, 0.0).astype(BF16) for lm in lmats]
        inv16 = [x.astype(BF16) for x in invs]
        xc = [_dot(x, o).astype(BF16) for x, o in zip(inv16, offs)]
        invs = [x - _dot(t, x16) for x, t, x16 in zip(invs, xc, inv16)]
        size *= 2
    return invs


def _delta_body(q_ref, k_ref, v_ref, gate_ref, bg_ref, gcr_ref, ng_ref, o_ref,
                state, p_s, n_s, qp_s, op_s, dec_s):
    i = pl.program_id(1)
    n_chunks = q_ref.shape[0] // CHUNK
    heads = range(DN_HEADS)
    head_cols = [slice(hd * DN_HEAD_DIM, (hd + 1) * DN_HEAD_DIM) for hd in heads]

    @pl.when(i == 0)
    def _():
        state[...] = jnp.zeros_like(state)

    r = lax.broadcasted_iota(jnp.int32, (CHUNK, CHUNK), 0)
    c = lax.broadcasted_iota(jnp.int32, (CHUNK, CHUNK), 1)
    causal = r >= c
    strict = r > c

    def prepare(ci, carry):
        rows = pl.ds(pl.multiple_of(ci * CHUNK, CHUNK), CHUNK)
        bg = bg_ref[rows, :]
        gcr_all = gcr_ref[ci]
        k16 = [k_ref[rows, hc] for hc in head_cols]
        q16 = [q_ref[rows, hc] for hc in head_cols]
        kf = [x.astype(F32) for x in k16]
        beta = [bg[:, hd:hd + 1] for hd in heads]
        gcc = [bg[:, DN_HEADS + hd:DN_HEADS + hd + 1] for hd in heads]
        gcr = [gcr_all[hd:hd + 1, :] for hd in heads]
        kb = [x * b for x, b in zip(kf, beta)]
        both = [_dot_nt(jnp.concatenate([a.astype(BF16), b], axis=0), x)
                for a, b, x in zip(kb, q16, k16)]
        decay = [jnp.exp(jnp.where(causal, a - b, NEG_INF)) for a, b in zip(gcc, gcr)]
        lmats = [jnp.where(strict, m[0:CHUNK] * d, 0.0) for m, d in zip(both, decay)]
        attn = [(m[CHUNK:2 * CHUNK] * d).astype(BF16) for m, d in zip(both, decay)]
        invs = _unit_lower_inverses(lmats)
        eg = [jnp.exp(x) for x in gcc]
        rhs = [jnp.concatenate([(v_ref[rows, hc].astype(F32) * b).astype(BF16), (a * e).astype(BF16)], axis=1)
               for hc, b, a, e in zip(head_cols, beta, kb, eg)]
        sol = [_dot(x.astype(BF16), y).astype(BF16) for x, y in zip(invs, rhs)]
        au = [_dot(a, s) for a, s in zip(attn, sol)]
        g_last = [x[CHUNK - 1:CHUNK, :] for x in gcc]
        kg = [(x * jnp.exp(gl - g)).astype(BF16) for x, gl, g in zip(kf, g_last, gcc)]
        kn = [_dot_tn(a, s) for a, s in zip(kg, sol)]
        for hd in heads:
            n_s[ci, hd] = kn[hd][:, 0:DN_HEAD_DIM]
            p_s[ci, hd] = kn[hd][:, DN_HEAD_DIM:2 * DN_HEAD_DIM].astype(BF16)
            op_s[ci, hd] = au[hd][:, 0:DN_HEAD_DIM]
            qp_s[ci, hd] = (q16[hd].astype(F32) * eg[hd] - au[hd][:, DN_HEAD_DIM:2 * DN_HEAD_DIM]).astype(BF16)
            dec_s[ci, hd] = jnp.broadcast_to(jnp.exp(g_last[hd]), (8, DN_HEAD_DIM))
        return carry

    def scan(ci, carry):
        rows = pl.ds(pl.multiple_of(ci * CHUNK, CHUNK), CHUNK)
        for hd, hc in zip(heads, head_cols):
            s = state[hd]
            s16 = s.astype(BF16)
            o = _dot(qp_s[ci, hd], s16) + op_s[ci, hd]
            state[hd] = s * dec_s[ci, hd][0:1, :] + n_s[ci, hd] - _dot(p_s[ci, hd], s16)
            ms = jnp.mean(o * o, axis=-1, keepdims=True)
            o = o * lax.rsqrt(ms + EPS) * ng_ref[0:1, :] * gate_ref[rows, hc].astype(F32)
            o_ref[rows, hc] = o.astype(BF16)
        return carry

    lax.fori_loop(0, n_chunks, prepare, 0)
    lax.fori_loop(0, n_chunks, scan, 0)


def _delta(q, k, v, gate, bg, norm_g, block=512):
    bsz, seq, _ = q.shape
    n = seq // CHUNK
    nc = block // CHUNK
    gc_rows = bg[:, :, DN_HEADS:2 * DN_HEADS].reshape(bsz, n, CHUNK, DN_HEADS).transpose(0, 1, 3, 2)
    seq_spec = lambda width: pl.BlockSpec((None, block, width), lambda b, i: (b, i, 0))
    per_head = lambda rows, dtype: pltpu.VMEM((nc, DN_HEADS, rows, DN_HEAD_DIM), dtype)
    return pl.pallas_call(
        _delta_body,
        grid=(bsz, seq // block),
        in_specs=[seq_spec(DN_WIDTH)] * 4 + [seq_spec(LANES),
                  pl.BlockSpec((None, nc, DN_HEADS, CHUNK), lambda b, i: (b, i, 0, 0)),
                  pl.BlockSpec((1, DN_HEAD_DIM), lambda b, i: (0, 0))],
        out_specs=seq_spec(DN_WIDTH),
        out_shape=jax.ShapeDtypeStruct((bsz, seq, DN_WIDTH), BF16),
        scratch_shapes=[pltpu.VMEM((DN_HEADS, DN_HEAD_DIM, DN_HEAD_DIM), F32),
                        per_head(DN_HEAD_DIM, BF16), per_head(DN_HEAD_DIM, F32),
                        per_head(CHUNK, BF16), per_head(CHUNK, F32), per_head(8, F32)],
        compiler_params=_params("arbitrary", "arbitrary"),
        name="gated_delta",
    )(q, k, v, gate, bg, gc_rows, norm_g.reshape(1, DN_HEAD_DIM))


def _route(hf, rw):
    tm = hf.shape[0]
    lane = lax.broadcasted_iota(jnp.int32, (tm, LANES), 1)
    logits = jnp.where(lane < N_EXPERTS, _dot(hf, rw, precision=HIGHEST), NEG_INF)
    m1 = jnp.max(logits, axis=-1, keepdims=True)
    i1 = jnp.min(jnp.where(logits == m1, lane, LANES), axis=-1, keepdims=True)
    rest = jnp.where(lane == i1, NEG_INF, logits)
    m2 = jnp.max(rest, axis=-1, keepdims=True)
    i2 = jnp.min(jnp.where(rest == m2, lane, LANES), axis=-1, keepdims=True)
    e2 = jnp.exp(m2 - m1)
    w1 = 1.0 / (1.0 + e2)
    w2 = e2 / (1.0 + e2)
    combine = jnp.where(lane == i1, w1, 0.0) + jnp.where(lane == i2, w2, 0.0)
    chosen = jnp.where((lane == i1) | (lane == i2), 1.0, 0.0)
    return combine, chosen


def _out1_body(a_ref, x_ref, vec_ref, w_ref, rw_ref,
               x_out, h_out, cw_out, rank_out, cnt_out, running, *, tiles_per_block):
    i = pl.program_id(1)
    tm = x_ref.shape[0]

    @pl.when(i % tiles_per_block == 0)
    def _():
        running[...] = jnp.zeros_like(running)

    y = _dot(a_ref[...], w_ref[...])
    x2 = _post_residual(x_ref[...], y, vec_ref[0:1, :], vec_ref[1:2, :])
    x_out[...] = x2
    hf = _norm_mod(x2, vec_ref[2:3, :], vec_ref[3:4, :], vec_ref[4:5, :])
    h_out[...] = hf.astype(BF16)
    combine, chosen = _route(hf, rw_ref[...])
    cw_out[...] = combine
    r = lax.broadcasted_iota(jnp.int32, (tm, tm), 0)
    c = lax.broadcasted_iota(jnp.int32, (tm, tm), 1)
    before = jnp.where(c < r, 1.0, 0.0).astype(BF16)
    rank = _dot(before, chosen.astype(BF16)) + running[0:1, :]
    rank_out[...] = jnp.where(chosen > 0.0, rank, -1.0)
    running[...] = running[...] + jnp.sum(chosen, axis=0, keepdims=True)
    cnt_out[...] = running[...]


def _out1(a, x, vec, w_out, router_w, tm=512, route_block=2048):
    bsz, seq, _ = x.shape
    tiles_per_block = route_block // tm
    blocks_per_seq = seq // route_block
    rw = jnp.zeros((D_MODEL, LANES), F32).at[:, :N_EXPERTS].set(router_w)
    seq_spec = lambda width: pl.BlockSpec((None, tm, width), lambda b, i: (b, i, 0))
    return pl.pallas_call(
        functools.partial(_out1_body, tiles_per_block=tiles_per_block),
        grid=(bsz, seq // tm),
        in_specs=[seq_spec(DN_WIDTH), seq_spec(D_MODEL),
                  pl.BlockSpec((None, 8, D_MODEL), lambda b, i: (b, 0, 0)),
                  pl.BlockSpec((DN_WIDTH, D_MODEL), lambda b, i: (0, 0)),
                  pl.BlockSpec((D_MODEL, LANES), lambda b, i: (0, 0))],
        out_specs=[seq_spec(D_MODEL), seq_spec(D_MODEL), seq_spec(LANES), seq_spec(LANES),
                   pl.BlockSpec((None, 8, LANES),
                                lambda b, i: (b * blocks_per_seq + i // tiles_per_block, 0, 0))],
        out_shape=[jax.ShapeDtypeStruct((bsz, seq, D_MODEL), F32),
                   jax.ShapeDtypeStruct((bsz, seq, D_MODEL), BF16),
                   jax.ShapeDtypeStruct((bsz, seq, LANES), F32),
                   jax.ShapeDtypeStruct((bsz, seq, LANES), F32),
                   jax.ShapeDtypeStruct((bsz * blocks_per_seq, 8, LANES), F32)],
        scratch_shapes=[pltpu.VMEM((8, LANES), F32)],
        compiler_params=_params("arbitrary", "arbitrary"),
        name="l1_out_proj_route",
    )(a, x, vec, w_out.astype(BF16), rw)


def _moe_body(cnt_ref, h_ref, rrow_ref, wrow_ref, rcol_ref, wg_ref, wu_ref, wd_ref, out_ref,
              xs, ys, wslot, *, slot_tile):
    b, e, f = pl.program_id(0), pl.program_id(1), pl.program_id(2)
    tb = h_ref.shape[0]
    n_tiles = (cnt_ref[b * N_EXPERTS + e] + slot_tile - 1) // slot_tile

    @pl.when((e == 0) & (f == 0))
    def _():
        out_ref[...] = jnp.zeros_like(out_ref)

    @pl.when(f == 0)
    def _():
        rrow = rrow_ref[pl.ds(e, 1), :]
        wrow = wrow_ref[pl.ds(e, 1), :]

        def pack(t, carry):
            base = pl.multiple_of(t * slot_tile, slot_tile)
            slot = (base + lax.broadcasted_iota(jnp.int32, (slot_tile, 1), 0)).astype(F32)
            hit = rrow == slot
            xs[pl.ds(base, slot_tile), :] = _dot(jnp.where(hit, 1.0, 0.0).astype(BF16),
                                                 h_ref[...]).astype(BF16)
            wslot[pl.ds(base, slot_tile), :] = jnp.sum(jnp.where(hit, wrow, 0.0), axis=-1, keepdims=True)
            ys[pl.ds(base, slot_tile), :] = jnp.zeros((slot_tile, D_MODEL), F32)
            return carry

        lax.fori_loop(0, n_tiles, pack, 0)

    def expert(t, carry):
        rows = pl.ds(pl.multiple_of(t * slot_tile, slot_tile), slot_tile)
        x = xs[rows, :]
        act = _silu(_dot(x, wg_ref[...])) * _dot(x, wu_ref[...])
        ys[rows, :] += _dot(act.astype(BF16), wd_ref[...])
        return carry

    lax.fori_loop(0, n_tiles, expert, 0)

    @pl.when(f == pl.num_programs(2) - 1)
    def _():
        lane = lax.broadcasted_iota(jnp.int32, (tb, LANES), 1)
        rcol = jnp.sum(jnp.where(lane == e, rcol_ref[...], 0.0), axis=-1, keepdims=True)

        def unpack(t, carry):
            base = pl.multiple_of(t * slot_tile, slot_tile)
            rows = pl.ds(base, slot_tile)
            y = (ys[rows, :] * wslot[rows, :]).astype(BF16)
            slot = (base + lax.broadcasted_iota(jnp.int32, (1, slot_tile), 1)).astype(F32)
            back = jnp.where(rcol == slot, 1.0, 0.0).astype(BF16)
            out_ref[...] += _dot(back, y)
            return carry

        lax.fori_loop(0, n_tiles, unpack, 0)


def _moe(h, combine, rank, counts, w_gate, w_up, w_down, route_block=2048, tf=512, slot_tile=256):
    bsz, seq, _ = h.shape
    tokens = bsz * seq
    n_blocks = tokens // route_block
    per_block_rows = lambda t: (t.reshape(n_blocks, route_block, LANES)[:, :, :N_EXPERTS]
                                .transpose(0, 2, 1))
    cnt = counts[:, 0, :N_EXPERTS].astype(jnp.int32).reshape(n_blocks * N_EXPERTS)
    slots = route_block + slot_tile
    grid_spec = pltpu.PrefetchScalarGridSpec(
        num_scalar_prefetch=1,
        grid=(n_blocks, N_EXPERTS, FFN_DIM // tf),
        in_specs=[pl.BlockSpec((route_block, D_MODEL), lambda b, e, f, cnt: (b, 0)),
                  pl.BlockSpec((None, N_EXPERTS, route_block), lambda b, e, f, cnt: (b, 0, 0)),
                  pl.BlockSpec((None, N_EXPERTS, route_block), lambda b, e, f, cnt: (b, 0, 0)),
                  pl.BlockSpec((route_block, LANES), lambda b, e, f, cnt: (b, 0)),
                  pl.BlockSpec((None, D_MODEL, tf), lambda b, e, f, cnt: (e, 0, f)),
                  pl.BlockSpec((None, D_MODEL, tf), lambda b, e, f, cnt: (e, 0, f)),
                  pl.BlockSpec((None, tf, D_MODEL), lambda b, e, f, cnt: (e, f, 0))],
        out_specs=pl.BlockSpec((route_block, D_MODEL), lambda b, e, f, cnt: (b, 0)),
        scratch_shapes=[pltpu.VMEM((slots, D_MODEL), BF16), pltpu.VMEM((slots, D_MODEL), F32),
                        pltpu.VMEM((slots, 1), F32)])
    out = pl.pallas_call(
        functools.partial(_moe_body, slot_tile=slot_tile),
        grid_spec=grid_spec,
        out_shape=jax.ShapeDtypeStruct((tokens, D_MODEL), F32),
        compiler_params=_params("arbitrary", "arbitrary", "arbitrary"),
        name="l1_moe",
    )(cnt, h.reshape(tokens, D_MODEL), per_block_rows(rank), per_block_rows(combine),
      rank.reshape(tokens, LANES), w_gate.astype(BF16), w_up.astype(BF16), w_down.astype(BF16))
    return out.reshape(bsz, seq, D_MODEL)


def _post_body(x_ref, y_ref, vec_ref, out_ref):
    out_ref[...] = _post_residual(x_ref[...], y_ref[...], vec_ref[0:1, :], vec_ref[1:2, :])


def _post(x, y, vec, tm=1024):
    bsz, seq, _ = x.shape
    seq_spec = pl.BlockSpec((None, tm, D_MODEL), lambda b, i: (b, i, 0))
    return pl.pallas_call(
        _post_body,
        grid=(bsz, seq // tm),
        in_specs=[seq_spec, seq_spec, pl.BlockSpec((None, 8, D_MODEL), lambda b, i: (b, 0, 0))],
        out_specs=seq_spec,
        out_shape=jax.ShapeDtypeStruct((bsz, seq, D_MODEL), F32),
        compiler_params=_params("arbitrary", "arbitrary"),
        name="l1_moe_residual",
    )(x, y, vec)


def kernel(x, c, positions, ada_w, ada_b, mix_pre_g, mix_post_g, ffn_pre_g, ffn_post_g, even_w_in, even_pool_w, even_pool_scale, even_w_out, even_ffn_w_gate, even_ffn_w_up, even_ffn_w_down, odd_w_in, odd_conv_w, odd_a_log, odd_dt_bias, odd_norm_g, odd_w_out, odd_router_w, odd_moe_w_gate, odd_moe_w_up, odd_moe_w_down):
    bsz = x.shape[0]
    mod = _adaln_mod(c, ada_w, ada_b)
    sh1, sc1, gt1, sh2, sc2, gt2 = (mod[:, :, n * D_MODEL:(n + 1) * D_MODEL] for n in range(6))

    vec = _vec_rows([mix_pre_g[0], sc1[0], sh1[0]], bsz)
    qs, ks, vs, o_b = _in0(x, positions, vec, even_w_in[0], even_pool_w[0], even_pool_scale[0])
    branches = [_banded_attention(q, k, v, d) for q, k, v, d in zip(qs, ks, vs, DILATIONS)]
    vec = _vec_rows([mix_post_g[0], gt1[0]], bsz)
    x = _out0([o for o, _ in branches], [l for _, l in branches], o_b, x, vec, even_w_out[0])
    vec = _vec_rows([ffn_pre_g[0], sc2[0], sh2[0], ffn_post_g[0], gt2[0]], bsz)
    x = _ffn(x, vec, even_ffn_w_gate[0], even_ffn_w_up[0], even_ffn_w_down[0])

    vec = _vec_rows([mix_pre_g[1], sc1[1], sh1[1]], bsz)
    q, k, v, gate, bg = _in1(x, vec, odd_w_in[0], odd_conv_w[0], odd_a_log[0], odd_dt_bias[0])
    o = _delta(q, k, v, gate, bg, odd_norm_g[0])
    vec = _vec_rows([mix_post_g[1], gt1[1], ffn_pre_g[1], sc2[1], sh2[1]], bsz)
    x, h, combine, rank, counts = _out1(o, x, vec, odd_w_out[0], odd_router_w[0])
    y = _moe(h, combine, rank, counts, odd_moe_w_gate[0], odd_moe_w_up[0], odd_moe_w_down[0])
    vec = _vec_rows([ffn_post_g[1], gt2[1]], bsz)
    return _post(x, y, vec)
```

```python
import functools

import numpy as np
import jax
import jax.numpy as jnp
from jax import lax
from jax.experimental import pallas as pl
from jax.experimental.pallas import tpu as pltpu

F32 = jnp.float32
BF16 = jnp.bfloat16
HIGHEST = lax.Precision.HIGHEST

D_MODEL = 1024
DEPTH = 2
A_HEADS = 8
A_HEAD_DIM = 64
A_WIDTH = 512
DILATIONS = (1, 4, 16)
BAND = 128
ROT_DIM = 16
ROPE_THETA = 500000.0
POOL_WINDOWS = (2, 4, 8, 16)
POOL_GROUP_DIM = 128
POOL_WIDTH = 512
POOL_HALO = 16
DN_HEADS = 8
DN_HEAD_DIM = 128
DN_WIDTH = 1024
CONV_WIDTH = 4
CONV_HALO = 8
CHUNK = 64
FFN_DIM = 3584
N_EXPERTS = 8
EPS = 1e-6
LANES = 128
ROUTE_BLOCK = 1024
SLOT_TILE = 288
NEG_INF = float("-inf")

VMEM_LIMIT = 56 * 1024 * 1024


def _params(*sem):
    return pltpu.CompilerParams(dimension_semantics=sem, vmem_limit_bytes=VMEM_LIMIT)


def _dot(a, b, precision=None):
    return jnp.dot(a, b, preferred_element_type=F32, precision=precision)


def _dot_nt(a, b, precision=None):
    return lax.dot_general(a, b, (((1,), (1,)), ((), ())), preferred_element_type=F32,
                           precision=precision)


def _dot_tn(a, b, precision=None):
    return lax.dot_general(a, b, (((0,), (0,)), ((), ())), preferred_element_type=F32,
                           precision=precision)


def _silu(x):
    return x * jax.nn.sigmoid(x)


def _norm_mod(x, g, sc, sh):
    ms = jnp.mean(x * x, axis=-1, keepdims=True)
    return x * lax.rsqrt(ms + EPS) * g * (1.0 + sc) + sh


def _post_residual(x, y, g, gt):
    ms = jnp.mean(y * y, axis=-1, keepdims=True)
    return x + gt * (y * lax.rsqrt(ms + EPS) * g)


def _mod_body(c_ref, w_ref, b_ref, o_ref):
    c = c_ref[...]
    o_ref[...] = _dot(_silu(c), w_ref[...], precision=HIGHEST) + b_ref[...]


def _adaln_mod(c, ada_w, ada_b):
    bsz = c.shape[0]
    rows = 8
    c_pad = jnp.zeros((rows, D_MODEL), F32).at[:bsz].set(c)
    tn = 1536
    out = pl.pallas_call(
        _mod_body,
        grid=(DEPTH, 6 * D_MODEL // tn),
        in_specs=[pl.BlockSpec((rows, D_MODEL), lambda l, n: (0, 0)),
                  pl.BlockSpec((None, D_MODEL, tn), lambda l, n: (l, 0, n)),
                  pl.BlockSpec((None, 1, tn), lambda l, n: (l, 0, n))],
        out_specs=pl.BlockSpec((None, rows, tn), lambda l, n: (l, 0, n)),
        out_shape=jax.ShapeDtypeStruct((DEPTH, rows, 6 * D_MODEL), F32),
        compiler_params=_params("arbitrary", "arbitrary"),
        name="adaln_mod",
    )(c_pad, ada_w, ada_b.reshape(DEPTH, 1, 6 * D_MODEL))
    return out[:, :bsz]


def _vec_rows(rows, bsz):
    rows = [jnp.broadcast_to(r.astype(F32), (bsz, D_MODEL)) for r in rows]
    rows = rows + [jnp.zeros((bsz, D_MODEL), F32)] * (8 - len(rows))
    return jnp.stack(rows, axis=1)


def _rope(t, cosv, sin_lo, sin_hi):
    return (t * cosv + pltpu.roll(t, LANES - ROT_DIM // 2, axis=1) * sin_lo
            + pltpu.roll(t, ROT_DIM // 2, axis=1) * sin_hi)


def _store_folded(nat, refs):
    tm = nat.shape[1]
    for d, ref in zip(DILATIONS, refs):
        for r in range(d):
            rows = pl.ds(r, tm // d, stride=d) if d > 1 else slice(None)
            for t in range(A_WIDTH // LANES):
                col = r * A_WIDTH + t * LANES
                ref[:, col:col + LANES] = nat[t, rows, :].astype(BF16)


def _in0_body(x_ref, pos_ref, vec_ref, w_ref, tab_ref, pw_ref, ps_ref,
              q1_ref, q4_ref, q16_ref, k1_ref, k4_ref, k16_ref, v1_ref, v4_ref, v16_ref, ob_ref,
              ubuf, nat):
    i = pl.program_id(1)
    tm = x_ref.shape[0]

    @pl.when(i == 0)
    def _():
        ubuf[0:POOL_HALO, :] = jnp.zeros((POOL_HALO, POOL_WIDTH), F32)

    h = _norm_mod(x_ref[...], vec_ref[0:1, :], vec_ref[1:2, :], vec_ref[2:3, :]).astype(BF16)

    ang = pos_ref[...].astype(F32) * tab_ref[0:1, :]
    cosv = jnp.cos(ang)
    sinv = jnp.sin(ang)
    sin_lo = -sinv * tab_ref[1:2, :]
    sin_hi = sinv * tab_ref[2:3, :]

    pq = _dot(h, w_ref[:, 0:A_WIDTH])
    for t in range(A_WIDTH // LANES):
        cols = slice(t * LANES, (t + 1) * LANES)
        nat[t] = _rope(pq[:, cols], cosv, sin_lo, sin_hi) * (A_HEAD_DIM ** -0.5)
    _store_folded(nat, (q1_ref, q4_ref, q16_ref))
    pk = _dot(h, w_ref[:, A_WIDTH:2 * A_WIDTH])
    for t in range(A_WIDTH // LANES):
        cols = slice(t * LANES, (t + 1) * LANES)
        nat[t] = _rope(pk[:, cols], cosv, sin_lo, sin_hi)
    _store_folded(nat, (k1_ref, k4_ref, k16_ref))
    pv = _dot(h, w_ref[:, 2 * A_WIDTH:3 * A_WIDTH])
    for t in range(A_WIDTH // LANES):
        nat[t] = pv[:, t * LANES:(t + 1) * LANES]
    _store_folded(nat, (v1_ref, v4_ref, v16_ref))

    ubuf[POOL_HALO:POOL_HALO + tm, :] = _dot(h, w_ref[:, 3 * A_WIDTH:3 * A_WIDTH + POOL_WIDTH])
    tpos = i * tm + lax.broadcasted_iota(jnp.int32, (tm, 1), 0)
    for g, win in enumerate(POOL_WINDOWS):
        cols = slice(g * POOL_GROUP_DIM, (g + 1) * POOL_GROUP_DIM)
        cur = ubuf[POOL_HALO:POOL_HALO + tm, cols]
        acc = cur
        for j in range(1, win):
            acc = acc + ubuf[POOL_HALO - j:POOL_HALO - j + tm, cols]
        cnt = jnp.minimum(tpos + 1, win).astype(F32)
        pooled = acc / cnt - cur
        mixed = _dot(pooled.astype(BF16), pw_ref[g]) * ps_ref[0:1, cols]
        ob_ref[:, cols] = mixed.astype(BF16)
    ubuf[0:POOL_HALO, :] = ubuf[tm:tm + POOL_HALO, :]


def _rope_table():
    lane = np.arange(LANES) % A_HEAD_DIM
    half = ROT_DIM // 2
    inv_freq = ROPE_THETA ** (-jnp.arange(0, ROT_DIM, 2, dtype=F32) / ROT_DIM)
    freq = jnp.where(jnp.asarray(lane < ROT_DIM), inv_freq[jnp.asarray(lane % half)], 0.0)
    lo = jnp.asarray((lane < half).astype(np.float32))
    hi = jnp.asarray(((lane >= half) & (lane < ROT_DIM)).astype(np.float32))
    return jnp.stack([freq, lo, hi] + [jnp.zeros((LANES,), F32)] * 5)


def _in0(x, positions, vec, w_in, pool_w, pool_scale, tm=512):
    bsz, seq, _ = x.shape
    n_out = 3 * A_WIDTH + POOL_WIDTH
    tab = _rope_table()
    seq_spec = lambda width: pl.BlockSpec((None, tm, width), lambda b, i: (b, i, 0))
    full = lambda shape: pl.BlockSpec(shape, lambda b, i: (0,) * len(shape))
    folded_spec = lambda d: pl.BlockSpec((None, tm // d, d * A_WIDTH), lambda b, i: (b, i, 0))
    folded_sds = lambda d: jax.ShapeDtypeStruct((bsz, seq // d, d * A_WIDTH), BF16)
    outs = pl.pallas_call(
        _in0_body,
        grid=(bsz, seq // tm),
        in_specs=[seq_spec(D_MODEL), seq_spec(1),
                  pl.BlockSpec((None, 8, D_MODEL), lambda b, i: (b, 0, 0)),
                  full((D_MODEL, n_out)), full((8, LANES)),
                  full((len(POOL_WINDOWS), POOL_GROUP_DIM, POOL_GROUP_DIM)), full((1, POOL_WIDTH))],
        out_specs=[folded_spec(d) for d in DILATIONS] * 3 + [seq_spec(POOL_WIDTH)],
        out_shape=[folded_sds(d) for d in DILATIONS] * 3 + [folded_sds(1)],
        scratch_shapes=[pltpu.VMEM((tm + POOL_HALO, POOL_WIDTH), F32),
                        pltpu.VMEM((A_WIDTH // LANES, tm, LANES), F32)],
        compiler_params=_params("arbitrary", "arbitrary"),
        name="l0_in_proj",
    )(x, positions.reshape(bsz, seq, 1), vec, w_in.astype(BF16), tab,
      pool_w.astype(BF16), pool_scale.reshape(1, POOL_WIDTH))
    n = len(DILATIONS)
    return outs[0:n], outs[n:2 * n], outs[2 * n:3 * n], outs[3 * n]


def _attn_body(q_ref, kp_ref, kc_ref, vp_ref, vc_ref, o_ref, lse_ref):
    i = pl.program_id(2)
    n_bands = q_ref.shape[0] // BAND
    qi = lax.broadcasted_iota(jnp.int32, (BAND, BAND), 0)
    kj = lax.broadcasted_iota(jnp.int32, (BAND, BAND), 1)
    back = kj >= qi
    mask_cur = kj <= qi
    lane = lax.broadcasted_iota(jnp.int32, (BAND, LANES), 1)
    first = lane < A_HEAD_DIM
    pairs = range(A_WIDTH // LANES)
    cols = [slice(g * LANES, (g + 1) * LANES) for g in pairs]
    rows = [slice(s * BAND, (s + 1) * BAND) for s in range(n_bands)]
    probs = [(s, g, half) for s in range(n_bands) for g in pairs for half in range(2)]

    def prev_of(ref_prev, ref_cur, s, g):
        return ref_prev[:, cols[g]] if s == 0 else ref_cur[rows[s - 1], cols[g]]

    qh = []
    for s, g, half in probs:
        qp = q_ref[rows[s], cols[g]]
        qh.append(jnp.where(first if half == 0 else jnp.logical_not(first), qp, jnp.zeros_like(qp)))
    sp = [jnp.where(back & (i > 0) if s == 0 else back, _dot_nt(q, prev_of(kp_ref, kc_ref, s, g)), NEG_INF)
          for q, (s, g, _) in zip(qh, probs)]
    sc = [jnp.where(mask_cur, _dot_nt(q, kc_ref[rows[s], cols[g]]), NEG_INF)
          for q, (s, g, _) in zip(qh, probs)]
    m = [jnp.maximum(jnp.max(a, axis=-1, keepdims=True), jnp.max(b, axis=-1, keepdims=True))
         for a, b in zip(sp, sc)]
    pp = [jnp.exp(a - mm) for a, mm in zip(sp, m)]
    pc = [jnp.exp(a - mm) for a, mm in zip(sc, m)]
    den = [jnp.sum(a, axis=-1, keepdims=True) + jnp.sum(b, axis=-1, keepdims=True)
           for a, b in zip(pp, pc)]
    outs = [(_dot(a.astype(BF16), prev_of(vp_ref, vc_ref, s, g))
             + _dot(b.astype(BF16), vc_ref[rows[s], cols[g]])) / d
            for a, b, d, (s, g, _) in zip(pp, pc, den, probs)]
    per_band = 2 * len(pairs)
    for s in range(n_bands):
        lse_tile = jnp.zeros((BAND, LANES), F32)
        for idx in range(per_band):
            n = s * per_band + idx
            lse_tile = jnp.where(lane == idx, m[n] + jnp.log(den[n]), lse_tile)
        lse_ref[rows[s], :] = lse_tile
        for g in pairs:
            n = s * per_band + 2 * g
            o_ref[rows[s], cols[g]] = jnp.where(first, outs[n], outs[n + 1]).astype(BF16)


def _banded_attention(q, k, v, dilation, bands=2):
    bsz, n_sub, _ = q.shape
    rows = bands * BAND
    cur = pl.BlockSpec((None, rows, A_WIDTH), lambda b, r, i: (b, i, r))
    prev = pl.BlockSpec((None, BAND, A_WIDTH), lambda b, r, i: (b, jnp.maximum(i * bands - 1, 0), r))
    o, lse = pl.pallas_call(
        _attn_body,
        grid=(bsz, dilation, n_sub // rows),
        in_specs=[cur, prev, cur, prev, cur],
        out_specs=[cur, pl.BlockSpec((None, rows, LANES), lambda b, r, i: (b, i, r))],
        out_shape=[jax.ShapeDtypeStruct((bsz, n_sub, dilation * A_WIDTH), BF16),
                   jax.ShapeDtypeStruct((bsz, n_sub, dilation * LANES), F32)],
        compiler_params=_params("arbitrary", "arbitrary", "arbitrary"),
        name=f"dilated_attn_d{dilation}",
    )(q, k, k, v, v)
    return o, lse


def _out0_body(o1_ref, o4_ref, o16_ref, l1_ref, l4_ref, l16_ref, ob_ref, x_ref, vec_ref, w_ref,
               out_ref, o_nat, l_nat):
    tm = x_ref.shape[0]
    for slot, (d, o_ref, l_ref) in enumerate(zip(DILATIONS, (o1_ref, o4_ref, o16_ref),
                                                  (l1_ref, l4_ref, l16_ref))):
        for r in range(d):
            rows = pl.ds(r, tm // d, stride=d) if d > 1 else slice(None)
            for t in range(A_WIDTH // LANES):
                col = r * A_WIDTH + t * LANES
                o_nat[slot, t, rows, :] = o_ref[:, col:col + LANES].astype(F32)
            l_nat[slot, rows, :] = l_ref[:, r * LANES:(r + 1) * LANES]
    la, lb, lc = l_nat[0], l_nat[1], l_nat[2]
    m = jnp.maximum(jnp.maximum(la, lb), lc)
    ea, eb, ec = jnp.exp(la - m), jnp.exp(lb - m), jnp.exp(lc - m)
    tot = ea + eb + ec
    weights = (ea / tot, eb / tot, ec / tot)
    lane = lax.broadcasted_iota(jnp.int32, (tm, LANES), 1)
    first = lane < A_HEAD_DIM
    pieces = []
    for g in range(A_WIDTH // LANES):
        cols = slice(g * LANES, (g + 1) * LANES)
        acc = jnp.zeros((tm, LANES), F32)
        for slot, wt in enumerate(weights):
            w_pair = jnp.where(first,
                               jnp.broadcast_to(wt[:, 2 * g:2 * g + 1], (tm, LANES)),
                               jnp.broadcast_to(wt[:, 2 * g + 1:2 * g + 2], (tm, LANES)))
            acc = acc + w_pair * o_nat[slot, g]
        pieces.append(acc.astype(BF16))
    o_a = jnp.concatenate(pieces, axis=-1)
    y = _dot(o_a, w_ref[0:A_WIDTH, :]) + _dot(ob_ref[...], w_ref[A_WIDTH:A_WIDTH + POOL_WIDTH, :])
    out_ref[...] = _post_residual(x_ref[...], y, vec_ref[0:1, :], vec_ref[1:2, :])


def _out0(os, lses, ob, x, vec, w_out, tm=512):
    bsz, seq, _ = x.shape
    seq_spec = lambda width: pl.BlockSpec((None, tm, width), lambda b, i: (b, i, 0))
    folded = lambda width: [pl.BlockSpec((None, tm // d, d * width), lambda b, i: (b, i, 0))
                            for d in DILATIONS]
    n = len(DILATIONS)
    return pl.pallas_call(
        _out0_body,
        grid=(bsz, seq // tm),
        in_specs=folded(A_WIDTH) + folded(LANES) + [seq_spec(POOL_WIDTH),
                  seq_spec(D_MODEL), pl.BlockSpec((None, 8, D_MODEL), lambda b, i: (b, 0, 0)),
                  pl.BlockSpec((A_WIDTH + POOL_WIDTH, D_MODEL), lambda b, i: (0, 0))],
        out_specs=seq_spec(D_MODEL),
        out_shape=jax.ShapeDtypeStruct((bsz, seq, D_MODEL), F32),
        scratch_shapes=[pltpu.VMEM((n, A_WIDTH // LANES, tm, LANES), F32),
                        pltpu.VMEM((n, tm, LANES), F32)],
        compiler_params=_params("arbitrary", "arbitrary"),
        name="l0_out_proj",
    )(*os, *lses, ob, x, vec, w_out.astype(BF16))


def _ffn_body(x_ref, vec_ref, wg_ref, wu_ref, wd_ref, out_ref, hbuf, acc):
    f = pl.program_id(1)

    @pl.when(f == 0)
    def _():
        hbuf[...] = _norm_mod(x_ref[...], vec_ref[0:1, :], vec_ref[1:2, :],
                              vec_ref[2:3, :]).astype(BF16)
        acc[...] = jnp.zeros_like(acc)

    h = hbuf[...]
    act = _silu(_dot(h, wg_ref[...])) * _dot(h, wu_ref[...])
    acc[...] += _dot(act.astype(BF16), wd_ref[...])

    @pl.when(f == pl.num_programs(1) - 1)
    def _():
        out_ref[...] = _post_residual(x_ref[...], acc[...], vec_ref[3:4, :], vec_ref[4:5, :])


def _ffn(x, vec, w_gate, w_up, w_down, tm=1024, tf=512):
    bsz, seq, _ = x.shape
    tiles_per_seq = seq // tm
    xt = x.reshape(bsz * seq, D_MODEL)
    row = pl.BlockSpec((tm, D_MODEL), lambda i, f: (i, 0))
    out = pl.pallas_call(
        _ffn_body,
        grid=(bsz * seq // tm, FFN_DIM // tf),
        in_specs=[row, pl.BlockSpec((None, 8, D_MODEL), lambda i, f: (i // tiles_per_seq, 0, 0)),
                  pl.BlockSpec((D_MODEL, tf), lambda i, f: (0, f)),
                  pl.BlockSpec((D_MODEL, tf), lambda i, f: (0, f)),
                  pl.BlockSpec((tf, D_MODEL), lambda i, f: (f, 0))],
        out_specs=row,
        out_shape=jax.ShapeDtypeStruct((bsz * seq, D_MODEL), F32),
        scratch_shapes=[pltpu.VMEM((tm, D_MODEL), BF16), pltpu.VMEM((tm, D_MODEL), F32)],
        compiler_params=_params("arbitrary", "arbitrary"),
        name="l0_swiglu",
    )(xt, vec, w_gate.astype(BF16), w_up.astype(BF16), w_down.astype(BF16))
    return out.reshape(bsz, seq, D_MODEL)


def _split3(a):
    a1 = a.astype(BF16)
    r1 = a - a1.astype(F32)
    a2 = r1.astype(BF16)
    a3 = (r1 - a2.astype(F32)).astype(BF16)
    return a1, a2, a3


def _dot_split(a, b):
    a_hi = a.astype(BF16)
    a_lo = (a - a_hi.astype(F32)).astype(BF16)
    b_hi = b.astype(BF16)
    b_lo = (b - b_hi.astype(F32)).astype(BF16)
    return _dot(a_hi, b_hi) + (_dot(a_hi, b_lo) + _dot(a_lo, b_hi))


def _in1_body(x_ref, vec_ref, w_ref, wba_ref, cw_ref, hp_ref,
              q_ref, k_ref, v_ref, gate_ref, bg_ref, cbuf):
    i = pl.program_id(1)
    tm = x_ref.shape[0]

    @pl.when(i == 0)
    def _():
        cbuf[:, 0:CONV_HALO, :] = jnp.zeros((3, CONV_HALO, DN_WIDTH), F32)

    hf = _norm_mod(x_ref[...], vec_ref[0:1, :], vec_ref[1:2, :], vec_ref[2:3, :])
    h = hf.astype(BF16)

    for idx, dst in enumerate((q_ref, k_ref, v_ref)):
        cols = slice(idx * DN_WIDTH, (idx + 1) * DN_WIDTH)
        cbuf[idx, CONV_HALO:CONV_HALO + tm, :] = _dot(h, w_ref[:, cols])
        y = jnp.zeros((tm, DN_WIDTH), F32)
        for j in range(CONV_WIDTH):
            off = CONV_HALO - (CONV_WIDTH - 1) + j
            y = y + cw_ref[j:j + 1, cols] * cbuf[idx, off:off + tm, :]
        y = _silu(y)
        cbuf[idx, 0:CONV_HALO, :] = cbuf[idx, tm:tm + CONV_HALO, :]
        if idx < 2:
            scale = DN_HEAD_DIM ** -0.5 if idx == 0 else 1.0
            for hd in range(DN_HEADS):
                hc = slice(hd * DN_HEAD_DIM, (hd + 1) * DN_HEAD_DIM)
                blk = y[:, hc]
                ss = jnp.sum(blk * blk, axis=-1, keepdims=True)
                dst[:, hc] = (blk * (lax.rsqrt(ss + EPS) * scale)).astype(BF16)
        else:
            dst[...] = y.astype(BF16)

    gate_ref[...] = _silu(_dot(h, w_ref[:, 3 * DN_WIDTH:4 * DN_WIDTH])).astype(BF16)

    ba = _dot_split(hf, wba_ref[...])
    beta = jax.nn.sigmoid(ba)
    z = ba + hp_ref[1:2, :]
    softplus = jnp.maximum(z, 0.0) + jnp.log1p(jnp.exp(-jnp.abs(z)))
    g = -jnp.exp(hp_ref[0:1, :]) * softplus
    r = lax.broadcasted_iota(jnp.int32, (tm, tm), 0)
    c = lax.broadcasted_iota(jnp.int32, (tm, tm), 1)
    tri = jnp.where((r // CHUNK == c // CHUNK) & (c <= r), 1.0, 0.0).astype(BF16)
    g1, g2, g3 = _split3(g)
    gc = _dot(tri, g1) + _dot(tri, g2) + _dot(tri, g3)
    lane = lax.broadcasted_iota(jnp.int32, (tm, LANES), 1)
    bg_ref[...] = jnp.where(lane < DN_HEADS, beta, gc)


def _in1(x, vec, w_in, conv_w, a_log, dt_bias, tm=256):
    bsz, seq, _ = x.shape
    w_main = w_in[:, :4 * DN_WIDTH].astype(BF16)
    w_ba = jnp.zeros((D_MODEL, LANES), F32).at[:, :2 * DN_HEADS].set(w_in[:, 4 * DN_WIDTH:])
    hp = jnp.zeros((8, LANES), F32)
    hp = hp.at[0, DN_HEADS:2 * DN_HEADS].set(a_log).at[1, DN_HEADS:2 * DN_HEADS].set(dt_bias)
    seq_spec = lambda width: pl.BlockSpec((None, tm, width), lambda b, i: (b, i, 0))
    full = lambda shape: pl.BlockSpec(shape, lambda b, i: (0,) * len(shape))
    wide = jax.ShapeDtypeStruct((bsz, seq, DN_WIDTH), BF16)
    return pl.pallas_call(
        _in1_body,
        grid=(bsz, seq // tm),
        in_specs=[seq_spec(D_MODEL), pl.BlockSpec((None, 8, D_MODEL), lambda b, i: (b, 0, 0)),
                  full((D_MODEL, 4 * DN_WIDTH)), full((D_MODEL, LANES)),
                  full((CONV_WIDTH, 3 * DN_WIDTH)), full((8, LANES))],
        out_specs=[seq_spec(DN_WIDTH)] * 4 + [seq_spec(LANES)],
        out_shape=[wide] * 4 + [jax.ShapeDtypeStruct((bsz, seq, LANES), F32)],
        scratch_shapes=[pltpu.VMEM((3, tm + CONV_HALO, DN_WIDTH), F32)],
        compiler_params=_params("arbitrary", "arbitrary"),
        name="l1_in_proj",
    )(x, vec, w_main, w_ba, conv_w, hp)


def _unit_lower_inverses(lmats):
    n = lmats[0].shape[0]
    r = lax.broadcasted_iota(jnp.int32, (n, n), 0)
    c = lax.broadcasted_iota(jnp.int32, (n, n), 1)

    def below(size):
        return (r // (2 * size) == c // (2 * size)) & (r % (2 * size) >= size) & (c % (2 * size) < size)

    eye = jnp.where(r == c, 1.0, 0.0).astype(F32)
    first = below(1)
    invs = [eye - jnp.where(first, lm, 0.0) for lm in lmats]
    size = 2
    while size < n:
        mask = below(size)
        offs = [jnp.where(mask, lm, 0.0).astype(BF16) for lm in lmats]
        inv16 = [x.astype(BF16) for x in invs]
        xc = [_dot(x, o).astype(BF16) for x, o in zip(inv16, offs)]
        invs = [x - _dot(t, x16) for x, t, x16 in zip(invs, xc, inv16)]
        size *= 2
    return invs


def _delta_body(q_ref, k_ref, v_ref, gate_ref, bg_ref, gcr_ref, ng_ref, o_ref,
                state, p_s, n_s, qp_s, op_s, dec_s):
    i = pl.program_id(1)
    n_chunks = q_ref.shape[0] // CHUNK
    heads = range(DN_HEADS)
    head_cols = [slice(hd * DN_HEAD_DIM, (hd + 1) * DN_HEAD_DIM) for hd in heads]

    @pl.when(i == 0)
    def _():
        state[...] = jnp.zeros_like(state)

    r = lax.broadcasted_iota(jnp.int32, (CHUNK, CHUNK), 0)
    c = lax.broadcasted_iota(jnp.int32, (CHUNK, CHUNK), 1)
    causal = r >= c
    strict = r > c

    def prepare(ci, carry):
        rows = pl.ds(pl.multiple_of(ci * CHUNK, CHUNK), CHUNK)
        bg = bg_ref[rows, :]
        gcr_all = gcr_ref[ci]
        k16 = [k_ref[rows, hc] for hc in head_cols]
        q16 = [q_ref[rows, hc] for hc in head_cols]
        kf = [x.astype(F32) for x in k16]
        beta = [bg[:, hd:hd + 1] for hd in heads]
        gcc = [bg[:, DN_HEADS + hd:DN_HEADS + hd + 1] for hd in heads]
        gcr = [gcr_all[hd:hd + 1, :] for hd in heads]
        kb = [x * b for x, b in zip(kf, beta)]
        both = [_dot_nt(jnp.concatenate([a.astype(BF16), b], axis=0), x)
                for a, b, x in zip(kb, q16, k16)]
        decay = [jnp.exp(jnp.where(causal, a - b, NEG_INF)) for a, b in zip(gcc, gcr)]
        lmats = [jnp.where(strict, m[0:CHUNK] * d, 0.0) for m, d in zip(both, decay)]
        attn = [(m[CHUNK:2 * CHUNK] * d).astype(BF16) for m, d in zip(both, decay)]
        invs = _unit_lower_inverses(lmats)
        eg = [jnp.exp(x) for x in gcc]
        rhs = [jnp.concatenate([(v_ref[rows, hc].astype(F32) * b).astype(BF16), (a * e).astype(BF16)], axis=1)
               for hc, b, a, e in zip(head_cols, beta, kb, eg)]
        sol = [_dot(x.astype(BF16), y).astype(BF16) for x, y in zip(invs, rhs)]
        au = [_dot(a, s) for a, s in zip(attn, sol)]
        g_last = [x[CHUNK - 1:CHUNK, :] for x in gcc]
        kg = [(x * jnp.exp(gl - g)).astype(BF16) for x, gl, g in zip(kf, g_last, gcc)]
        kn = [_dot_tn(a, s) for a, s in zip(kg, sol)]
        for hd in heads:
            n_s[ci, hd] = kn[hd][:, 0:DN_HEAD_DIM]
            p_s[ci, hd] = kn[hd][:, DN_HEAD_DIM:2 * DN_HEAD_DIM].astype(BF16)
            op_s[ci, hd] = au[hd][:, 0:DN_HEAD_DIM]
            qp_s[ci, hd] = (q16[hd].astype(F32) * eg[hd] - au[hd][:, DN_HEAD_DIM:2 * DN_HEAD_DIM]).astype(BF16)
            dec_s[ci, hd] = jnp.broadcast_to(jnp.exp(g_last[hd]), (8, DN_HEAD_DIM))
        return carry

    def scan(ci, carry):
        rows = pl.ds(pl.multiple_of(ci * CHUNK, CHUNK), CHUNK)
        for hd, hc in zip(heads, head_cols):
            s = state[hd]
            s16 = s.astype(BF16)
            o = _dot(qp_s[ci, hd], s16) + op_s[ci, hd]
            state[hd] = s * dec_s[ci, hd][0:1, :] + n_s[ci, hd] - _dot(p_s[ci, hd], s16)
            ms = jnp.mean(o * o, axis=-1, keepdims=True)
            o = o * lax.rsqrt(ms + EPS) * ng_ref[0:1, :] * gate_ref[rows, hc].astype(F32)
            o_ref[rows, hc] = o.astype(BF16)
        return carry

    lax.fori_loop(0, n_chunks, prepare, 0)
    lax.fori_loop(0, n_chunks, scan, 0)


def _delta(q, k, v, gate, bg, norm_g, block=512):
    bsz, seq, _ = q.shape
    n = seq // CHUNK
    nc = block // CHUNK
    gc_rows = bg[:, :, DN_HEADS:2 * DN_HEADS].reshape(bsz, n, CHUNK, DN_HEADS).transpose(0, 1, 3, 2)
    seq_spec = lambda width: pl.BlockSpec((None, block, width), lambda b, i: (b, i, 0))
    per_head = lambda rows, dtype: pltpu.VMEM((nc, DN_HEADS, rows, DN_HEAD_DIM), dtype)
    return pl.pallas_call(
        _delta_body,
        grid=(bsz, seq // block),
        in_specs=[seq_spec(DN_WIDTH)] * 4 + [seq_spec(LANES),
                  pl.BlockSpec((None, nc, DN_HEADS, CHUNK), lambda b, i: (b, i, 0, 0)),
                  pl.BlockSpec((1, DN_HEAD_DIM), lambda b, i: (0, 0))],
        out_specs=seq_spec(DN_WIDTH),
        out_shape=jax.ShapeDtypeStruct((bsz, seq, DN_WIDTH), BF16),
        scratch_shapes=[pltpu.VMEM((DN_HEADS, DN_HEAD_DIM, DN_HEAD_DIM), F32),
                        per_head(DN_HEAD_DIM, BF16), per_head(DN_HEAD_DIM, F32),
                        per_head(CHUNK, BF16), per_head(CHUNK, F32), per_head(8, F32)],
        compiler_params=_params("arbitrary", "arbitrary"),
        name="gated_delta",
    )(q, k, v, gate, bg, gc_rows, norm_g.reshape(1, DN_HEAD_DIM))


def _route(hf, rw):
    tm = hf.shape[0]
    lane = lax.broadcasted_iota(jnp.int32, (tm, LANES), 1)
    logits = jnp.where(lane < N_EXPERTS, _dot_split(hf, rw), NEG_INF)
    m1 = jnp.max(logits, axis=-1, keepdims=True)
    i1 = jnp.min(jnp.where(logits == m1, lane, LANES), axis=-1, keepdims=True)
    rest = jnp.where(lane == i1, NEG_INF, logits)
    m2 = jnp.max(rest, axis=-1, keepdims=True)
    i2 = jnp.min(jnp.where(rest == m2, lane, LANES), axis=-1, keepdims=True)
    e2 = jnp.exp(m2 - m1)
    w1 = 1.0 / (1.0 + e2)
    w2 = e2 / (1.0 + e2)
    combine = jnp.where(lane == i1, w1, 0.0) + jnp.where(lane == i2, w2, 0.0)
    chosen = jnp.where((lane == i1) | (lane == i2), 1.0, 0.0)
    return combine, chosen


def _out1_body(a_ref, x_ref, vec_ref, w_ref, rw_ref,
               x_out, h_out, cw_out, rank_out, cnt_out, running, *, tiles_per_block):
    i = pl.program_id(1)
    tm = x_ref.shape[0]

    @pl.when(i % tiles_per_block == 0)
    def _():
        running[...] = jnp.zeros_like(running)

    y = _dot(a_ref[...], w_ref[...])
    x2 = _post_residual(x_ref[...], y, vec_ref[0:1, :], vec_ref[1:2, :])
    x_out[...] = x2
    hf = _norm_mod(x2, vec_ref[2:3, :], vec_ref[3:4, :], vec_ref[4:5, :])
    h_out[...] = hf.astype(BF16)
    combine, chosen = _route(hf, rw_ref[...])
    cw_out[...] = combine
    r = lax.broadcasted_iota(jnp.int32, (tm, tm), 0)
    c = lax.broadcasted_iota(jnp.int32, (tm, tm), 1)
    before = jnp.where(c < r, 1.0, 0.0).astype(BF16)
    rank = _dot(before, chosen.astype(BF16)) + running[0:1, :]
    rank_out[...] = jnp.where(chosen > 0.0, rank, -1.0)
    running[...] = running[...] + jnp.sum(chosen, axis=0, keepdims=True)
    cnt_out[...] = running[...]


def _out1(a, x, vec, w_out, router_w, tm=512, route_block=ROUTE_BLOCK):
    bsz, seq, _ = x.shape
    tiles_per_block = route_block // tm
    blocks_per_seq = seq // route_block
    rw = jnp.zeros((D_MODEL, LANES), F32).at[:, :N_EXPERTS].set(router_w)
    seq_spec = lambda width: pl.BlockSpec((None, tm, width), lambda b, i: (b, i, 0))
    return pl.pallas_call(
        functools.partial(_out1_body, tiles_per_block=tiles_per_block),
        grid=(bsz, seq // tm),
        in_specs=[seq_spec(DN_WIDTH), seq_spec(D_MODEL),
                  pl.BlockSpec((None, 8, D_MODEL), lambda b, i: (b, 0, 0)),
                  pl.BlockSpec((DN_WIDTH, D_MODEL), lambda b, i: (0, 0)),
                  pl.BlockSpec((D_MODEL, LANES), lambda b, i: (0, 0))],
        out_specs=[seq_spec(D_MODEL), seq_spec(D_MODEL), seq_spec(LANES), seq_spec(LANES),
                   pl.BlockSpec((None, 8, LANES),
                                lambda b, i: (b * blocks_per_seq + i // tiles_per_block, 0, 0))],
        out_shape=[jax.ShapeDtypeStruct((bsz, seq, D_MODEL), F32),
                   jax.ShapeDtypeStruct((bsz, seq, D_MODEL), BF16),
                   jax.ShapeDtypeStruct((bsz, seq, LANES), F32),
                   jax.ShapeDtypeStruct((bsz, seq, LANES), F32),
                   jax.ShapeDtypeStruct((bsz * blocks_per_seq, 8, LANES), F32)],
        scratch_shapes=[pltpu.VMEM((8, LANES), F32)],
        compiler_params=_params("arbitrary", "arbitrary"),
        name="l1_out_proj_route",
    )(a, x, vec, w_out.astype(BF16), rw)


def _moe_body(cnt_ref, h_ref, rrow_ref, wrow_ref, rcol_ref, wg_ref, wu_ref, wd_ref, out_ref,
              xs, ys, wslot, *, slot_tile):
    b, e, f = pl.program_id(0), pl.program_id(1), pl.program_id(2)
    tb = h_ref.shape[0]
    n_tiles = (cnt_ref[b * N_EXPERTS + e] + slot_tile - 1) // slot_tile

    @pl.when((e == 0) & (f == 0))
    def _():
        out_ref[...] = jnp.zeros_like(out_ref)

    @pl.when(f == 0)
    def _():
        rrow = rrow_ref[pl.ds(e, 1), :]
        wrow = wrow_ref[pl.ds(e, 1), :]

        def pack(t, carry):
            base = pl.multiple_of(t * slot_tile, slot_tile)
            slot = (base + lax.broadcasted_iota(jnp.int32, (slot_tile, 1), 0)).astype(F32)
            hit = rrow == slot
            xs[pl.ds(base, slot_tile), :] = _dot(jnp.where(hit, 1.0, 0.0).astype(BF16),
                                                 h_ref[...]).astype(BF16)
            wslot[pl.ds(base, slot_tile), :] = jnp.sum(jnp.where(hit, wrow, 0.0), axis=-1, keepdims=True)
            ys[pl.ds(base, slot_tile), :] = jnp.zeros((slot_tile, D_MODEL), F32)
            return carry

        lax.fori_loop(0, n_tiles, pack, 0)

    def expert(t, carry):
        rows = pl.ds(pl.multiple_of(t * slot_tile, slot_tile), slot_tile)
        x = xs[rows, :]
        act = _silu(_dot(x, wg_ref[...])) * _dot(x, wu_ref[...])
        ys[rows, :] += _dot(act.astype(BF16), wd_ref[...])
        return carry

    lax.fori_loop(0, n_tiles, expert, 0)

    @pl.when(f == pl.num_programs(2) - 1)
    def _():
        lane = lax.broadcasted_iota(jnp.int32, (tb, LANES), 1)
        rcol = jnp.sum(jnp.where(lane == e, rcol_ref[...], 0.0), axis=-1, keepdims=True)

        def unpack(t, carry):
            base = pl.multiple_of(t * slot_tile, slot_tile)
            rows = pl.ds(base, slot_tile)
            y = (ys[rows, :] * wslot[rows, :]).astype(BF16)
            slot = (base + lax.broadcasted_iota(jnp.int32, (1, slot_tile), 1)).astype(F32)
            back = jnp.where(rcol == slot, 1.0, 0.0).astype(BF16)
            out_ref[...] += _dot(back, y)
            return carry

        lax.fori_loop(0, n_tiles, unpack, 0)


def _moe(h, combine, rank, counts, w_gate, w_up, w_down, route_block=ROUTE_BLOCK, tf=896,
         slot_tile=SLOT_TILE):
    bsz, seq, _ = h.shape
    tokens = bsz * seq
    n_blocks = tokens // route_block
    per_block_rows = lambda t: (t.reshape(n_blocks, route_block, LANES)[:, :, :N_EXPERTS]
                                .transpose(0, 2, 1))
    cnt = counts[:, 0, :N_EXPERTS].astype(jnp.int32).reshape(n_blocks * N_EXPERTS)
    slots = route_block + slot_tile
    grid_spec = pltpu.PrefetchScalarGridSpec(
        num_scalar_prefetch=1,
        grid=(n_blocks, N_EXPERTS, FFN_DIM // tf),
        in_specs=[pl.BlockSpec((route_block, D_MODEL), lambda b, e, f, cnt: (b, 0)),
                  pl.BlockSpec((None, N_EXPERTS, route_block), lambda b, e, f, cnt: (b, 0, 0)),
                  pl.BlockSpec((None, N_EXPERTS, route_block), lambda b, e, f, cnt: (b, 0, 0)),
                  pl.BlockSpec((route_block, LANES), lambda b, e, f, cnt: (b, 0)),
                  pl.BlockSpec((None, D_MODEL, tf), lambda b, e, f, cnt: (e, 0, f)),
                  pl.BlockSpec((None, D_MODEL, tf), lambda b, e, f, cnt: (e, 0, f)),
                  pl.BlockSpec((None, tf, D_MODEL), lambda b, e, f, cnt: (e, f, 0))],
        out_specs=pl.BlockSpec((route_block, D_MODEL), lambda b, e, f, cnt: (b, 0)),
        scratch_shapes=[pltpu.VMEM((slots, D_MODEL), BF16), pltpu.VMEM((slots, D_MODEL), F32),
                        pltpu.VMEM((slots, 1), F32)])
    out = pl.pallas_call(
        functools.partial(_moe_body, slot_tile=slot_tile),
        grid_spec=grid_spec,
        out_shape=jax.ShapeDtypeStruct((tokens, D_MODEL), F32),
        compiler_params=_params("arbitrary", "arbitrary", "arbitrary"),
        name="l1_moe",
    )(cnt, h.reshape(tokens, D_MODEL), per_block_rows(rank), per_block_rows(combine),
      rank.reshape(tokens, LANES), w_gate.astype(BF16), w_up.astype(BF16), w_down.astype(BF16))
    return out.reshape(bsz, seq, D_MODEL)


def _post_body(x_ref, y_ref, vec_ref, out_ref):
    out_ref[...] = _post_residual(x_ref[...], y_ref[...], vec_ref[0:1, :], vec_ref[1:2, :])


def _post(x, y, vec, tm=1024):
    bsz, seq, _ = x.shape
    seq_spec = pl.BlockSpec((None, tm, D_MODEL), lambda b, i: (b, i, 0))
    return pl.pallas_call(
        _post_body,
        grid=(bsz, seq // tm),
        in_specs=[seq_spec, seq_spec, pl.BlockSpec((None, 8, D_MODEL), lambda b, i: (b, 0, 0))],
        out_specs=seq_spec,
        out_shape=jax.ShapeDtypeStruct((bsz, seq, D_MODEL), F32),
        compiler_params=_params("arbitrary", "arbitrary"),
        name="l1_moe_residual",
    )(x, y, vec)


def kernel(x, c, positions, ada_w, ada_b, mix_pre_g, mix_post_g, ffn_pre_g, ffn_post_g, even_w_in, even_pool_w, even_pool_scale, even_w_out, even_ffn_w_gate, even_ffn_w_up, even_ffn_w_down, odd_w_in, odd_conv_w, odd_a_log, odd_dt_bias, odd_norm_g, odd_w_out, odd_router_w, odd_moe_w_gate, odd_moe_w_up, odd_moe_w_down):
    bsz = x.shape[0]
    mod = _adaln_mod(c, ada_w, ada_b)
    sh1, sc1, gt1, sh2, sc2, gt2 = (mod[:, :, n * D_MODEL:(n + 1) * D_MODEL] for n in range(6))

    vec = _vec_rows([mix_pre_g[0], sc1[0], sh1[0]], bsz)
    qs, ks, vs, o_b = _in0(x, positions, vec, even_w_in[0], even_pool_w[0], even_pool_scale[0])
    branches = [_banded_attention(q, k, v, d) for q, k, v, d in zip(qs, ks, vs, DILATIONS)]
    vec = _vec_rows([mix_post_g[0], gt1[0]], bsz)
    x = _out0([o for o, _ in branches], [l for _, l in branches], o_b, x, vec, even_w_out[0])
    vec = _vec_rows([ffn_pre_g[0], sc2[0], sh2[0], ffn_post_g[0], gt2[0]], bsz)
    x = _ffn(x, vec, even_ffn_w_gate[0], even_ffn_w_up[0], even_ffn_w_down[0])

    vec = _vec_rows([mix_pre_g[1], sc1[1], sh1[1]], bsz)
    q, k, v, gate, bg = _in1(x, vec, odd_w_in[0], odd_conv_w[0], odd_a_log[0], odd_dt_bias[0])
    o = _delta(q, k, v, gate, bg, odd_norm_g[0])
    vec = _vec_rows([mix_post_g[1], gt1[1], ffn_pre_g[1], sc2[1], sh2[1]], bsz)
    x, h, combine, rank, counts = _out1(o, x, vec, odd_w_out[0], odd_router_w[0])
    y = _moe(h, combine, rank, counts, odd_moe_w_gate[0], odd_moe_w_up[0], odd_moe_w_down[0])
    vec = _vec_rows([ffn_post_g[1], gt2[1]], bsz)
    return _post(x, y, vec)
```

```python
import functools

import numpy as np
import jax
import jax.numpy as jnp
from jax import lax
from jax.experimental import pallas as pl
from jax.experimental.pallas import tpu as pltpu

F32 = jnp.float32
BF16 = jnp.bfloat16
HIGHEST = lax.Precision.HIGHEST

D_MODEL = 1024
DEPTH = 2
A_HEADS = 8
A_HEAD_DIM = 64
A_WIDTH = 512
DILATIONS = (1, 4, 16)
BAND = 128
ROT_DIM = 16
ROPE_THETA = 500000.0
POOL_WINDOWS = (2, 4, 8, 16)
POOL_GROUP_DIM = 128
POOL_WIDTH = 512
POOL_HALO = 16
DN_HEADS = 8
DN_HEAD_DIM = 128
DN_WIDTH = 1024
CONV_WIDTH = 4
CONV_HALO = 8
CHUNK = 64
PREP_CHUNKS = 4
FFN_DIM = 3584
N_EXPERTS = 8
EPS = 1e-6
LANES = 128
ROUTE_BLOCK = 1024
SLOT_TILE = 288
SLOT_ALIGN = 32
UNPACK_TILE = 256
NEG_INF = float("-inf")

VMEM_LIMIT = 56 * 1024 * 1024


def _params(*sem):
    return pltpu.CompilerParams(dimension_semantics=sem, vmem_limit_bytes=VMEM_LIMIT)


def _dot(a, b, precision=None):
    return jnp.dot(a, b, preferred_element_type=F32, precision=precision)


def _dot_nt(a, b, precision=None):
    return lax.dot_general(a, b, (((1,), (1,)), ((), ())), preferred_element_type=F32,
                           precision=precision)


def _dot_tn(a, b, precision=None):
    return lax.dot_general(a, b, (((0,), (0,)), ((), ())), preferred_element_type=F32,
                           precision=precision)


def _silu(x):
    return x * jax.nn.sigmoid(x)


def _norm_mod(x, g, sc, sh):
    ms = jnp.mean(x * x, axis=-1, keepdims=True)
    return x * lax.rsqrt(ms + EPS) * g * (1.0 + sc) + sh


def _post_residual(x, y, g, gt):
    ms = jnp.mean(y * y, axis=-1, keepdims=True)
    return x + gt * (y * lax.rsqrt(ms + EPS) * g)


def _mod_body(c_ref, w_ref, b_ref, o_ref):
    c = c_ref[...]
    o_ref[...] = _dot(_silu(c), w_ref[...], precision=HIGHEST) + b_ref[...]


def _adaln_mod(c, ada_w, ada_b):
    bsz = c.shape[0]
    rows = 8
    c_pad = jnp.zeros((rows, D_MODEL), F32).at[:bsz].set(c)
    tn = 1536
    out = pl.pallas_call(
        _mod_body,
        grid=(DEPTH, 6 * D_MODEL // tn),
        in_specs=[pl.BlockSpec((rows, D_MODEL), lambda l, n: (0, 0)),
                  pl.BlockSpec((None, D_MODEL, tn), lambda l, n: (l, 0, n)),
                  pl.BlockSpec((None, 1, tn), lambda l, n: (l, 0, n))],
        out_specs=pl.BlockSpec((None, rows, tn), lambda l, n: (l, 0, n)),
        out_shape=jax.ShapeDtypeStruct((DEPTH, rows, 6 * D_MODEL), F32),
        compiler_params=_params("arbitrary", "arbitrary"),
        name="adaln_mod",
    )(c_pad, ada_w, ada_b.reshape(DEPTH, 1, 6 * D_MODEL))
    return out[:, :bsz]


def _vec_rows(rows, bsz):
    rows = [jnp.broadcast_to(r.astype(F32), (bsz, D_MODEL)) for r in rows]
    rows = rows + [jnp.zeros((bsz, D_MODEL), F32)] * (8 - len(rows))
    return jnp.stack(rows, axis=1)


def _rope(t, cosv, sin_lo, sin_hi):
    return (t * cosv + pltpu.roll(t, LANES - ROT_DIM // 2, axis=1) * sin_lo
            + pltpu.roll(t, ROT_DIM // 2, axis=1) * sin_hi)


def _store_folded(nat, refs):
    tm = nat.shape[1]
    for d, ref in zip(DILATIONS, refs):
        for r in range(d):
            rows = pl.ds(r, tm // d, stride=d) if d > 1 else slice(None)
            for t in range(A_WIDTH // LANES):
                col = r * A_WIDTH + t * LANES
                ref[:, col:col + LANES] = nat[t, rows, :].astype(BF16)


def _in0_body(x_ref, pos_ref, vec_ref, w_ref, tab_ref, pw_ref, ps_ref,
              q1_ref, q4_ref, q16_ref, k1_ref, k4_ref, k16_ref, v1_ref, v4_ref, v16_ref, ob_ref,
              ubuf, nat):
    i = pl.program_id(1)
    tm = x_ref.shape[0]

    @pl.when(i == 0)
    def _():
        ubuf[0:POOL_HALO, :] = jnp.zeros((POOL_HALO, POOL_WIDTH), F32)

    h = _norm_mod(x_ref[...], vec_ref[0:1, :], vec_ref[1:2, :], vec_ref[2:3, :]).astype(BF16)

    ang = pos_ref[...].astype(F32) * tab_ref[0:1, :]
    cosv = jnp.cos(ang)
    sinv = jnp.sin(ang)
    sin_lo = -sinv * tab_ref[1:2, :]
    sin_hi = sinv * tab_ref[2:3, :]

    pq = _dot(h, w_ref[:, 0:A_WIDTH])
    for t in range(A_WIDTH // LANES):
        cols = slice(t * LANES, (t + 1) * LANES)
        nat[t] = _rope(pq[:, cols], cosv, sin_lo, sin_hi) * (A_HEAD_DIM ** -0.5)
    _store_folded(nat, (q1_ref, q4_ref, q16_ref))
    pk = _dot(h, w_ref[:, A_WIDTH:2 * A_WIDTH])
    for t in range(A_WIDTH // LANES):
        cols = slice(t * LANES, (t + 1) * LANES)
        nat[t] = _rope(pk[:, cols], cosv, sin_lo, sin_hi)
    _store_folded(nat, (k1_ref, k4_ref, k16_ref))
    pv = _dot(h, w_ref[:, 2 * A_WIDTH:3 * A_WIDTH])
    for t in range(A_WIDTH // LANES):
        nat[t] = pv[:, t * LANES:(t + 1) * LANES]
    _store_folded(nat, (v1_ref, v4_ref, v16_ref))

    ubuf[POOL_HALO:POOL_HALO + tm, :] = _dot(h, w_ref[:, 3 * A_WIDTH:3 * A_WIDTH + POOL_WIDTH])
    tpos = i * tm + lax.broadcasted_iota(jnp.int32, (tm, 1), 0)
    for g, win in enumerate(POOL_WINDOWS):
        cols = slice(g * POOL_GROUP_DIM, (g + 1) * POOL_GROUP_DIM)
        cur = ubuf[POOL_HALO:POOL_HALO + tm, cols]
        acc = cur
        for j in range(1, win):
            acc = acc + ubuf[POOL_HALO - j:POOL_HALO - j + tm, cols]
        cnt = jnp.minimum(tpos + 1, win).astype(F32)
        pooled = acc / cnt - cur
        mixed = _dot(pooled.astype(BF16), pw_ref[g]) * ps_ref[0:1, cols]
        ob_ref[:, cols] = mixed.astype(BF16)
    ubuf[0:POOL_HALO, :] = ubuf[tm:tm + POOL_HALO, :]


def _rope_table():
    lane = np.arange(LANES) % A_HEAD_DIM
    half = ROT_DIM // 2
    inv_freq = ROPE_THETA ** (-jnp.arange(0, ROT_DIM, 2, dtype=F32) / ROT_DIM)
    freq = jnp.where(jnp.asarray(lane < ROT_DIM), inv_freq[jnp.asarray(lane % half)], 0.0)
    lo = jnp.asarray((lane < half).astype(np.float32))
    hi = jnp.asarray(((lane >= half) & (lane < ROT_DIM)).astype(np.float32))
    return jnp.stack([freq, lo, hi] + [jnp.zeros((LANES,), F32)] * 5)


def _in0(x, positions, vec, w_in, pool_w, pool_scale, tm=512):
    bsz, seq, _ = x.shape
    n_out = 3 * A_WIDTH + POOL_WIDTH
    tab = _rope_table()
    seq_spec = lambda width: pl.BlockSpec((None, tm, width), lambda b, i: (b, i, 0))
    full = lambda shape: pl.BlockSpec(shape, lambda b, i: (0,) * len(shape))
    folded_spec = lambda d: pl.BlockSpec((None, tm // d, d * A_WIDTH), lambda b, i: (b, i, 0))
    folded_sds = lambda d: jax.ShapeDtypeStruct((bsz, seq // d, d * A_WIDTH), BF16)
    outs = pl.pallas_call(
        _in0_body,
        grid=(bsz, seq // tm),
        in_specs=[seq_spec(D_MODEL), seq_spec(1),
                  pl.BlockSpec((None, 8, D_MODEL), lambda b, i: (b, 0, 0)),
                  full((D_MODEL, n_out)), full((8, LANES)),
                  full((len(POOL_WINDOWS), POOL_GROUP_DIM, POOL_GROUP_DIM)), full((1, POOL_WIDTH))],
        out_specs=[folded_spec(d) for d in DILATIONS] * 3 + [seq_spec(POOL_WIDTH)],
        out_shape=[folded_sds(d) for d in DILATIONS] * 3 + [folded_sds(1)],
        scratch_shapes=[pltpu.VMEM((tm + POOL_HALO, POOL_WIDTH), F32),
                        pltpu.VMEM((A_WIDTH // LANES, tm, LANES), F32)],
        compiler_params=_params("arbitrary", "arbitrary"),
        name="l0_in_proj",
    )(x, positions.reshape(bsz, seq, 1), vec, w_in.astype(BF16), tab,
      pool_w.astype(BF16), pool_scale.reshape(1, POOL_WIDTH))
    n = len(DILATIONS)
    return outs[0:n], outs[n:2 * n], outs[2 * n:3 * n], outs[3 * n]


def _attn_body(q_ref, kp_ref, kc_ref, vp_ref, vc_ref, o_ref, lse_ref):
    i = pl.program_id(2)
    n_bands = q_ref.shape[0] // BAND
    qi = lax.broadcasted_iota(jnp.int32, (BAND, BAND), 0)
    kj = lax.broadcasted_iota(jnp.int32, (BAND, BAND), 1)
    back = kj >= qi
    mask_cur = kj <= qi
    lane = lax.broadcasted_iota(jnp.int32, (BAND, LANES), 1)
    first = lane < A_HEAD_DIM
    pairs = range(A_WIDTH // LANES)
    cols = [slice(g * LANES, (g + 1) * LANES) for g in pairs]
    rows = [slice(s * BAND, (s + 1) * BAND) for s in range(n_bands)]
    probs = [(s, g, half) for s in range(n_bands) for g in pairs for half in range(2)]

    def prev_of(ref_prev, ref_cur, s, g):
        return ref_prev[:, cols[g]] if s == 0 else ref_cur[rows[s - 1], cols[g]]

    qh = []
    for s, g, half in probs:
        qp = q_ref[rows[s], cols[g]]
        qh.append(jnp.where(first if half == 0 else jnp.logical_not(first), qp, jnp.zeros_like(qp)))
    sp = [jnp.where(back & (i > 0) if s == 0 else back, _dot_nt(q, prev_of(kp_ref, kc_ref, s, g)), NEG_INF)
          for q, (s, g, _) in zip(qh, probs)]
    sc = [jnp.where(mask_cur, _dot_nt(q, kc_ref[rows[s], cols[g]]), NEG_INF)
          for q, (s, g, _) in zip(qh, probs)]
    m = [jnp.maximum(jnp.max(a, axis=-1, keepdims=True), jnp.max(b, axis=-1, keepdims=True))
         for a, b in zip(sp, sc)]
    pp = [jnp.exp(a - mm) for a, mm in zip(sp, m)]
    pc = [jnp.exp(a - mm) for a, mm in zip(sc, m)]
    den = [jnp.sum(a, axis=-1, keepdims=True) + jnp.sum(b, axis=-1, keepdims=True)
           for a, b in zip(pp, pc)]
    outs = [(_dot(a.astype(BF16), prev_of(vp_ref, vc_ref, s, g))
             + _dot(b.astype(BF16), vc_ref[rows[s], cols[g]])) / d
            for a, b, d, (s, g, _) in zip(pp, pc, den, probs)]
    per_band = 2 * len(pairs)
    for s in range(n_bands):
        lse_tile = jnp.zeros((BAND, LANES), F32)
        for idx in range(per_band):
            n = s * per_band + idx
            lse_tile = jnp.where(lane == idx, m[n] + jnp.log(den[n]), lse_tile)
        lse_ref[rows[s], :] = lse_tile
        for g in pairs:
            n = s * per_band + 2 * g
            o_ref[rows[s], cols[g]] = jnp.where(first, outs[n], outs[n + 1]).astype(BF16)


def _banded_attention(q, k, v, dilation, bands=2):
    bsz, n_sub, _ = q.shape
    rows = bands * BAND
    cur = pl.BlockSpec((None, rows, A_WIDTH), lambda b, r, i: (b, i, r))
    prev = pl.BlockSpec((None, BAND, A_WIDTH), lambda b, r, i: (b, jnp.maximum(i * bands - 1, 0), r))
    o, lse = pl.pallas_call(
        _attn_body,
        grid=(bsz, dilation, n_sub // rows),
        in_specs=[cur, prev, cur, prev, cur],
        out_specs=[cur, pl.BlockSpec((None, rows, LANES), lambda b, r, i: (b, i, r))],
        out_shape=[jax.ShapeDtypeStruct((bsz, n_sub, dilation * A_WIDTH), BF16),
                   jax.ShapeDtypeStruct((bsz, n_sub, dilation * LANES), F32)],
        compiler_params=_params("arbitrary", "arbitrary", "arbitrary"),
        name=f"dilated_attn_d{dilation}",
    )(q, k, k, v, v)
    return o, lse


def _out0_body(o1_ref, o4_ref, o16_ref, l1_ref, l4_ref, l16_ref, ob_ref, x_ref, vec_ref, w_ref,
               out_ref, o_nat, l_nat):
    tm = x_ref.shape[0]
    for slot, (d, o_ref, l_ref) in enumerate(zip(DILATIONS, (o1_ref, o4_ref, o16_ref),
                                                  (l1_ref, l4_ref, l16_ref))):
        for r in range(d):
            rows = pl.ds(r, tm // d, stride=d) if d > 1 else slice(None)
            for t in range(A_WIDTH // LANES):
                col = r * A_WIDTH + t * LANES
                o_nat[slot, t, rows, :] = o_ref[:, col:col + LANES].astype(F32)
            l_nat[slot, rows, :] = l_ref[:, r * LANES:(r + 1) * LANES]
    la, lb, lc = l_nat[0], l_nat[1], l_nat[2]
    m = jnp.maximum(jnp.maximum(la, lb), lc)
    ea, eb, ec = jnp.exp(la - m), jnp.exp(lb - m), jnp.exp(lc - m)
    tot = ea + eb + ec
    weights = (ea / tot, eb / tot, ec / tot)
    lane = lax.broadcasted_iota(jnp.int32, (tm, LANES), 1)
    first = lane < A_HEAD_DIM
    pieces = []
    for g in range(A_WIDTH // LANES):
        cols = slice(g * LANES, (g + 1) * LANES)
        acc = jnp.zeros((tm, LANES), F32)
        for slot, wt in enumerate(weights):
            w_pair = jnp.where(first,
                               jnp.broadcast_to(wt[:, 2 * g:2 * g + 1], (tm, LANES)),
                               jnp.broadcast_to(wt[:, 2 * g + 1:2 * g + 2], (tm, LANES)))
            acc = acc + w_pair * o_nat[slot, g]
        pieces.append(acc.astype(BF16))
    o_a = jnp.concatenate(pieces, axis=-1)
    y = _dot(o_a, w_ref[0:A_WIDTH, :]) + _dot(ob_ref[...], w_ref[A_WIDTH:A_WIDTH + POOL_WIDTH, :])
    out_ref[...] = _post_residual(x_ref[...], y, vec_ref[0:1, :], vec_ref[1:2, :])


def _out0(os, lses, ob, x, vec, w_out, tm=512):
    bsz, seq, _ = x.shape
    seq_spec = lambda width: pl.BlockSpec((None, tm, width), lambda b, i: (b, i, 0))
    folded = lambda width: [pl.BlockSpec((None, tm // d, d * width), lambda b, i: (b, i, 0))
                            for d in DILATIONS]
    n = len(DILATIONS)
    return pl.pallas_call(
        _out0_body,
        grid=(bsz, seq // tm),
        in_specs=folded(A_WIDTH) + folded(LANES) + [seq_spec(POOL_WIDTH),
                  seq_spec(D_MODEL), pl.BlockSpec((None, 8, D_MODEL), lambda b, i: (b, 0, 0)),
                  pl.BlockSpec((A_WIDTH + POOL_WIDTH, D_MODEL), lambda b, i: (0, 0))],
        out_specs=seq_spec(D_MODEL),
        out_shape=jax.ShapeDtypeStruct((bsz, seq, D_MODEL), F32),
        scratch_shapes=[pltpu.VMEM((n, A_WIDTH // LANES, tm, LANES), F32),
                        pltpu.VMEM((n, tm, LANES), F32)],
        compiler_params=_params("arbitrary", "arbitrary"),
        name="l0_out_proj",
    )(*os, *lses, ob, x, vec, w_out.astype(BF16))


def _ffn_body(x_ref, vec_ref, wg_ref, wu_ref, wd_ref, out_ref, hbuf, acc):
    f = pl.program_id(1)

    @pl.when(f == 0)
    def _():
        hbuf[...] = _norm_mod(x_ref[...], vec_ref[0:1, :], vec_ref[1:2, :],
                              vec_ref[2:3, :]).astype(BF16)
        acc[...] = jnp.zeros_like(acc)

    h = hbuf[...]
    act = _silu(_dot(h, wg_ref[...])) * _dot(h, wu_ref[...])
    acc[...] += _dot(act.astype(BF16), wd_ref[...])

    @pl.when(f == pl.num_programs(1) - 1)
    def _():
        out_ref[...] = _post_residual(x_ref[...], acc[...], vec_ref[3:4, :], vec_ref[4:5, :])


def _ffn(x, vec, w_gate, w_up, w_down, tm=1024, tf=512):
    bsz, seq, _ = x.shape
    tiles_per_seq = seq // tm
    xt = x.reshape(bsz * seq, D_MODEL)
    row = pl.BlockSpec((tm, D_MODEL), lambda i, f: (i, 0))
    out = pl.pallas_call(
        _ffn_body,
        grid=(bsz * seq // tm, FFN_DIM // tf),
        in_specs=[row, pl.BlockSpec((None, 8, D_MODEL), lambda i, f: (i // tiles_per_seq, 0, 0)),
                  pl.BlockSpec((D_MODEL, tf), lambda i, f: (0, f)),
                  pl.BlockSpec((D_MODEL, tf), lambda i, f: (0, f)),
                  pl.BlockSpec((tf, D_MODEL), lambda i, f: (f, 0))],
        out_specs=row,
        out_shape=jax.ShapeDtypeStruct((bsz * seq, D_MODEL), F32),
        scratch_shapes=[pltpu.VMEM((tm, D_MODEL), BF16), pltpu.VMEM((tm, D_MODEL), F32)],
        compiler_params=_params("arbitrary", "arbitrary"),
        name="l0_swiglu",
    )(xt, vec, w_gate.astype(BF16), w_up.astype(BF16), w_down.astype(BF16))
    return out.reshape(bsz, seq, D_MODEL)


def _split3(a):
    a1 = a.astype(BF16)
    r1 = a - a1.astype(F32)
    a2 = r1.astype(BF16)
    a3 = (r1 - a2.astype(F32)).astype(BF16)
    return a1, a2, a3


def _dot_split(a, b):
    a_hi = a.astype(BF16)
    a_lo = (a - a_hi.astype(F32)).astype(BF16)
    b_hi = b.astype(BF16)
    b_lo = (b - b_hi.astype(F32)).astype(BF16)
    return _dot(a_hi, b_hi) + (_dot(a_hi, b_lo) + _dot(a_lo, b_hi))


def _in1_body(x_ref, vec_ref, w_ref, wba_ref, cw_ref, hp_ref,
              q_ref, k_ref, v_ref, gate_ref, bg_ref, cbuf):
    i = pl.program_id(1)
    tm = x_ref.shape[0]

    @pl.when(i == 0)
    def _():
        cbuf[:, 0:CONV_HALO, :] = jnp.zeros((3, CONV_HALO, DN_WIDTH), F32)

    hf = _norm_mod(x_ref[...], vec_ref[0:1, :], vec_ref[1:2, :], vec_ref[2:3, :])
    h = hf.astype(BF16)

    for idx, dst in enumerate((q_ref, k_ref, v_ref)):
        cols = slice(idx * DN_WIDTH, (idx + 1) * DN_WIDTH)
        cbuf[idx, CONV_HALO:CONV_HALO + tm, :] = _dot(h, w_ref[:, cols])
        y = jnp.zeros((tm, DN_WIDTH), F32)
        for j in range(CONV_WIDTH):
            off = CONV_HALO - (CONV_WIDTH - 1) + j
            y = y + cw_ref[j:j + 1, cols] * cbuf[idx, off:off + tm, :]
        y = _silu(y)
        cbuf[idx, 0:CONV_HALO, :] = cbuf[idx, tm:tm + CONV_HALO, :]
        if idx < 2:
            scale = DN_HEAD_DIM ** -0.5 if idx == 0 else 1.0
            for hd in range(DN_HEADS):
                hc = slice(hd * DN_HEAD_DIM, (hd + 1) * DN_HEAD_DIM)
                blk = y[:, hc]
                ss = jnp.sum(blk * blk, axis=-1, keepdims=True)
                dst[:, hc] = (blk * (lax.rsqrt(ss + EPS) * scale)).astype(BF16)
        else:
            dst[...] = y.astype(BF16)

    gate_ref[...] = _silu(_dot(h, w_ref[:, 3 * DN_WIDTH:4 * DN_WIDTH])).astype(BF16)

    ba = _dot_split(hf, wba_ref[...])
    beta = jax.nn.sigmoid(ba)
    z = ba + hp_ref[1:2, :]
    softplus = jnp.maximum(z, 0.0) + jnp.log1p(jnp.exp(-jnp.abs(z)))
    g = -jnp.exp(hp_ref[0:1, :]) * softplus
    r = lax.broadcasted_iota(jnp.int32, (tm, tm), 0)
    c = lax.broadcasted_iota(jnp.int32, (tm, tm), 1)
    tri = jnp.where((r // CHUNK == c // CHUNK) & (c <= r), 1.0, 0.0).astype(BF16)
    g1, g2, g3 = _split3(g)
    gc = _dot(tri, g1) + _dot(tri, g2) + _dot(tri, g3)
    lane = lax.broadcasted_iota(jnp.int32, (tm, LANES), 1)
    bg_ref[...] = jnp.where(lane < DN_HEADS, beta, gc)


def _in1(x, vec, w_in, conv_w, a_log, dt_bias, tm=256):
    bsz, seq, _ = x.shape
    w_main = w_in[:, :4 * DN_WIDTH].astype(BF16)
    w_ba = jnp.zeros((D_MODEL, LANES), F32).at[:, :2 * DN_HEADS].set(w_in[:, 4 * DN_WIDTH:])
    hp = jnp.zeros((8, LANES), F32)
    hp = hp.at[0, DN_HEADS:2 * DN_HEADS].set(a_log).at[1, DN_HEADS:2 * DN_HEADS].set(dt_bias)
    seq_spec = lambda width: pl.BlockSpec((None, tm, width), lambda b, i: (b, i, 0))
    full = lambda shape: pl.BlockSpec(shape, lambda b, i: (0,) * len(shape))
    wide = jax.ShapeDtypeStruct((bsz, seq, DN_WIDTH), BF16)
    return pl.pallas_call(
        _in1_body,
        grid=(bsz, seq // tm),
        in_specs=[seq_spec(D_MODEL), pl.BlockSpec((None, 8, D_MODEL), lambda b, i: (b, 0, 0)),
                  full((D_MODEL, 4 * DN_WIDTH)), full((D_MODEL, LANES)),
                  full((CONV_WIDTH, 3 * DN_WIDTH)), full((8, LANES))],
        out_specs=[seq_spec(DN_WIDTH)] * 4 + [seq_spec(LANES)],
        out_shape=[wide] * 4 + [jax.ShapeDtypeStruct((bsz, seq, LANES), F32)],
        scratch_shapes=[pltpu.VMEM((3, tm + CONV_HALO, DN_WIDTH), F32)],
        compiler_params=_params("arbitrary", "arbitrary"),
        name="l1_in_proj",
    )(x, vec, w_main, w_ba, conv_w, hp)


def _unit_lower_inverses(lmats):
    n = lmats[0].shape[0]
    r = lax.broadcasted_iota(jnp.int32, (n, n), 0)
    c = lax.broadcasted_iota(jnp.int32, (n, n), 1)

    def below(size):
        return (r // (2 * size) == c // (2 * size)) & (r % (2 * size) >= size) & (c % (2 * size) < size)

    eye = jnp.where(r == c, 1.0, 0.0).astype(F32)
    first = below(1)
    invs = [eye - jnp.where(first, lm, 0.0) for lm in lmats]
    size = 2
    while size < n:
        mask = below(size)
        offs = [jnp.where(mask, lm, 0.0).astype(BF16) for lm in lmats]
        inv16 = [x.astype(BF16) for x in invs]
        xc = [_dot(x, o).astype(BF16) for x, o in zip(inv16, offs)]
        invs = [x - _dot(t, x16) for x, t, x16 in zip(invs, xc, inv16)]
        size *= 2
    return invs


def _delta_body(q_ref, k_ref, v_ref, gate_ref, bg_ref, gcr_ref, ng_ref, o_ref,
                state, p_s, n_s, qp_s, op_s, dec_s):
    i = pl.program_id(1)
    n_chunks = q_ref.shape[0] // CHUNK
    heads = range(DN_HEADS)
    head_cols = [slice(hd * DN_HEAD_DIM, (hd + 1) * DN_HEAD_DIM) for hd in heads]

    @pl.when(i == 0)
    def _():
        state[...] = jnp.zeros_like(state)

    r = lax.broadcasted_iota(jnp.int32, (CHUNK, CHUNK), 0)
    c = lax.broadcasted_iota(jnp.int32, (CHUNK, CHUNK), 1)
    causal = r >= c
    strict = r > c

    def prepare(step, carry):
        chunks = [step * PREP_CHUNKS + n for n in range(PREP_CHUNKS)]
        rows = [pl.ds(pl.multiple_of(ci * CHUNK, CHUNK), CHUNK) for ci in chunks]
        bg = [bg_ref[r, :] for r in rows]
        gcr_all = [gcr_ref[ci] for ci in chunks]
        probs = [(n, hd) for n in range(PREP_CHUNKS) for hd in heads]
        k16 = [k_ref[rows[n], head_cols[hd]] for n, hd in probs]
        q16 = [q_ref[rows[n], head_cols[hd]] for n, hd in probs]
        kf = [x.astype(F32) for x in k16]
        beta = [bg[n][:, hd:hd + 1] for n, hd in probs]
        gcc = [bg[n][:, DN_HEADS + hd:DN_HEADS + hd + 1] for n, hd in probs]
        gcr = [gcr_all[n][hd:hd + 1, :] for n, hd in probs]
        kb = [x * b for x, b in zip(kf, beta)]
        both = [_dot_nt(jnp.concatenate([a.astype(BF16), b], axis=0), x)
                for a, b, x in zip(kb, q16, k16)]
        decay = [jnp.exp(jnp.where(causal, a - b, NEG_INF)) for a, b in zip(gcc, gcr)]
        lmats = [jnp.where(strict, m[0:CHUNK] * d, 0.0) for m, d in zip(both, decay)]
        attn = [(m[CHUNK:2 * CHUNK] * d).astype(BF16) for m, d in zip(both, decay)]
        invs = _unit_lower_inverses(lmats)
        eg = [jnp.exp(x) for x in gcc]
        rhs = [jnp.concatenate([(v_ref[rows[n], head_cols[hd]].astype(F32) * b).astype(BF16),
                                (a * e).astype(BF16)], axis=1)
               for (n, hd), b, a, e in zip(probs, beta, kb, eg)]
        sol = [_dot(x.astype(BF16), y).astype(BF16) for x, y in zip(invs, rhs)]
        au = [_dot(a, s) for a, s in zip(attn, sol)]
        g_last = [x[CHUNK - 1:CHUNK, :] for x in gcc]
        kg = [(x * jnp.exp(gl - g)).astype(BF16) for x, gl, g in zip(kf, g_last, gcc)]
        kn = [_dot_tn(a, s) for a, s in zip(kg, sol)]
        for idx, (n, hd) in enumerate(probs):
            ci = chunks[n]
            n_s[ci, hd] = kn[idx][:, 0:DN_HEAD_DIM]
            p_s[ci, hd] = kn[idx][:, DN_HEAD_DIM:2 * DN_HEAD_DIM].astype(BF16)
            op_s[ci, hd] = au[idx][:, 0:DN_HEAD_DIM]
            qp_s[ci, hd] = (q16[idx].astype(F32) * eg[idx]
                            - au[idx][:, DN_HEAD_DIM:2 * DN_HEAD_DIM]).astype(BF16)
            dec_s[ci, hd] = jnp.broadcast_to(jnp.exp(g_last[idx]), (8, DN_HEAD_DIM))
        return carry

    def scan(ci, carry):
        rows = pl.ds(pl.multiple_of(ci * CHUNK, CHUNK), CHUNK)
        for hd, hc in zip(heads, head_cols):
            s = state[hd]
            s16 = s.astype(BF16)
            o = _dot(qp_s[ci, hd], s16) + op_s[ci, hd]
            state[hd] = s * dec_s[ci, hd][0:1, :] + n_s[ci, hd] - _dot(p_s[ci, hd], s16)
            ms = jnp.mean(o * o, axis=-1, keepdims=True)
            o = o * lax.rsqrt(ms + EPS) * ng_ref[0:1, :] * gate_ref[rows, hc].astype(F32)
            o_ref[rows, hc] = o.astype(BF16)
        return carry

    lax.fori_loop(0, n_chunks // PREP_CHUNKS, prepare, 0)
    lax.fori_loop(0, n_chunks, scan, 0, unroll=4)


def _delta(q, k, v, gate, bg, norm_g, block=512):
    bsz, seq, _ = q.shape
    n = seq // CHUNK
    nc = block // CHUNK
    gc_rows = bg[:, :, DN_HEADS:2 * DN_HEADS].reshape(bsz, n, CHUNK, DN_HEADS).transpose(0, 1, 3, 2)
    seq_spec = lambda width: pl.BlockSpec((None, block, width), lambda b, i: (b, i, 0))
    per_head = lambda rows, dtype: pltpu.VMEM((nc, DN_HEADS, rows, DN_HEAD_DIM), dtype)
    return pl.pallas_call(
        _delta_body,
        grid=(bsz, seq // block),
        in_specs=[seq_spec(DN_WIDTH)] * 4 + [seq_spec(LANES),
                  pl.BlockSpec((None, nc, DN_HEADS, CHUNK), lambda b, i: (b, i, 0, 0)),
                  pl.BlockSpec((1, DN_HEAD_DIM), lambda b, i: (0, 0))],
        out_specs=seq_spec(DN_WIDTH),
        out_shape=jax.ShapeDtypeStruct((bsz, seq, DN_WIDTH), BF16),
        scratch_shapes=[pltpu.VMEM((DN_HEADS, DN_HEAD_DIM, DN_HEAD_DIM), F32),
                        per_head(DN_HEAD_DIM, BF16), per_head(DN_HEAD_DIM, F32),
                        per_head(CHUNK, BF16), per_head(CHUNK, F32), per_head(8, F32)],
        compiler_params=_params("arbitrary", "arbitrary"),
        name="gated_delta",
    )(q, k, v, gate, bg, gc_rows, norm_g.reshape(1, DN_HEAD_DIM))


def _route(hf, rw):
    tm = hf.shape[0]
    lane = lax.broadcasted_iota(jnp.int32, (tm, LANES), 1)
    logits = jnp.where(lane < N_EXPERTS, _dot_split(hf, rw), NEG_INF)
    m1 = jnp.max(logits, axis=-1, keepdims=True)
    i1 = jnp.min(jnp.where(logits == m1, lane, LANES), axis=-1, keepdims=True)
    rest = jnp.where(lane == i1, NEG_INF, logits)
    m2 = jnp.max(rest, axis=-1, keepdims=True)
    i2 = jnp.min(jnp.where(rest == m2, lane, LANES), axis=-1, keepdims=True)
    e2 = jnp.exp(m2 - m1)
    w1 = 1.0 / (1.0 + e2)
    w2 = e2 / (1.0 + e2)
    combine = jnp.where(lane == i1, w1, 0.0) + jnp.where(lane == i2, w2, 0.0)
    chosen = jnp.where((lane == i1) | (lane == i2), 1.0, 0.0)
    return combine, chosen


def _out1_body(a_ref, x_ref, vec_ref, w_ref, rw_ref,
               x_out, h_out, cw_out, rank_out, cnt_out, running, *, tiles_per_block):
    i = pl.program_id(1)
    tm = x_ref.shape[0]

    @pl.when(i % tiles_per_block == 0)
    def _():
        running[...] = jnp.zeros_like(running)

    y = _dot(a_ref[...], w_ref[...])
    x2 = _post_residual(x_ref[...], y, vec_ref[0:1, :], vec_ref[1:2, :])
    x_out[...] = x2
    hf = _norm_mod(x2, vec_ref[2:3, :], vec_ref[3:4, :], vec_ref[4:5, :])
    h_out[...] = hf.astype(BF16)
    combine, chosen = _route(hf, rw_ref[...])
    cw_out[...] = combine
    r = lax.broadcasted_iota(jnp.int32, (tm, tm), 0)
    c = lax.broadcasted_iota(jnp.int32, (tm, tm), 1)
    before = jnp.where(c < r, 1.0, 0.0).astype(BF16)
    rank = _dot(before, chosen.astype(BF16)) + running[0:1, :]
    rank_out[...] = jnp.where(chosen > 0.0, rank, -1.0)
    running[...] = running[...] + jnp.sum(chosen, axis=0, keepdims=True)
    cnt_out[...] = running[...]


def _out1(a, x, vec, w_out, router_w, tm=512, route_block=ROUTE_BLOCK):
    bsz, seq, _ = x.shape
    tiles_per_block = route_block // tm
    blocks_per_seq = seq // route_block
    rw = jnp.zeros((D_MODEL, LANES), F32).at[:, :N_EXPERTS].set(router_w)
    seq_spec = lambda width: pl.BlockSpec((None, tm, width), lambda b, i: (b, i, 0))
    return pl.pallas_call(
        functools.partial(_out1_body, tiles_per_block=tiles_per_block),
        grid=(bsz, seq // tm),
        in_specs=[seq_spec(DN_WIDTH), seq_spec(D_MODEL),
                  pl.BlockSpec((None, 8, D_MODEL), lambda b, i: (b, 0, 0)),
                  pl.BlockSpec((DN_WIDTH, D_MODEL), lambda b, i: (0, 0)),
                  pl.BlockSpec((D_MODEL, LANES), lambda b, i: (0, 0))],
        out_specs=[seq_spec(D_MODEL), seq_spec(D_MODEL), seq_spec(LANES), seq_spec(LANES),
                   pl.BlockSpec((None, 8, LANES),
                                lambda b, i: (b * blocks_per_seq + i // tiles_per_block, 0, 0))],
        out_shape=[jax.ShapeDtypeStruct((bsz, seq, D_MODEL), F32),
                   jax.ShapeDtypeStruct((bsz, seq, D_MODEL), BF16),
                   jax.ShapeDtypeStruct((bsz, seq, LANES), F32),
                   jax.ShapeDtypeStruct((bsz, seq, LANES), F32),
                   jax.ShapeDtypeStruct((bsz * blocks_per_seq, 8, LANES), F32)],
        scratch_shapes=[pltpu.VMEM((8, LANES), F32)],
        compiler_params=_params("arbitrary", "arbitrary"),
        name="l1_out_proj_route",
    )(a, x, vec, w_out.astype(BF16), rw)


def _moe_body(cnt_ref, h_ref, rrow_ref, wrow_ref, rcol_ref, wg_ref, wu_ref, wd_ref, out_ref,
              xs, ys, wslot, *, slot_tile, route_block):
    b, e, f = pl.program_id(0), pl.program_id(1), pl.program_id(2)
    group = h_ref.shape[0] // route_block
    region = xs.shape[0] // group

    @pl.when((e == 0) & (f == 0))
    def _():
        out_ref[...] = jnp.zeros_like(out_ref)

    for s in range(group):
        count = cnt_ref[(b * group + s) * N_EXPERTS + e]
        n_tiles = (count + slot_tile - 1) // slot_tile
        n_chunks = (count + UNPACK_TILE - 1) // UNPACK_TILE
        tokens = slice(s * route_block, (s + 1) * route_block)
        first_row = s * region

        def slot_rows(t, size, align):
            return pl.ds(first_row + pl.multiple_of(t * size, align), size)

        @pl.when(f == 0)
        def _():
            rrow = rrow_ref[s, pl.ds(e, 1), :]
            wrow = wrow_ref[s, pl.ds(e, 1), :]

            def pack(t, carry):
                rows = slot_rows(t, slot_tile, SLOT_ALIGN)
                slot = (t * slot_tile + lax.broadcasted_iota(jnp.int32, (slot_tile, 1), 0)).astype(F32)
                hit = rrow == slot
                xs[rows, :] = _dot(jnp.where(hit, 1.0, 0.0).astype(BF16), h_ref[tokens, :]).astype(BF16)
                wslot[rows, :] = jnp.sum(jnp.where(hit, wrow, 0.0), axis=-1, keepdims=True)
                ys[rows, :] = jnp.zeros((slot_tile, D_MODEL), F32)
                return carry

            lax.fori_loop(0, n_tiles, pack, 0)

            @pl.when(n_chunks * UNPACK_TILE > n_tiles * slot_tile)
            def _():
                rows = slot_rows(n_tiles, slot_tile, SLOT_ALIGN)
                ys[rows, :] = jnp.zeros((slot_tile, D_MODEL), F32)
                wslot[rows, :] = jnp.zeros((slot_tile, 1), F32)

        def expert(t, carry):
            rows = slot_rows(t, slot_tile, SLOT_ALIGN)
            x = xs[rows, :]
            act = _silu(_dot(x, wg_ref[...])) * _dot(x, wu_ref[...])
            ys[rows, :] += _dot(act.astype(BF16), wd_ref[...])
            return carry

        lax.fori_loop(0, n_tiles, expert, 0)

        @pl.when(f == pl.num_programs(2) - 1)
        def _():
            lane = lax.broadcasted_iota(jnp.int32, (route_block, LANES), 1)
            rcol = jnp.sum(jnp.where(lane == e, rcol_ref[tokens, :], 0.0), axis=-1, keepdims=True)

            def unpack(c, carry):
                rows = slot_rows(c, UNPACK_TILE, UNPACK_TILE)
                y = (ys[rows, :] * wslot[rows, :]).astype(BF16)
                slot = (c * UNPACK_TILE + lax.broadcasted_iota(jnp.int32, (1, UNPACK_TILE), 1)).astype(F32)
                back = jnp.where(rcol == slot, 1.0, 0.0).astype(BF16)
                out_ref[tokens, :] += _dot(back, y)
                return carry

            lax.fori_loop(0, n_chunks, unpack, 0)


def _moe(h, combine, rank, counts, w_gate, w_up, w_down, route_block=ROUTE_BLOCK, group=2, tf=512,
         slot_tile=SLOT_TILE):
    bsz, seq, _ = h.shape
    tokens = bsz * seq
    n_blocks = tokens // route_block
    n_f = FFN_DIM // tf
    per_block_rows = lambda t: (t.reshape(n_blocks, route_block, LANES)[:, :, :N_EXPERTS]
                                .transpose(0, 2, 1))
    cnt = counts[:, 0, :N_EXPERTS].astype(jnp.int32).reshape(n_blocks * N_EXPERTS)
    sliced = lambda w: w.astype(BF16).reshape(N_EXPERTS, D_MODEL, n_f, tf).transpose(0, 2, 1, 3)
    region = -(-route_block // slot_tile) * slot_tile
    assert region >= route_block and slot_tile % SLOT_ALIGN == 0 and UNPACK_TILE <= slot_tile
    rows = group * route_block
    grid_spec = pltpu.PrefetchScalarGridSpec(
        num_scalar_prefetch=1,
        grid=(n_blocks // group, N_EXPERTS, n_f),
        in_specs=[pl.BlockSpec((rows, D_MODEL), lambda b, e, f, cnt: (b, 0)),
                  pl.BlockSpec((group, N_EXPERTS, route_block), lambda b, e, f, cnt: (b, 0, 0)),
                  pl.BlockSpec((group, N_EXPERTS, route_block), lambda b, e, f, cnt: (b, 0, 0)),
                  pl.BlockSpec((rows, LANES), lambda b, e, f, cnt: (b, 0)),
                  pl.BlockSpec((None, None, D_MODEL, tf), lambda b, e, f, cnt: (e, f, 0, 0)),
                  pl.BlockSpec((None, None, D_MODEL, tf), lambda b, e, f, cnt: (e, f, 0, 0)),
                  pl.BlockSpec((None, tf, D_MODEL), lambda b, e, f, cnt: (e, f, 0))],
        out_specs=pl.BlockSpec((rows, D_MODEL), lambda b, e, f, cnt: (b, 0)),
        scratch_shapes=[pltpu.VMEM((group * region, D_MODEL), BF16),
                        pltpu.VMEM((group * region, D_MODEL), F32),
                        pltpu.VMEM((group * region, 1), F32)])
    out = pl.pallas_call(
        functools.partial(_moe_body, slot_tile=slot_tile, route_block=route_block),
        grid_spec=grid_spec,
        out_shape=jax.ShapeDtypeStruct((tokens, D_MODEL), F32),
        compiler_params=_params("arbitrary", "arbitrary", "arbitrary"),
        name="l1_moe",
    )(cnt, h.reshape(tokens, D_MODEL), per_block_rows(rank), per_block_rows(combine),
      rank.reshape(tokens, LANES), sliced(w_gate), sliced(w_up), w_down.astype(BF16))
    return out.reshape(bsz, seq, D_MODEL)


def _post_body(x_ref, y_ref, vec_ref, out_ref):
    out_ref[...] = _post_residual(x_ref[...], y_ref[...], vec_ref[0:1, :], vec_ref[1:2, :])


def _post(x, y, vec, tm=1024):
    bsz, seq, _ = x.shape
    seq_spec = pl.BlockSpec((None, tm, D_MODEL), lambda b, i: (b, i, 0))
    return pl.pallas_call(
        _post_body,
        grid=(bsz, seq // tm),
        in_specs=[seq_spec, seq_spec, pl.BlockSpec((None, 8, D_MODEL), lambda b, i: (b, 0, 0))],
        out_specs=seq_spec,
        out_shape=jax.ShapeDtypeStruct((bsz, seq, D_MODEL), F32),
        compiler_params=_params("arbitrary", "arbitrary"),
        name="l1_moe_residual",
    )(x, y, vec)


def kernel(x, c, positions, ada_w, ada_b, mix_pre_g, mix_post_g, ffn_pre_g, ffn_post_g, even_w_in, even_pool_w, even_pool_scale, even_w_out, even_ffn_w_gate, even_ffn_w_up, even_ffn_w_down, odd_w_in, odd_conv_w, odd_a_log, odd_dt_bias, odd_norm_g, odd_w_out, odd_router_w, odd_moe_w_gate, odd_moe_w_up, odd_moe_w_down):
    bsz = x.shape[0]
    mod = _adaln_mod(c, ada_w, ada_b)
    sh1, sc1, gt1, sh2, sc2, gt2 = (mod[:, :, n * D_MODEL:(n + 1) * D_MODEL] for n in range(6))

    vec = _vec_rows([mix_pre_g[0], sc1[0], sh1[0]], bsz)
    qs, ks, vs, o_b = _in0(x, positions, vec, even_w_in[0], even_pool_w[0], even_pool_scale[0])
    branches = [_banded_attention(q, k, v, d) for q, k, v, d in zip(qs, ks, vs, DILATIONS)]
    vec = _vec_rows([mix_post_g[0], gt1[0]], bsz)
    x = _out0([o for o, _ in branches], [l for _, l in branches], o_b, x, vec, even_w_out[0])
    vec = _vec_rows([ffn_pre_g[0], sc2[0], sh2[0], ffn_post_g[0], gt2[0]], bsz)
    x = _ffn(x, vec, even_ffn_w_gate[0], even_ffn_w_up[0], even_ffn_w_down[0])

    vec = _vec_rows([mix_pre_g[1], sc1[1], sh1[1]], bsz)
    q, k, v, gate, bg = _in1(x, vec, odd_w_in[0], odd_conv_w[0], odd_a_log[0], odd_dt_bias[0])
    o = _delta(q, k, v, gate, bg, odd_norm_g[0])
    vec = _vec_rows([mix_post_g[1], gt1[1], ffn_pre_g[1], sc2[1], sh2[1]], bsz)
    x, h, combine, rank, counts = _out1(o, x, vec, odd_w_out[0], odd_router_w[0])
    y = _moe(h, combine, rank, counts, odd_moe_w_gate[0], odd_moe_w_up[0], odd_moe_w_down[0])
    vec = _vec_rows([ffn_post_g[1], gt2[1]], bsz)
    return _post(x, y, vec)
```

```python
import functools

import numpy as np
import jax
import jax.numpy as jnp
from jax import lax
from jax.experimental import pallas as pl
from jax.experimental.pallas import tpu as pltpu

F32 = jnp.float32
BF16 = jnp.bfloat16
HIGHEST = lax.Precision.HIGHEST

D_MODEL = 1024
DEPTH = 2
A_HEADS = 8
A_HEAD_DIM = 64
A_WIDTH = 512
DILATIONS = (1, 4, 16)
BAND = 128
ROT_DIM = 16
ROPE_THETA = 500000.0
POOL_WINDOWS = (2, 4, 8, 16)
POOL_GROUP_DIM = 128
POOL_WIDTH = 512
POOL_HALO = 16
DN_HEADS = 8
DN_HEAD_DIM = 128
DN_WIDTH = 1024
CONV_WIDTH = 4
CONV_HALO = 8
CHUNK = 64
PREP_CHUNKS = 4
FFN_DIM = 3584
N_EXPERTS = 8
EPS = 1e-6
LANES = 128
ROUTE_BLOCK = 2048
SEGMENTS = (512, 256, 128)
NEG_INF = float("-inf")

VMEM_LIMIT = 56 * 1024 * 1024


def _params(*sem):
    return pltpu.CompilerParams(dimension_semantics=sem, vmem_limit_bytes=VMEM_LIMIT)


def _dot(a, b, precision=None):
    return jnp.dot(a, b, preferred_element_type=F32, precision=precision)


def _dot_nt(a, b, precision=None):
    return lax.dot_general(a, b, (((1,), (1,)), ((), ())), preferred_element_type=F32,
                           precision=precision)


def _dot_tn(a, b, precision=None):
    return lax.dot_general(a, b, (((0,), (0,)), ((), ())), preferred_element_type=F32,
                           precision=precision)


def _silu(x):
    return x * jax.nn.sigmoid(x)


def _norm_mod(x, g, sc, sh):
    ms = jnp.mean(x * x, axis=-1, keepdims=True)
    return x * lax.rsqrt(ms + EPS) * g * (1.0 + sc) + sh


def _post_residual(x, y, g, gt):
    ms = jnp.mean(y * y, axis=-1, keepdims=True)
    return x + gt * (y * lax.rsqrt(ms + EPS) * g)


def _mod_body(c_ref, w_ref, b_ref, o_ref):
    c = c_ref[...]
    o_ref[...] = _dot(_silu(c), w_ref[...], precision=HIGHEST) + b_ref[...]


def _adaln_mod(c, ada_w, ada_b):
    bsz = c.shape[0]
    rows = 8
    c_pad = jnp.zeros((rows, D_MODEL), F32).at[:bsz].set(c)
    tn = 1536
    out = pl.pallas_call(
        _mod_body,
        grid=(DEPTH, 6 * D_MODEL // tn),
        in_specs=[pl.BlockSpec((rows, D_MODEL), lambda l, n: (0, 0)),
                  pl.BlockSpec((None, D_MODEL, tn), lambda l, n: (l, 0, n)),
                  pl.BlockSpec((None, 1, tn), lambda l, n: (l, 0, n))],
        out_specs=pl.BlockSpec((None, rows, tn), lambda l, n: (l, 0, n)),
        out_shape=jax.ShapeDtypeStruct((DEPTH, rows, 6 * D_MODEL), F32),
        compiler_params=_params("arbitrary", "arbitrary"),
        name="adaln_mod",
    )(c_pad, ada_w, ada_b.reshape(DEPTH, 1, 6 * D_MODEL))
    return out[:, :bsz]


def _vec_rows(rows, bsz):
    rows = [jnp.broadcast_to(r.astype(F32), (bsz, D_MODEL)) for r in rows]
    rows = rows + [jnp.zeros((bsz, D_MODEL), F32)] * (8 - len(rows))
    return jnp.stack(rows, axis=1)


def _rope(t, cosv, sin_lo, sin_hi):
    return (t * cosv + pltpu.roll(t, LANES - ROT_DIM // 2, axis=1) * sin_lo
            + pltpu.roll(t, ROT_DIM // 2, axis=1) * sin_hi)


def _store_folded(nat, refs):
    tm = nat.shape[1]
    for d, ref in zip(DILATIONS, refs):
        for r in range(d):
            rows = pl.ds(r, tm // d, stride=d) if d > 1 else slice(None)
            for t in range(A_WIDTH // LANES):
                col = r * A_WIDTH + t * LANES
                ref[:, col:col + LANES] = nat[t, rows, :].astype(BF16)


def _in0_body(x_ref, pos_ref, vec_ref, w_ref, tab_ref, pw_ref, ps_ref,
              q1_ref, q4_ref, q16_ref, k1_ref, k4_ref, k16_ref, v1_ref, v4_ref, v16_ref, ob_ref,
              ubuf, nat):
    i = pl.program_id(1)
    tm = x_ref.shape[0]

    @pl.when(i == 0)
    def _():
        ubuf[0:POOL_HALO, :] = jnp.zeros((POOL_HALO, POOL_WIDTH), F32)

    h = _norm_mod(x_ref[...], vec_ref[0:1, :], vec_ref[1:2, :], vec_ref[2:3, :]).astype(BF16)

    ang = pos_ref[...].astype(F32) * tab_ref[0:1, :]
    cosv = jnp.cos(ang)
    sinv = jnp.sin(ang)
    sin_lo = -sinv * tab_ref[1:2, :]
    sin_hi = sinv * tab_ref[2:3, :]

    pq = _dot(h, w_ref[:, 0:A_WIDTH])
    for t in range(A_WIDTH // LANES):
        cols = slice(t * LANES, (t + 1) * LANES)
        nat[t] = _rope(pq[:, cols], cosv, sin_lo, sin_hi) * (A_HEAD_DIM ** -0.5)
    _store_folded(nat, (q1_ref, q4_ref, q16_ref))
    pk = _dot(h, w_ref[:, A_WIDTH:2 * A_WIDTH])
    for t in range(A_WIDTH // LANES):
        cols = slice(t * LANES, (t + 1) * LANES)
        nat[t] = _rope(pk[:, cols], cosv, sin_lo, sin_hi)
    _store_folded(nat, (k1_ref, k4_ref, k16_ref))
    pv = _dot(h, w_ref[:, 2 * A_WIDTH:3 * A_WIDTH])
    for t in range(A_WIDTH // LANES):
        nat[t] = pv[:, t * LANES:(t + 1) * LANES]
    _store_folded(nat, (v1_ref, v4_ref, v16_ref))

    ubuf[POOL_HALO:POOL_HALO + tm, :] = _dot(h, w_ref[:, 3 * A_WIDTH:3 * A_WIDTH + POOL_WIDTH])
    tpos = i * tm + lax.broadcasted_iota(jnp.int32, (tm, 1), 0)
    for g, win in enumerate(POOL_WINDOWS):
        cols = slice(g * POOL_GROUP_DIM, (g + 1) * POOL_GROUP_DIM)
        cur = ubuf[POOL_HALO:POOL_HALO + tm, cols]
        acc = cur
        for j in range(1, win):
            acc = acc + ubuf[POOL_HALO - j:POOL_HALO - j + tm, cols]
        cnt = jnp.minimum(tpos + 1, win).astype(F32)
        pooled = acc / cnt - cur
        mixed = _dot(pooled.astype(BF16), pw_ref[g]) * ps_ref[0:1, cols]
        ob_ref[:, cols] = mixed.astype(BF16)
    ubuf[0:POOL_HALO, :] = ubuf[tm:tm + POOL_HALO, :]


def _rope_table():
    lane = np.arange(LANES) % A_HEAD_DIM
    half = ROT_DIM // 2
    inv_freq = ROPE_THETA ** (-jnp.arange(0, ROT_DIM, 2, dtype=F32) / ROT_DIM)
    freq = jnp.where(jnp.asarray(lane < ROT_DIM), inv_freq[jnp.asarray(lane % half)], 0.0)
    lo = jnp.asarray((lane < half).astype(np.float32))
    hi = jnp.asarray(((lane >= half) & (lane < ROT_DIM)).astype(np.float32))
    return jnp.stack([freq, lo, hi] + [jnp.zeros((LANES,), F32)] * 5)


def _in0(x, positions, vec, w_in, pool_w, pool_scale, tm=512):
    bsz, seq, _ = x.shape
    n_out = 3 * A_WIDTH + POOL_WIDTH
    tab = _rope_table()
    seq_spec = lambda width: pl.BlockSpec((None, tm, width), lambda b, i: (b, i, 0))
    full = lambda shape: pl.BlockSpec(shape, lambda b, i: (0,) * len(shape))
    folded_spec = lambda d: pl.BlockSpec((None, tm // d, d * A_WIDTH), lambda b, i: (b, i, 0))
    folded_sds = lambda d: jax.ShapeDtypeStruct((bsz, seq // d, d * A_WIDTH), BF16)
    outs = pl.pallas_call(
        _in0_body,
        grid=(bsz, seq // tm),
        in_specs=[seq_spec(D_MODEL), seq_spec(1),
                  pl.BlockSpec((None, 8, D_MODEL), lambda b, i: (b, 0, 0)),
                  full((D_MODEL, n_out)), full((8, LANES)),
                  full((len(POOL_WINDOWS), POOL_GROUP_DIM, POOL_GROUP_DIM)), full((1, POOL_WIDTH))],
        out_specs=[folded_spec(d) for d in DILATIONS] * 3 + [seq_spec(POOL_WIDTH)],
        out_shape=[folded_sds(d) for d in DILATIONS] * 3 + [folded_sds(1)],
        scratch_shapes=[pltpu.VMEM((tm + POOL_HALO, POOL_WIDTH), F32),
                        pltpu.VMEM((A_WIDTH // LANES, tm, LANES), F32)],
        compiler_params=_params("arbitrary", "arbitrary"),
        name="l0_in_proj",
    )(x, positions.reshape(bsz, seq, 1), vec, w_in.astype(BF16), tab,
      pool_w.astype(BF16), pool_scale.reshape(1, POOL_WIDTH))
    n = len(DILATIONS)
    return outs[0:n], outs[n:2 * n], outs[2 * n:3 * n], outs[3 * n]


def _attn_body(q_ref, kp_ref, kc_ref, vp_ref, vc_ref, o_ref, lse_ref):
    i = pl.program_id(2)
    n_bands = q_ref.shape[0] // BAND
    qi = lax.broadcasted_iota(jnp.int32, (BAND, BAND), 0)
    kj = lax.broadcasted_iota(jnp.int32, (BAND, BAND), 1)
    back = kj >= qi
    mask_cur = kj <= qi
    lane = lax.broadcasted_iota(jnp.int32, (BAND, LANES), 1)
    first = lane < A_HEAD_DIM
    pairs = range(A_WIDTH // LANES)
    cols = [slice(g * LANES, (g + 1) * LANES) for g in pairs]
    rows = [slice(s * BAND, (s + 1) * BAND) for s in range(n_bands)]
    probs = [(s, g, half) for s in range(n_bands) for g in pairs for half in range(2)]

    def prev_of(ref_prev, ref_cur, s, g):
        return ref_prev[:, cols[g]] if s == 0 else ref_cur[rows[s - 1], cols[g]]

    qh = []
    for s, g, half in probs:
        qp = q_ref[rows[s], cols[g]]
        qh.append(jnp.where(first if half == 0 else jnp.logical_not(first), qp, jnp.zeros_like(qp)))
    sp = [jnp.where(back & (i > 0) if s == 0 else back, _dot_nt(q, prev_of(kp_ref, kc_ref, s, g)), NEG_INF)
          for q, (s, g, _) in zip(qh, probs)]
    sc = [jnp.where(mask_cur, _dot_nt(q, kc_ref[rows[s], cols[g]]), NEG_INF)
          for q, (s, g, _) in zip(qh, probs)]
    m = [jnp.maximum(jnp.max(a, axis=-1, keepdims=True), jnp.max(b, axis=-1, keepdims=True))
         for a, b in zip(sp, sc)]
    pp = [jnp.exp(a - mm) for a, mm in zip(sp, m)]
    pc = [jnp.exp(a - mm) for a, mm in zip(sc, m)]
    den = [jnp.sum(a, axis=-1, keepdims=True) + jnp.sum(b, axis=-1, keepdims=True)
           for a, b in zip(pp, pc)]
    outs = [(_dot(a.astype(BF16), prev_of(vp_ref, vc_ref, s, g))
             + _dot(b.astype(BF16), vc_ref[rows[s], cols[g]])) / d
            for a, b, d, (s, g, _) in zip(pp, pc, den, probs)]
    per_band = 2 * len(pairs)
    for s in range(n_bands):
        lse_tile = jnp.zeros((BAND, LANES), F32)
        for idx in range(per_band):
            n = s * per_band + idx
            lse_tile = jnp.where(lane == idx, m[n] + jnp.log(den[n]), lse_tile)
        lse_ref[rows[s], :] = lse_tile
        for g in pairs:
            n = s * per_band + 2 * g
            o_ref[rows[s], cols[g]] = jnp.where(first, outs[n], outs[n + 1]).astype(BF16)


def _banded_attention(q, k, v, dilation, bands=2):
    bsz, n_sub, _ = q.shape
    rows = bands * BAND
    cur = pl.BlockSpec((None, rows, A_WIDTH), lambda b, r, i: (b, i, r))
    prev = pl.BlockSpec((None, BAND, A_WIDTH), lambda b, r, i: (b, jnp.maximum(i * bands - 1, 0), r))
    o, lse = pl.pallas_call(
        _attn_body,
        grid=(bsz, dilation, n_sub // rows),
        in_specs=[cur, prev, cur, prev, cur],
        out_specs=[cur, pl.BlockSpec((None, rows, LANES), lambda b, r, i: (b, i, r))],
        out_shape=[jax.ShapeDtypeStruct((bsz, n_sub, dilation * A_WIDTH), BF16),
                   jax.ShapeDtypeStruct((bsz, n_sub, dilation * LANES), F32)],
        compiler_params=_params("arbitrary", "arbitrary", "arbitrary"),
        name=f"dilated_attn_d{dilation}",
    )(q, k, k, v, v)
    return o, lse


def _out0_body(o1_ref, o4_ref, o16_ref, l1_ref, l4_ref, l16_ref, ob_ref, x_ref, vec_ref, w_ref,
               out_ref, o_nat, l_nat):
    tm = x_ref.shape[0]
    for slot, (d, o_ref, l_ref) in enumerate(zip(DILATIONS, (o1_ref, o4_ref, o16_ref),
                                                  (l1_ref, l4_ref, l16_ref))):
        for r in range(d):
            rows = pl.ds(r, tm // d, stride=d) if d > 1 else slice(None)
            for t in range(A_WIDTH // LANES):
                col = r * A_WIDTH + t * LANES
                o_nat[slot, t, rows, :] = o_ref[:, col:col + LANES].astype(F32)
            l_nat[slot, rows, :] = l_ref[:, r * LANES:(r + 1) * LANES]
    la, lb, lc = l_nat[0], l_nat[1], l_nat[2]
    m = jnp.maximum(jnp.maximum(la, lb), lc)
    ea, eb, ec = jnp.exp(la - m), jnp.exp(lb - m), jnp.exp(lc - m)
    tot = ea + eb + ec
    weights = (ea / tot, eb / tot, ec / tot)
    lane = lax.broadcasted_iota(jnp.int32, (tm, LANES), 1)
    first = lane < A_HEAD_DIM
    pieces = []
    for g in range(A_WIDTH // LANES):
        cols = slice(g * LANES, (g + 1) * LANES)
        acc = jnp.zeros((tm, LANES), F32)
        for slot, wt in enumerate(weights):
            w_pair = jnp.where(first,
                               jnp.broadcast_to(wt[:, 2 * g:2 * g + 1], (tm, LANES)),
                               jnp.broadcast_to(wt[:, 2 * g + 1:2 * g + 2], (tm, LANES)))
            acc = acc + w_pair * o_nat[slot, g]
        pieces.append(acc.astype(BF16))
    o_a = jnp.concatenate(pieces, axis=-1)
    y = _dot(o_a, w_ref[0:A_WIDTH, :]) + _dot(ob_ref[...], w_ref[A_WIDTH:A_WIDTH + POOL_WIDTH, :])
    out_ref[...] = _post_residual(x_ref[...], y, vec_ref[0:1, :], vec_ref[1:2, :])


def _out0(os, lses, ob, x, vec, w_out, tm=512):
    bsz, seq, _ = x.shape
    seq_spec = lambda width: pl.BlockSpec((None, tm, width), lambda b, i: (b, i, 0))
    folded = lambda width: [pl.BlockSpec((None, tm // d, d * width), lambda b, i: (b, i, 0))
                            for d in DILATIONS]
    n = len(DILATIONS)
    return pl.pallas_call(
        _out0_body,
        grid=(bsz, seq // tm),
        in_specs=folded(A_WIDTH) + folded(LANES) + [seq_spec(POOL_WIDTH),
                  seq_spec(D_MODEL), pl.BlockSpec((None, 8, D_MODEL), lambda b, i: (b, 0, 0)),
                  pl.BlockSpec((A_WIDTH + POOL_WIDTH, D_MODEL), lambda b, i: (0, 0))],
        out_specs=seq_spec(D_MODEL),
        out_shape=jax.ShapeDtypeStruct((bsz, seq, D_MODEL), F32),
        scratch_shapes=[pltpu.VMEM((n, A_WIDTH // LANES, tm, LANES), F32),
                        pltpu.VMEM((n, tm, LANES), F32)],
        compiler_params=_params("arbitrary", "arbitrary"),
        name="l0_out_proj",
    )(*os, *lses, ob, x, vec, w_out.astype(BF16))


def _ffn_body(x_ref, vec_ref, wg_ref, wu_ref, wd_ref, out_ref, hbuf, acc):
    f = pl.program_id(1)

    @pl.when(f == 0)
    def _():
        hbuf[...] = _norm_mod(x_ref[...], vec_ref[0:1, :], vec_ref[1:2, :],
                              vec_ref[2:3, :]).astype(BF16)
        acc[...] = jnp.zeros_like(acc)

    h = hbuf[...]
    act = _silu(_dot(h, wg_ref[...])) * _dot(h, wu_ref[...])
    acc[...] += _dot(act.astype(BF16), wd_ref[...])

    @pl.when(f == pl.num_programs(1) - 1)
    def _():
        out_ref[...] = _post_residual(x_ref[...], acc[...], vec_ref[3:4, :], vec_ref[4:5, :])


def _ffn(x, vec, w_gate, w_up, w_down, tm=1024, tf=512):
    bsz, seq, _ = x.shape
    tiles_per_seq = seq // tm
    xt = x.reshape(bsz * seq, D_MODEL)
    row = pl.BlockSpec((tm, D_MODEL), lambda i, f: (i, 0))
    out = pl.pallas_call(
        _ffn_body,
        grid=(bsz * seq // tm, FFN_DIM // tf),
        in_specs=[row, pl.BlockSpec((None, 8, D_MODEL), lambda i, f: (i // tiles_per_seq, 0, 0)),
                  pl.BlockSpec((D_MODEL, tf), lambda i, f: (0, f)),
                  pl.BlockSpec((D_MODEL, tf), lambda i, f: (0, f)),
                  pl.BlockSpec((tf, D_MODEL), lambda i, f: (f, 0))],
        out_specs=row,
        out_shape=jax.ShapeDtypeStruct((bsz * seq, D_MODEL), F32),
        scratch_shapes=[pltpu.VMEM((tm, D_MODEL), BF16), pltpu.VMEM((tm, D_MODEL), F32)],
        compiler_params=_params("arbitrary", "arbitrary"),
        name="l0_swiglu",
    )(xt, vec, w_gate.astype(BF16), w_up.astype(BF16), w_down.astype(BF16))
    return out.reshape(bsz, seq, D_MODEL)


def _split3(a):
    a1 = a.astype(BF16)
    r1 = a - a1.astype(F32)
    a2 = r1.astype(BF16)
    a3 = (r1 - a2.astype(F32)).astype(BF16)
    return a1, a2, a3


def _dot_split(a, b):
    a_hi = a.astype(BF16)
    a_lo = (a - a_hi.astype(F32)).astype(BF16)
    b_hi = b.astype(BF16)
    b_lo = (b - b_hi.astype(F32)).astype(BF16)
    return _dot(a_hi, b_hi) + (_dot(a_hi, b_lo) + _dot(a_lo, b_hi))


def _in1_body(x_ref, vec_ref, w_ref, wba_ref, cw_ref, hp_ref,
              q_ref, k_ref, v_ref, gate_ref, bg_ref, cbuf):
    i = pl.program_id(1)
    tm = x_ref.shape[0]

    @pl.when(i == 0)
    def _():
        cbuf[:, 0:CONV_HALO, :] = jnp.zeros((3, CONV_HALO, DN_WIDTH), F32)

    hf = _norm_mod(x_ref[...], vec_ref[0:1, :], vec_ref[1:2, :], vec_ref[2:3, :])
    h = hf.astype(BF16)

    for idx, dst in enumerate((q_ref, k_ref, v_ref)):
        cols = slice(idx * DN_WIDTH, (idx + 1) * DN_WIDTH)
        cbuf[idx, CONV_HALO:CONV_HALO + tm, :] = _dot(h, w_ref[:, cols])
        y = jnp.zeros((tm, DN_WIDTH), F32)
        for j in range(CONV_WIDTH):
            off = CONV_HALO - (CONV_WIDTH - 1) + j
            y = y + cw_ref[j:j + 1, cols] * cbuf[idx, off:off + tm, :]
        y = _silu(y)
        cbuf[idx, 0:CONV_HALO, :] = cbuf[idx, tm:tm + CONV_HALO, :]
        if idx < 2:
            scale = DN_HEAD_DIM ** -0.5 if idx == 0 else 1.0
            for hd in range(DN_HEADS):
                hc = slice(hd * DN_HEAD_DIM, (hd + 1) * DN_HEAD_DIM)
                blk = y[:, hc]
                ss = jnp.sum(blk * blk, axis=-1, keepdims=True)
                dst[:, hc] = (blk * (lax.rsqrt(ss + EPS) * scale)).astype(BF16)
        else:
            dst[...] = y.astype(BF16)

    gate_ref[...] = _silu(_dot(h, w_ref[:, 3 * DN_WIDTH:4 * DN_WIDTH])).astype(BF16)

    ba = _dot_split(hf, wba_ref[...])
    beta = jax.nn.sigmoid(ba)
    z = ba + hp_ref[1:2, :]
    softplus = jnp.maximum(z, 0.0) + jnp.log1p(jnp.exp(-jnp.abs(z)))
    g = -jnp.exp(hp_ref[0:1, :]) * softplus
    r = lax.broadcasted_iota(jnp.int32, (tm, tm), 0)
    c = lax.broadcasted_iota(jnp.int32, (tm, tm), 1)
    tri = jnp.where((r // CHUNK == c // CHUNK) & (c <= r), 1.0, 0.0).astype(BF16)
    g1, g2, g3 = _split3(g)
    gc = _dot(tri, g1) + _dot(tri, g2) + _dot(tri, g3)
    lane = lax.broadcasted_iota(jnp.int32, (tm, LANES), 1)
    bg_ref[...] = jnp.where(lane < DN_HEADS, beta, gc)


def _in1(x, vec, w_in, conv_w, a_log, dt_bias, tm=256):
    bsz, seq, _ = x.shape
    w_main = w_in[:, :4 * DN_WIDTH].astype(BF16)
    w_ba = jnp.zeros((D_MODEL, LANES), F32).at[:, :2 * DN_HEADS].set(w_in[:, 4 * DN_WIDTH:])
    hp = jnp.zeros((8, LANES), F32)
    hp = hp.at[0, DN_HEADS:2 * DN_HEADS].set(a_log).at[1, DN_HEADS:2 * DN_HEADS].set(dt_bias)
    seq_spec = lambda width: pl.BlockSpec((None, tm, width), lambda b, i: (b, i, 0))
    full = lambda shape: pl.BlockSpec(shape, lambda b, i: (0,) * len(shape))
    wide = jax.ShapeDtypeStruct((bsz, seq, DN_WIDTH), BF16)
    return pl.pallas_call(
        _in1_body,
        grid=(bsz, seq // tm),
        in_specs=[seq_spec(D_MODEL), pl.BlockSpec((None, 8, D_MODEL), lambda b, i: (b, 0, 0)),
                  full((D_MODEL, 4 * DN_WIDTH)), full((D_MODEL, LANES)),
                  full((CONV_WIDTH, 3 * DN_WIDTH)), full((8, LANES))],
        out_specs=[seq_spec(DN_WIDTH)] * 4 + [seq_spec(LANES)],
        out_shape=[wide] * 4 + [jax.ShapeDtypeStruct((bsz, seq, LANES), F32)],
        scratch_shapes=[pltpu.VMEM((3, tm + CONV_HALO, DN_WIDTH), F32)],
        compiler_params=_params("arbitrary", "arbitrary"),
        name="l1_in_proj",
    )(x, vec, w_main, w_ba, conv_w, hp)


def _unit_lower_inverses(lmats):
    n = lmats[0].shape[0]
    r = lax.broadcasted_iota(jnp.int32, (n, n), 0)
    c = lax.broadcasted_iota(jnp.int32, (n, n), 1)

    def below(size):
        return (r // (2 * size) == c // (2 * size)) & (r % (2 * size) >= size) & (c % (2 * size) < size)

    eye = jnp.where(r == c, 1.0, 0.0).astype(F32)
    first = below(1)
    invs = [eye - jnp.where(first, lm, 0.0) for lm in lmats]
    size = 2
    while size < n:
        mask = below(size)
        offs = [jnp.where(mask, lm, 0.0).astype(BF16) for lm in lmats]
        inv16 = [x.astype(BF16) for x in invs]
        xc = [_dot(x, o).astype(BF16) for x, o in zip(inv16, offs)]
        invs = [x - _dot(t, x16) for x, t, x16 in zip(invs, xc, inv16)]
        size *= 2
    return invs


def _delta_body(q_ref, k_ref, v_ref, gate_ref, bg_ref, gcr_ref, ng_ref, o_ref,
                state, p_s, n_s, qp_s, op_s, dec_s):
    i = pl.program_id(1)
    n_chunks = q_ref.shape[0] // CHUNK
    heads = range(DN_HEADS)
    head_cols = [slice(hd * DN_HEAD_DIM, (hd + 1) * DN_HEAD_DIM) for hd in heads]

    @pl.when(i == 0)
    def _():
        state[...] = jnp.zeros_like(state)

    r = lax.broadcasted_iota(jnp.int32, (CHUNK, CHUNK), 0)
    c = lax.broadcasted_iota(jnp.int32, (CHUNK, CHUNK), 1)
    causal = r >= c
    strict = r > c

    def prepare(step, carry):
        chunks = [step * PREP_CHUNKS + n for n in range(PREP_CHUNKS)]
        rows = [pl.ds(pl.multiple_of(ci * CHUNK, CHUNK), CHUNK) for ci in chunks]
        bg = [bg_ref[r, :] for r in rows]
        gcr_all = [gcr_ref[ci] for ci in chunks]
        probs = [(n, hd) for n in range(PREP_CHUNKS) for hd in heads]
        k16 = [k_ref[rows[n], head_cols[hd]] for n, hd in probs]
        q16 = [q_ref[rows[n], head_cols[hd]] for n, hd in probs]
        kf = [x.astype(F32) for x in k16]
        beta = [bg[n][:, hd:hd + 1] for n, hd in probs]
        gcc = [bg[n][:, DN_HEADS + hd:DN_HEADS + hd + 1] for n, hd in probs]
        gcr = [gcr_all[n][hd:hd + 1, :] for n, hd in probs]
        kb = [x * b for x, b in zip(kf, beta)]
        both = [_dot_nt(jnp.concatenate([a.astype(BF16), b], axis=0), x)
                for a, b, x in zip(kb, q16, k16)]
        decay = [jnp.exp(jnp.where(causal, a - b, NEG_INF)) for a, b in zip(gcc, gcr)]
        lmats = [jnp.where(strict, m[0:CHUNK] * d, 0.0) for m, d in zip(both, decay)]
        attn = [(m[CHUNK:2 * CHUNK] * d).astype(BF16) for m, d in zip(both, decay)]
        invs = _unit_lower_inverses(lmats)
        eg = [jnp.exp(x) for x in gcc]
        rhs = [jnp.concatenate([(v_ref[rows[n], head_cols[hd]].astype(F32) * b).astype(BF16),
                                (a * e).astype(BF16)], axis=1)
               for (n, hd), b, a, e in zip(probs, beta, kb, eg)]
        sol = [_dot(x.astype(BF16), y).astype(BF16) for x, y in zip(invs, rhs)]
        au = [_dot(a, s) for a, s in zip(attn, sol)]
        g_last = [x[CHUNK - 1:CHUNK, :] for x in gcc]
        kg = [(x * jnp.exp(gl - g)).astype(BF16) for x, gl, g in zip(kf, g_last, gcc)]
        kn = [_dot_tn(a, s) for a, s in zip(kg, sol)]
        for idx, (n, hd) in enumerate(probs):
            ci = chunks[n]
            n_s[ci, hd] = kn[idx][:, 0:DN_HEAD_DIM]
            p_s[ci, hd] = kn[idx][:, DN_HEAD_DIM:2 * DN_HEAD_DIM].astype(BF16)
            op_s[ci, hd] = au[idx][:, 0:DN_HEAD_DIM]
            qp_s[ci, hd] = (q16[idx].astype(F32) * eg[idx]
                            - au[idx][:, DN_HEAD_DIM:2 * DN_HEAD_DIM]).astype(BF16)
            dec_s[ci, hd] = jnp.broadcast_to(jnp.exp(g_last[idx]), (8, DN_HEAD_DIM))
        return carry

    def scan(ci, carry):
        rows = pl.ds(pl.multiple_of(ci * CHUNK, CHUNK), CHUNK)
        for hd, hc in zip(heads, head_cols):
            s = state[hd]
            s16 = s.astype(BF16)
            o = _dot(qp_s[ci, hd], s16) + op_s[ci, hd]
            state[hd] = s * dec_s[ci, hd][0:1, :] + n_s[ci, hd] - _dot(p_s[ci, hd], s16)
            ms = jnp.mean(o * o, axis=-1, keepdims=True)
            o = o * lax.rsqrt(ms + EPS) * ng_ref[0:1, :] * gate_ref[rows, hc].astype(F32)
            o_ref[rows, hc] = o.astype(BF16)
        return carry

    lax.fori_loop(0, n_chunks // PREP_CHUNKS, prepare, 0)
    lax.fori_loop(0, n_chunks, scan, 0, unroll=4)


def _delta(q, k, v, gate, bg, norm_g, block=512):
    bsz, seq, _ = q.shape
    n = seq // CHUNK
    nc = block // CHUNK
    gc_rows = bg[:, :, DN_HEADS:2 * DN_HEADS].reshape(bsz, n, CHUNK, DN_HEADS).transpose(0, 1, 3, 2)
    seq_spec = lambda width: pl.BlockSpec((None, block, width), lambda b, i: (b, i, 0))
    per_head = lambda rows, dtype: pltpu.VMEM((nc, DN_HEADS, rows, DN_HEAD_DIM), dtype)
    return pl.pallas_call(
        _delta_body,
        grid=(bsz, seq // block),
        in_specs=[seq_spec(DN_WIDTH)] * 4 + [seq_spec(LANES),
                  pl.BlockSpec((None, nc, DN_HEADS, CHUNK), lambda b, i: (b, i, 0, 0)),
                  pl.BlockSpec((1, DN_HEAD_DIM), lambda b, i: (0, 0))],
        out_specs=seq_spec(DN_WIDTH),
        out_shape=jax.ShapeDtypeStruct((bsz, seq, DN_WIDTH), BF16),
        scratch_shapes=[pltpu.VMEM((DN_HEADS, DN_HEAD_DIM, DN_HEAD_DIM), F32),
                        per_head(DN_HEAD_DIM, BF16), per_head(DN_HEAD_DIM, F32),
                        per_head(CHUNK, BF16), per_head(CHUNK, F32), per_head(8, F32)],
        compiler_params=_params("arbitrary", "arbitrary"),
        name="gated_delta",
    )(q, k, v, gate, bg, gc_rows, norm_g.reshape(1, DN_HEAD_DIM))


def _route(hf, rw):
    tm = hf.shape[0]
    lane = lax.broadcasted_iota(jnp.int32, (tm, LANES), 1)
    logits = jnp.where(lane < N_EXPERTS, _dot_split(hf, rw), NEG_INF)
    m1 = jnp.max(logits, axis=-1, keepdims=True)
    i1 = jnp.min(jnp.where(logits == m1, lane, LANES), axis=-1, keepdims=True)
    rest = jnp.where(lane == i1, NEG_INF, logits)
    m2 = jnp.max(rest, axis=-1, keepdims=True)
    i2 = jnp.min(jnp.where(rest == m2, lane, LANES), axis=-1, keepdims=True)
    e2 = jnp.exp(m2 - m1)
    w1 = 1.0 / (1.0 + e2)
    w2 = e2 / (1.0 + e2)
    combine = jnp.where(lane == i1, w1, 0.0) + jnp.where(lane == i2, w2, 0.0)
    chosen = jnp.where((lane == i1) | (lane == i2), 1.0, 0.0)
    return combine, chosen


def _out1_body(a_ref, x_ref, vec_ref, w_ref, rw_ref,
               x_out, h_out, cw_out, rank_out, cnt_out, running, *, tiles_per_block):
    i = pl.program_id(1)
    tm = x_ref.shape[0]

    @pl.when(i % tiles_per_block == 0)
    def _():
        running[...] = jnp.zeros_like(running)

    y = _dot(a_ref[...], w_ref[...])
    x2 = _post_residual(x_ref[...], y, vec_ref[0:1, :], vec_ref[1:2, :])
    x_out[...] = x2
    hf = _norm_mod(x2, vec_ref[2:3, :], vec_ref[3:4, :], vec_ref[4:5, :])
    h_out[...] = hf.astype(BF16)
    combine, chosen = _route(hf, rw_ref[...])
    cw_out[...] = combine
    r = lax.broadcasted_iota(jnp.int32, (tm, tm), 0)
    c = lax.broadcasted_iota(jnp.int32, (tm, tm), 1)
    before = jnp.where(c < r, 1.0, 0.0).astype(BF16)
    rank = _dot(before, chosen.astype(BF16)) + running[0:1, :]
    rank_out[...] = jnp.where(chosen > 0.0, rank, -1.0)
    running[...] = running[...] + jnp.sum(chosen, axis=0, keepdims=True)
    cnt_out[...] = running[...]


def _out1(a, x, vec, w_out, router_w, tm=512, route_block=ROUTE_BLOCK):
    bsz, seq, _ = x.shape
    tiles_per_block = route_block // tm
    blocks_per_seq = seq // route_block
    rw = jnp.zeros((D_MODEL, LANES), F32).at[:, :N_EXPERTS].set(router_w)
    seq_spec = lambda width: pl.BlockSpec((None, tm, width), lambda b, i: (b, i, 0))
    return pl.pallas_call(
        functools.partial(_out1_body, tiles_per_block=tiles_per_block),
        grid=(bsz, seq // tm),
        in_specs=[seq_spec(DN_WIDTH), seq_spec(D_MODEL),
                  pl.BlockSpec((None, 8, D_MODEL), lambda b, i: (b, 0, 0)),
                  pl.BlockSpec((DN_WIDTH, D_MODEL), lambda b, i: (0, 0)),
                  pl.BlockSpec((D_MODEL, LANES), lambda b, i: (0, 0))],
        out_specs=[seq_spec(D_MODEL), seq_spec(D_MODEL), seq_spec(LANES), seq_spec(LANES),
                   pl.BlockSpec((None, 8, LANES),
                                lambda b, i: (b * blocks_per_seq + i // tiles_per_block, 0, 0))],
        out_shape=[jax.ShapeDtypeStruct((bsz, seq, D_MODEL), F32),
                   jax.ShapeDtypeStruct((bsz, seq, D_MODEL), BF16),
                   jax.ShapeDtypeStruct((bsz, seq, LANES), F32),
                   jax.ShapeDtypeStruct((bsz, seq, LANES), F32),
                   jax.ShapeDtypeStruct((bsz * blocks_per_seq, 8, LANES), F32)],
        scratch_shapes=[pltpu.VMEM((8, LANES), F32)],
        compiler_params=_params("arbitrary", "arbitrary"),
        name="l1_out_proj_route",
    )(a, x, vec, w_out.astype(BF16), rw)


def _moe_body(cnt_ref, h_ref, rrow_ref, wrow_ref, rcol_ref, wg_ref, wu_ref, wd_ref, out_ref,
              xs, ys, wslot):
    b, e, f = pl.program_id(0), pl.program_id(1), pl.program_id(2)
    tb = h_ref.shape[0]

    @pl.when((e == 0) & (f == 0))
    def _():
        out_ref[...] = jnp.zeros_like(out_ref)

    count = cnt_ref[b * N_EXPERTS + e]
    padded = (count + SEGMENTS[-1] - 1) // SEGMENTS[-1] * SEGMENTS[-1]
    n_large = padded // SEGMENTS[0]

    def for_each_segment(fn):
        def large(t, carry):
            fn(pl.multiple_of(t * SEGMENTS[0], SEGMENTS[0]), SEGMENTS[0])
            return carry

        lax.fori_loop(0, n_large, large, 0)
        done = n_large * SEGMENTS[0]
        for size in SEGMENTS[1:]:
            take = ((padded - done) // size) > 0

            @pl.when(take)
            def _(done=done, size=size):
                fn(pl.multiple_of(done, SEGMENTS[-1]), size)

            done = done + jnp.where(take, size, 0)

    @pl.when(f == 0)
    def _():
        rrow = rrow_ref[pl.ds(e, 1), :]
        wrow = wrow_ref[pl.ds(e, 1), :]

        def pack(base, size):
            rows = pl.ds(base, size)
            slot = (base + lax.broadcasted_iota(jnp.int32, (size, 1), 0)).astype(F32)
            hit = rrow == slot
            xs[rows, :] = _dot(jnp.where(hit, 1.0, 0.0).astype(BF16), h_ref[...]).astype(BF16)
            wslot[rows, :] = jnp.sum(jnp.where(hit, wrow, 0.0), axis=-1, keepdims=True)
            ys[rows, :] = jnp.zeros((size, D_MODEL), F32)

        for_each_segment(pack)

    def expert(base, size):
        rows = pl.ds(base, size)
        x = xs[rows, :]
        act = _silu(_dot(x, wg_ref[...])) * _dot(x, wu_ref[...])
        ys[rows, :] += _dot(act.astype(BF16), wd_ref[...])

    for_each_segment(expert)

    @pl.when(f == pl.num_programs(2) - 1)
    def _():
        lane = lax.broadcasted_iota(jnp.int32, (tb, LANES), 1)
        rcol = jnp.sum(jnp.where(lane == e, rcol_ref[...], 0.0), axis=-1, keepdims=True)

        def unpack(base, size):
            rows = pl.ds(base, size)
            y = (ys[rows, :] * wslot[rows, :]).astype(BF16)
            slot = (base + lax.broadcasted_iota(jnp.int32, (1, size), 1)).astype(F32)
            back = jnp.where(rcol == slot, 1.0, 0.0).astype(BF16)
            out_ref[...] += _dot(back, y)

        for_each_segment(unpack)


def _moe(h, combine, rank, counts, w_gate, w_up, w_down, route_block=ROUTE_BLOCK, tf=512):
    bsz, seq, _ = h.shape
    tokens = bsz * seq
    n_blocks = tokens // route_block
    per_block_rows = lambda t: (t.reshape(n_blocks, route_block, LANES)[:, :, :N_EXPERTS]
                                .transpose(0, 2, 1))
    cnt = counts[:, 0, :N_EXPERTS].astype(jnp.int32).reshape(n_blocks * N_EXPERTS)
    assert route_block % SEGMENTS[-1] == 0
    grid_spec = pltpu.PrefetchScalarGridSpec(
        num_scalar_prefetch=1,
        grid=(n_blocks, N_EXPERTS, FFN_DIM // tf),
        in_specs=[pl.BlockSpec((route_block, D_MODEL), lambda b, e, f, cnt: (b, 0)),
                  pl.BlockSpec((None, N_EXPERTS, route_block), lambda b, e, f, cnt: (b, 0, 0)),
                  pl.BlockSpec((None, N_EXPERTS, route_block), lambda b, e, f, cnt: (b, 0, 0)),
                  pl.BlockSpec((route_block, LANES), lambda b, e, f, cnt: (b, 0)),
                  pl.BlockSpec((None, D_MODEL, tf), lambda b, e, f, cnt: (e, 0, f)),
                  pl.BlockSpec((None, D_MODEL, tf), lambda b, e, f, cnt: (e, 0, f)),
                  pl.BlockSpec((None, tf, D_MODEL), lambda b, e, f, cnt: (e, f, 0))],
        out_specs=pl.BlockSpec((route_block, D_MODEL), lambda b, e, f, cnt: (b, 0)),
        scratch_shapes=[pltpu.VMEM((route_block, D_MODEL), BF16), pltpu.VMEM((route_block, D_MODEL), F32),
                        pltpu.VMEM((route_block, 1), F32)])
    out = pl.pallas_call(
        _moe_body,
        grid_spec=grid_spec,
        out_shape=jax.ShapeDtypeStruct((tokens, D_MODEL), F32),
        compiler_params=_params("arbitrary", "arbitrary", "arbitrary"),
        name="l1_moe",
    )(cnt, h.reshape(tokens, D_MODEL), per_block_rows(rank), per_block_rows(combine),
      rank.reshape(tokens, LANES), w_gate.astype(BF16), w_up.astype(BF16), w_down.astype(BF16))
    return out.reshape(bsz, seq, D_MODEL)


def _post_body(x_ref, y_ref, vec_ref, out_ref):
    out_ref[...] = _post_residual(x_ref[...], y_ref[...], vec_ref[0:1, :], vec_ref[1:2, :])


def _post(x, y, vec, tm=1024):
    bsz, seq, _ = x.shape
    seq_spec = pl.BlockSpec((None, tm, D_MODEL), lambda b, i: (b, i, 0))
    return pl.pallas_call(
        _post_body,
        grid=(bsz, seq // tm),
        in_specs=[seq_spec, seq_spec, pl.BlockSpec((None, 8, D_MODEL), lambda b, i: (b, 0, 0))],
        out_specs=seq_spec,
        out_shape=jax.ShapeDtypeStruct((bsz, seq, D_MODEL), F32),
        compiler_params=_params("arbitrary", "arbitrary"),
        name="l1_moe_residual",
    )(x, y, vec)


def kernel(x, c, positions, ada_w, ada_b, mix_pre_g, mix_post_g, ffn_pre_g, ffn_post_g, even_w_in, even_pool_w, even_pool_scale, even_w_out, even_ffn_w_gate, even_ffn_w_up, even_ffn_w_down, odd_w_in, odd_conv_w, odd_a_log, odd_dt_bias, odd_norm_g, odd_w_out, odd_router_w, odd_moe_w_gate, odd_moe_w_up, odd_moe_w_down):
    bsz = x.shape[0]
    mod = _adaln_mod(c, ada_w, ada_b)
    sh1, sc1, gt1, sh2, sc2, gt2 = (mod[:, :, n * D_MODEL:(n + 1) * D_MODEL] for n in range(6))

    vec = _vec_rows([mix_pre_g[0], sc1[0], sh1[0]], bsz)
    qs, ks, vs, o_b = _in0(x, positions, vec, even_w_in[0], even_pool_w[0], even_pool_scale[0])
    branches = [_banded_attention(q, k, v, d) for q, k, v, d in zip(qs, ks, vs, DILATIONS)]
    vec = _vec_rows([mix_post_g[0], gt1[0]], bsz)
    x = _out0([o for o, _ in branches], [l for _, l in branches], o_b, x, vec, even_w_out[0])
    vec = _vec_rows([ffn_pre_g[0], sc2[0], sh2[0], ffn_post_g[0], gt2[0]], bsz)
    x = _ffn(x, vec, even_ffn_w_gate[0], even_ffn_w_up[0], even_ffn_w_down[0])

    vec = _vec_rows([mix_pre_g[1], sc1[1], sh1[1]], bsz)
    q, k, v, gate, bg = _in1(x, vec, odd_w_in[0], odd_conv_w[0], odd_a_log[0], odd_dt_bias[0])
    o = _delta(q, k, v, gate, bg, odd_norm_g[0])
    vec = _vec_rows([mix_post_g[1], gt1[1], ffn_pre_g[1], sc2[1], sh2[1]], bsz)
    x, h, combine, rank, counts = _out1(o, x, vec, odd_w_out[0], odd_router_w[0])
    y = _moe(h, combine, rank, counts, odd_moe_w_gate[0], odd_moe_w_up[0], odd_moe_w_down[0])
    vec = _vec_rows([ffn_post_g[1], gt2[1]], bsz)
    return _post(x, y, vec)
```

```python
import functools

import numpy as np
import jax
import jax.numpy as jnp
from jax import lax
from jax.experimental import pallas as pl
from jax.experimental.pallas import tpu as pltpu

F32 = jnp.float32
BF16 = jnp.bfloat16
HIGHEST = lax.Precision.HIGHEST

D_MODEL = 1024
DEPTH = 2
A_HEADS = 8
A_HEAD_DIM = 64
A_WIDTH = 512
DILATIONS = (1, 4, 16)
BAND = 128
ROT_DIM = 16
ROPE_THETA = 500000.0
POOL_WINDOWS = (2, 4, 8, 16)
POOL_GROUP_DIM = 128
POOL_WIDTH = 512
POOL_HALO = 16
DN_HEADS = 8
DN_HEAD_DIM = 128
DN_WIDTH = 1024
CONV_WIDTH = 4
CONV_HALO = 8
CHUNK = 64
PREP_CHUNKS = 4
FFN_DIM = 3584
N_EXPERTS = 8
EPS = 1e-6
LANES = 128
ROUTE_BLOCK = 2048
SEGMENTS = (512, 256, 128)
NEG_INF = float("-inf")

VMEM_LIMIT = 56 * 1024 * 1024


def _params(*sem):
    return pltpu.CompilerParams(dimension_semantics=sem, vmem_limit_bytes=VMEM_LIMIT)


def _dot(a, b, precision=None):
    return jnp.dot(a, b, preferred_element_type=F32, precision=precision)


def _dot_nt(a, b, precision=None):
    return lax.dot_general(a, b, (((1,), (1,)), ((), ())), preferred_element_type=F32,
                           precision=precision)


def _dot_tn(a, b, precision=None):
    return lax.dot_general(a, b, (((0,), (0,)), ((), ())), preferred_element_type=F32,
                           precision=precision)


def _silu(x):
    return x * jax.nn.sigmoid(x)


def _norm_mod(x, g, sc, sh):
    ms = jnp.mean(x * x, axis=-1, keepdims=True)
    return x * lax.rsqrt(ms + EPS) * g * (1.0 + sc) + sh


def _post_residual(x, y, g, gt):
    ms = jnp.mean(y * y, axis=-1, keepdims=True)
    return x + gt * (y * lax.rsqrt(ms + EPS) * g)


def _mod_body(c_ref, w_ref, b_ref, o_ref):
    c = c_ref[...]
    o_ref[...] = _dot(_silu(c), w_ref[...], precision=HIGHEST) + b_ref[...]


def _adaln_mod(c, ada_w, ada_b):
    bsz = c.shape[0]
    rows = 8
    c_pad = jnp.zeros((rows, D_MODEL), F32).at[:bsz].set(c)
    tn = 1536
    out = pl.pallas_call(
        _mod_body,
        grid=(DEPTH, 6 * D_MODEL // tn),
        in_specs=[pl.BlockSpec((rows, D_MODEL), lambda l, n: (0, 0)),
                  pl.BlockSpec((None, D_MODEL, tn), lambda l, n: (l, 0, n)),
                  pl.BlockSpec((None, 1, tn), lambda l, n: (l, 0, n))],
        out_specs=pl.BlockSpec((None, rows, tn), lambda l, n: (l, 0, n)),
        out_shape=jax.ShapeDtypeStruct((DEPTH, rows, 6 * D_MODEL), F32),
        compiler_params=_params("arbitrary", "arbitrary"),
        name="adaln_mod",
    )(c_pad, ada_w, ada_b.reshape(DEPTH, 1, 6 * D_MODEL))
    return out[:, :bsz]


def _vec_rows(rows, bsz):
    rows = [jnp.broadcast_to(r.astype(F32), (bsz, D_MODEL)) for r in rows]
    rows = rows + [jnp.zeros((bsz, D_MODEL), F32)] * (8 - len(rows))
    return jnp.stack(rows, axis=1)


def _rope(t, cosv, sin_lo, sin_hi):
    return (t * cosv + pltpu.roll(t, LANES - ROT_DIM // 2, axis=1) * sin_lo
            + pltpu.roll(t, ROT_DIM // 2, axis=1) * sin_hi)


def _store_folded(nat, refs):
    tm = nat.shape[1]
    for d, ref in zip(DILATIONS, refs):
        for r in range(d):
            rows = pl.ds(r, tm // d, stride=d) if d > 1 else slice(None)
            for t in range(A_WIDTH // LANES):
                col = r * A_WIDTH + t * LANES
                ref[:, col:col + LANES] = nat[t, rows, :].astype(BF16)


def _in0_body(x_ref, pos_ref, vec_ref, w_ref, tab_ref, pw_ref, ps_ref,
              q1_ref, q4_ref, q16_ref, k1_ref, k4_ref, k16_ref, v1_ref, v4_ref, v16_ref, ob_ref,
              ubuf, nat):
    i = pl.program_id(1)
    tm = x_ref.shape[0]

    @pl.when(i == 0)
    def _():
        ubuf[0:POOL_HALO, :] = jnp.zeros((POOL_HALO, POOL_WIDTH), F32)

    h = _norm_mod(x_ref[...], vec_ref[0:1, :], vec_ref[1:2, :], vec_ref[2:3, :]).astype(BF16)

    ang = pos_ref[...].astype(F32) * tab_ref[0:1, :]
    cosv = jnp.cos(ang)
    sinv = jnp.sin(ang)
    sin_lo = -sinv * tab_ref[1:2, :]
    sin_hi = sinv * tab_ref[2:3, :]

    pq = _dot(h, w_ref[:, 0:A_WIDTH])
    for t in range(A_WIDTH // LANES):
        cols = slice(t * LANES, (t + 1) * LANES)
        nat[t] = _rope(pq[:, cols], cosv, sin_lo, sin_hi) * (A_HEAD_DIM ** -0.5)
    _store_folded(nat, (q1_ref, q4_ref, q16_ref))
    pk = _dot(h, w_ref[:, A_WIDTH:2 * A_WIDTH])
    for t in range(A_WIDTH // LANES):
        cols = slice(t * LANES, (t + 1) * LANES)
        nat[t] = _rope(pk[:, cols], cosv, sin_lo, sin_hi)
    _store_folded(nat, (k1_ref, k4_ref, k16_ref))
    pv = _dot(h, w_ref[:, 2 * A_WIDTH:3 * A_WIDTH])
    for t in range(A_WIDTH // LANES):
        nat[t] = pv[:, t * LANES:(t + 1) * LANES]
    _store_folded(nat, (v1_ref, v4_ref, v16_ref))

    ubuf[POOL_HALO:POOL_HALO + tm, :] = _dot(h, w_ref[:, 3 * A_WIDTH:3 * A_WIDTH + POOL_WIDTH])
    tpos = i * tm + lax.broadcasted_iota(jnp.int32, (tm, 1), 0)
    for g, win in enumerate(POOL_WINDOWS):
        cols = slice(g * POOL_GROUP_DIM, (g + 1) * POOL_GROUP_DIM)
        cur = ubuf[POOL_HALO:POOL_HALO + tm, cols]
        acc = cur
        for j in range(1, win):
            acc = acc + ubuf[POOL_HALO - j:POOL_HALO - j + tm, cols]
        cnt = jnp.minimum(tpos + 1, win).astype(F32)
        pooled = acc / cnt - cur
        mixed = _dot(pooled.astype(BF16), pw_ref[g]) * ps_ref[0:1, cols]
        ob_ref[:, cols] = mixed.astype(BF16)
    ubuf[0:POOL_HALO, :] = ubuf[tm:tm + POOL_HALO, :]


def _rope_table():
    lane = np.arange(LANES) % A_HEAD_DIM
    half = ROT_DIM // 2
    inv_freq = ROPE_THETA ** (-jnp.arange(0, ROT_DIM, 2, dtype=F32) / ROT_DIM)
    freq = jnp.where(jnp.asarray(lane < ROT_DIM), inv_freq[jnp.asarray(lane % half)], 0.0)
    lo = jnp.asarray((lane < half).astype(np.float32))
    hi = jnp.asarray(((lane >= half) & (lane < ROT_DIM)).astype(np.float32))
    return jnp.stack([freq, lo, hi] + [jnp.zeros((LANES,), F32)] * 5)


def _in0(x, positions, vec, w_in, pool_w, pool_scale, tm=512):
    bsz, seq, _ = x.shape
    n_out = 3 * A_WIDTH + POOL_WIDTH
    tab = _rope_table()
    seq_spec = lambda width: pl.BlockSpec((None, tm, width), lambda b, i: (b, i, 0))
    full = lambda shape: pl.BlockSpec(shape, lambda b, i: (0,) * len(shape))
    folded_spec = lambda d: pl.BlockSpec((None, tm // d, d * A_WIDTH), lambda b, i: (b, i, 0))
    folded_sds = lambda d: jax.ShapeDtypeStruct((bsz, seq // d, d * A_WIDTH), BF16)
    outs = pl.pallas_call(
        _in0_body,
        grid=(bsz, seq // tm),
        in_specs=[seq_spec(D_MODEL), seq_spec(1),
                  pl.BlockSpec((None, 8, D_MODEL), lambda b, i: (b, 0, 0)),
                  full((D_MODEL, n_out)), full((8, LANES)),
                  full((len(POOL_WINDOWS), POOL_GROUP_DIM, POOL_GROUP_DIM)), full((1, POOL_WIDTH))],
        out_specs=[folded_spec(d) for d in DILATIONS] * 3 + [seq_spec(POOL_WIDTH)],
        out_shape=[folded_sds(d) for d in DILATIONS] * 3 + [folded_sds(1)],
        scratch_shapes=[pltpu.VMEM((tm + POOL_HALO, POOL_WIDTH), F32),
                        pltpu.VMEM((A_WIDTH // LANES, tm, LANES), F32)],
        compiler_params=_params("arbitrary", "arbitrary"),
        name="l0_in_proj",
    )(x, positions.reshape(bsz, seq, 1), vec, w_in.astype(BF16), tab,
      pool_w.astype(BF16), pool_scale.reshape(1, POOL_WIDTH))
    n = len(DILATIONS)
    return outs[0:n], outs[n:2 * n], outs[2 * n:3 * n], outs[3 * n]


def _attn_body(q_ref, kp_ref, kc_ref, vp_ref, vc_ref, o_ref, lse_ref):
    i = pl.program_id(2)
    n_bands = q_ref.shape[0] // BAND
    qi = lax.broadcasted_iota(jnp.int32, (BAND, BAND), 0)
    kj = lax.broadcasted_iota(jnp.int32, (BAND, BAND), 1)
    back = kj >= qi
    mask_cur = kj <= qi
    lane = lax.broadcasted_iota(jnp.int32, (BAND, LANES), 1)
    first = lane < A_HEAD_DIM
    pairs = range(A_WIDTH // LANES)
    cols = [slice(g * LANES, (g + 1) * LANES) for g in pairs]
    rows = [slice(s * BAND, (s + 1) * BAND) for s in range(n_bands)]
    probs = [(s, g, half) for s in range(n_bands) for g in pairs for half in range(2)]

    def prev_of(ref_prev, ref_cur, s, g):
        return ref_prev[:, cols[g]] if s == 0 else ref_cur[rows[s - 1], cols[g]]

    qh = []
    for s, g, half in probs:
        qp = q_ref[rows[s], cols[g]]
        qh.append(jnp.where(first if half == 0 else jnp.logical_not(first), qp, jnp.zeros_like(qp)))
    sp = [jnp.where(back & (i > 0) if s == 0 else back, _dot_nt(q, prev_of(kp_ref, kc_ref, s, g)), NEG_INF)
          for q, (s, g, _) in zip(qh, probs)]
    sc = [jnp.where(mask_cur, _dot_nt(q, kc_ref[rows[s], cols[g]]), NEG_INF)
          for q, (s, g, _) in zip(qh, probs)]
    m = [jnp.maximum(jnp.max(a, axis=-1, keepdims=True), jnp.max(b, axis=-1, keepdims=True))
         for a, b in zip(sp, sc)]
    pp = [jnp.exp(a - mm) for a, mm in zip(sp, m)]
    pc = [jnp.exp(a - mm) for a, mm in zip(sc, m)]
    den = [jnp.sum(a, axis=-1, keepdims=True) + jnp.sum(b, axis=-1, keepdims=True)
           for a, b in zip(pp, pc)]
    outs = [(_dot(a.astype(BF16), prev_of(vp_ref, vc_ref, s, g))
             + _dot(b.astype(BF16), vc_ref[rows[s], cols[g]])) / d
            for a, b, d, (s, g, _) in zip(pp, pc, den, probs)]
    per_band = 2 * len(pairs)
    for s in range(n_bands):
        lse_tile = jnp.zeros((BAND, LANES), F32)
        for idx in range(per_band):
            n = s * per_band + idx
            lse_tile = jnp.where(lane == idx, m[n] + jnp.log(den[n]), lse_tile)
        lse_ref[rows[s], :] = lse_tile
        for g in pairs:
            n = s * per_band + 2 * g
            o_ref[rows[s], cols[g]] = jnp.where(first, outs[n], outs[n + 1]).astype(BF16)


def _banded_attention(q, k, v, dilation, bands=2):
    bsz, n_sub, _ = q.shape
    rows = bands * BAND
    cur = pl.BlockSpec((None, rows, A_WIDTH), lambda b, r, i: (b, i, r))
    prev = pl.BlockSpec((None, BAND, A_WIDTH), lambda b, r, i: (b, jnp.maximum(i * bands - 1, 0), r))
    o, lse = pl.pallas_call(
        _attn_body,
        grid=(bsz, dilation, n_sub // rows),
        in_specs=[cur, prev, cur, prev, cur],
        out_specs=[cur, pl.BlockSpec((None, rows, LANES), lambda b, r, i: (b, i, r))],
        out_shape=[jax.ShapeDtypeStruct((bsz, n_sub, dilation * A_WIDTH), BF16),
                   jax.ShapeDtypeStruct((bsz, n_sub, dilation * LANES), F32)],
        compiler_params=_params("arbitrary", "arbitrary", "arbitrary"),
        name=f"dilated_attn_d{dilation}",
    )(q, k, k, v, v)
    return o, lse


def _out0_body(o1_ref, o4_ref, o16_ref, l1_ref, l4_ref, l16_ref, ob_ref, x_ref, vec_ref, w_ref,
               out_ref, o_nat, l_nat):
    tm = x_ref.shape[0]
    for slot, (d, o_ref, l_ref) in enumerate(zip(DILATIONS, (o1_ref, o4_ref, o16_ref),
                                                  (l1_ref, l4_ref, l16_ref))):
        for r in range(d):
            rows = pl.ds(r, tm // d, stride=d) if d > 1 else slice(None)
            for t in range(A_WIDTH // LANES):
                col = r * A_WIDTH + t * LANES
                o_nat[slot, t, rows, :] = o_ref[:, col:col + LANES].astype(F32)
            l_nat[slot, rows, :] = l_ref[:, r * LANES:(r + 1) * LANES]
    la, lb, lc = l_nat[0], l_nat[1], l_nat[2]
    m = jnp.maximum(jnp.maximum(la, lb), lc)
    ea, eb, ec = jnp.exp(la - m), jnp.exp(lb - m), jnp.exp(lc - m)
    tot = ea + eb + ec
    weights = (ea / tot, eb / tot, ec / tot)
    lane = lax.broadcasted_iota(jnp.int32, (tm, LANES), 1)
    first = lane < A_HEAD_DIM
    pieces = []
    for g in range(A_WIDTH // LANES):
        cols = slice(g * LANES, (g + 1) * LANES)
        acc = jnp.zeros((tm, LANES), F32)
        for slot, wt in enumerate(weights):
            w_pair = jnp.where(first,
                               jnp.broadcast_to(wt[:, 2 * g:2 * g + 1], (tm, LANES)),
                               jnp.broadcast_to(wt[:, 2 * g + 1:2 * g + 2], (tm, LANES)))
            acc = acc + w_pair * o_nat[slot, g]
        pieces.append(acc.astype(BF16))
    o_a = jnp.concatenate(pieces, axis=-1)
    y = _dot(o_a, w_ref[0:A_WIDTH, :]) + _dot(ob_ref[...], w_ref[A_WIDTH:A_WIDTH + POOL_WIDTH, :])
    out_ref[...] = _post_residual(x_ref[...], y, vec_ref[0:1, :], vec_ref[1:2, :])


def _out0(os, lses, ob, x, vec, w_out, tm=512):
    bsz, seq, _ = x.shape
    seq_spec = lambda width: pl.BlockSpec((None, tm, width), lambda b, i: (b, i, 0))
    folded = lambda width: [pl.BlockSpec((None, tm // d, d * width), lambda b, i: (b, i, 0))
                            for d in DILATIONS]
    n = len(DILATIONS)
    return pl.pallas_call(
        _out0_body,
        grid=(bsz, seq // tm),
        in_specs=folded(A_WIDTH) + folded(LANES) + [seq_spec(POOL_WIDTH),
                  seq_spec(D_MODEL), pl.BlockSpec((None, 8, D_MODEL), lambda b, i: (b, 0, 0)),
                  pl.BlockSpec((A_WIDTH + POOL_WIDTH, D_MODEL), lambda b, i: (0, 0))],
        out_specs=seq_spec(D_MODEL),
        out_shape=jax.ShapeDtypeStruct((bsz, seq, D_MODEL), F32),
        scratch_shapes=[pltpu.VMEM((n, A_WIDTH // LANES, tm, LANES), F32),
                        pltpu.VMEM((n, tm, LANES), F32)],
        compiler_params=_params("arbitrary", "arbitrary"),
        name="l0_out_proj",
    )(*os, *lses, ob, x, vec, w_out.astype(BF16))


def _ffn_body(x_ref, vec_ref, wg_ref, wu_ref, wd_ref, out_ref, hbuf, acc):
    f = pl.program_id(1)

    @pl.when(f == 0)
    def _():
        hbuf[...] = _norm_mod(x_ref[...], vec_ref[0:1, :], vec_ref[1:2, :],
                              vec_ref[2:3, :]).astype(BF16)
        acc[...] = jnp.zeros_like(acc)

    h = hbuf[...]
    act = _silu(_dot(h, wg_ref[...].astype(BF16))) * _dot(h, wu_ref[...].astype(BF16))
    acc[...] += _dot(act.astype(BF16), wd_ref[...].astype(BF16))

    @pl.when(f == pl.num_programs(1) - 1)
    def _():
        out_ref[...] = _post_residual(x_ref[...], acc[...], vec_ref[3:4, :], vec_ref[4:5, :])


def _ffn(x, vec, w_gate, w_up, w_down, tm=1024, tf=512):
    bsz, seq, _ = x.shape
    tiles_per_seq = seq // tm
    xt = x.reshape(bsz * seq, D_MODEL)
    row = pl.BlockSpec((tm, D_MODEL), lambda i, f: (i, 0))
    out = pl.pallas_call(
        _ffn_body,
        grid=(bsz * seq // tm, FFN_DIM // tf),
        in_specs=[row, pl.BlockSpec((None, 8, D_MODEL), lambda i, f: (i // tiles_per_seq, 0, 0)),
                  pl.BlockSpec((D_MODEL, tf), lambda i, f: (0, f)),
                  pl.BlockSpec((D_MODEL, tf), lambda i, f: (0, f)),
                  pl.BlockSpec((tf, D_MODEL), lambda i, f: (f, 0))],
        out_specs=row,
        out_shape=jax.ShapeDtypeStruct((bsz * seq, D_MODEL), F32),
        scratch_shapes=[pltpu.VMEM((tm, D_MODEL), BF16), pltpu.VMEM((tm, D_MODEL), F32)],
        compiler_params=_params("arbitrary", "arbitrary"),
        name="l0_swiglu",
    )(xt, vec, w_gate, w_up, w_down)
    return out.reshape(bsz, seq, D_MODEL)


def _split3(a):
    a1 = a.astype(BF16)
    r1 = a - a1.astype(F32)
    a2 = r1.astype(BF16)
    a3 = (r1 - a2.astype(F32)).astype(BF16)
    return a1, a2, a3


def _dot_split(a, b):
    a_hi = a.astype(BF16)
    a_lo = (a - a_hi.astype(F32)).astype(BF16)
    b_hi = b.astype(BF16)
    b_lo = (b - b_hi.astype(F32)).astype(BF16)
    return _dot(a_hi, b_hi) + (_dot(a_hi, b_lo) + _dot(a_lo, b_hi))


def _in1_body(x_ref, vec_ref, w_ref, wba_ref, cw_ref, hp_ref,
              q_ref, k_ref, v_ref, gate_ref, bg_ref, cbuf):
    i = pl.program_id(1)
    tm = x_ref.shape[0]

    @pl.when(i == 0)
    def _():
        cbuf[:, 0:CONV_HALO, :] = jnp.zeros((3, CONV_HALO, DN_WIDTH), F32)

    hf = _norm_mod(x_ref[...], vec_ref[0:1, :], vec_ref[1:2, :], vec_ref[2:3, :])
    h = hf.astype(BF16)

    for idx, dst in enumerate((q_ref, k_ref, v_ref)):
        cols = slice(idx * DN_WIDTH, (idx + 1) * DN_WIDTH)
        cbuf[idx, CONV_HALO:CONV_HALO + tm, :] = _dot(h, w_ref[:, cols])
        y = jnp.zeros((tm, DN_WIDTH), F32)
        for j in range(CONV_WIDTH):
            off = CONV_HALO - (CONV_WIDTH - 1) + j
            y = y + cw_ref[j:j + 1, cols] * cbuf[idx, off:off + tm, :]
        y = _silu(y)
        cbuf[idx, 0:CONV_HALO, :] = cbuf[idx, tm:tm + CONV_HALO, :]
        if idx < 2:
            scale = DN_HEAD_DIM ** -0.5 if idx == 0 else 1.0
            for hd in range(DN_HEADS):
                hc = slice(hd * DN_HEAD_DIM, (hd + 1) * DN_HEAD_DIM)
                blk = y[:, hc]
                ss = jnp.sum(blk * blk, axis=-1, keepdims=True)
                dst[:, hc] = (blk * (lax.rsqrt(ss + EPS) * scale)).astype(BF16)
        else:
            dst[...] = y.astype(BF16)

    gate_ref[...] = _silu(_dot(h, w_ref[:, 3 * DN_WIDTH:4 * DN_WIDTH])).astype(BF16)

    ba = _dot_split(hf, wba_ref[...])
    beta = jax.nn.sigmoid(ba)
    z = ba + hp_ref[1:2, :]
    softplus = jnp.maximum(z, 0.0) + jnp.log1p(jnp.exp(-jnp.abs(z)))
    g = -jnp.exp(hp_ref[0:1, :]) * softplus
    r = lax.broadcasted_iota(jnp.int32, (tm, tm), 0)
    c = lax.broadcasted_iota(jnp.int32, (tm, tm), 1)
    tri = jnp.where((r // CHUNK == c // CHUNK) & (c <= r), 1.0, 0.0).astype(BF16)
    g1, g2, g3 = _split3(g)
    gc = _dot(tri, g1) + _dot(tri, g2) + _dot(tri, g3)
    lane = lax.broadcasted_iota(jnp.int32, (tm, LANES), 1)
    bg_ref[...] = jnp.where(lane < DN_HEADS, beta, gc)


def _in1(x, vec, w_in, conv_w, a_log, dt_bias, tm=256):
    bsz, seq, _ = x.shape
    w_main = w_in[:, :4 * DN_WIDTH].astype(BF16)
    w_ba = jnp.zeros((D_MODEL, LANES), F32).at[:, :2 * DN_HEADS].set(w_in[:, 4 * DN_WIDTH:])
    hp = jnp.zeros((8, LANES), F32)
    hp = hp.at[0, DN_HEADS:2 * DN_HEADS].set(a_log).at[1, DN_HEADS:2 * DN_HEADS].set(dt_bias)
    seq_spec = lambda width: pl.BlockSpec((None, tm, width), lambda b, i: (b, i, 0))
    full = lambda shape: pl.BlockSpec(shape, lambda b, i: (0,) * len(shape))
    wide = jax.ShapeDtypeStruct((bsz, seq, DN_WIDTH), BF16)
    return pl.pallas_call(
        _in1_body,
        grid=(bsz, seq // tm),
        in_specs=[seq_spec(D_MODEL), pl.BlockSpec((None, 8, D_MODEL), lambda b, i: (b, 0, 0)),
                  full((D_MODEL, 4 * DN_WIDTH)), full((D_MODEL, LANES)),
                  full((CONV_WIDTH, 3 * DN_WIDTH)), full((8, LANES))],
        out_specs=[seq_spec(DN_WIDTH)] * 4 + [seq_spec(LANES)],
        out_shape=[wide] * 4 + [jax.ShapeDtypeStruct((bsz, seq, LANES), F32)],
        scratch_shapes=[pltpu.VMEM((3, tm + CONV_HALO, DN_WIDTH), F32)],
        compiler_params=_params("arbitrary", "arbitrary"),
        name="l1_in_proj",
    )(x, vec, w_main, w_ba, conv_w, hp)


def _unit_lower_inverses(lmats):
    n = lmats[0].shape[0]
    r = lax.broadcasted_iota(jnp.int32, (n, n), 0)
    c = lax.broadcasted_iota(jnp.int32, (n, n), 1)

    def below(size):
        return (r // (2 * size) == c // (2 * size)) & (r % (2 * size) >= size) & (c % (2 * size) < size)

    eye = jnp.where(r == c, 1.0, 0.0).astype(F32)
    first = below(1)
    invs = [eye - jnp.where(first, lm, 0.0) for lm in lmats]
    size = 2
    while size < n:
        mask = below(size)
        offs = [jnp.where(mask, lm, 0.0).astype(BF16) for lm in lmats]
        inv16 = [x.astype(BF16) for x in invs]
        xc = [_dot(x, o).astype(BF16) for x, o in zip(inv16, offs)]
        invs = [x - _dot(t, x16) for x, t, x16 in zip(invs, xc, inv16)]
        size *= 2
    return invs


def _delta_body(q_ref, k_ref, v_ref, gate_ref, bg_ref, gcr_ref, ng_ref, o_ref,
                state, p_s, n_s, qp_s, op_s, dec_s):
    i = pl.program_id(1)
    n_chunks = q_ref.shape[0] // CHUNK
    heads = range(DN_HEADS)
    head_cols = [slice(hd * DN_HEAD_DIM, (hd + 1) * DN_HEAD_DIM) for hd in heads]

    @pl.when(i == 0)
    def _():
        state[...] = jnp.zeros_like(state)

    r = lax.broadcasted_iota(jnp.int32, (CHUNK, CHUNK), 0)
    c = lax.broadcasted_iota(jnp.int32, (CHUNK, CHUNK), 1)
    causal = r >= c
    strict = r > c

    def prepare(step, carry):
        chunks = [step * PREP_CHUNKS + n for n in range(PREP_CHUNKS)]
        rows = [pl.ds(pl.multiple_of(ci * CHUNK, CHUNK), CHUNK) for ci in chunks]
        bg = [bg_ref[r, :] for r in rows]
        gcr_all = [gcr_ref[ci] for ci in chunks]
        probs = [(n, hd) for n in range(PREP_CHUNKS) for hd in heads]
        k16 = [k_ref[rows[n], head_cols[hd]] for n, hd in probs]
        q16 = [q_ref[rows[n], head_cols[hd]] for n, hd in probs]
        kf = [x.astype(F32) for x in k16]
        beta = [bg[n][:, hd:hd + 1] for n, hd in probs]
        gcc = [bg[n][:, DN_HEADS + hd:DN_HEADS + hd + 1] for n, hd in probs]
        gcr = [gcr_all[n][hd:hd + 1, :] for n, hd in probs]
        kb = [x * b for x, b in zip(kf, beta)]
        both = [_dot_nt(jnp.concatenate([a.astype(BF16), b], axis=0), x)
                for a, b, x in zip(kb, q16, k16)]
        decay = [jnp.exp(jnp.where(causal, a - b, NEG_INF)) for a, b in zip(gcc, gcr)]
        lmats = [jnp.where(strict, m[0:CHUNK] * d, 0.0) for m, d in zip(both, decay)]
        attn = [(m[CHUNK:2 * CHUNK] * d).astype(BF16) for m, d in zip(both, decay)]
        invs = _unit_lower_inverses(lmats)
        eg = [jnp.exp(x) for x in gcc]
        rhs = [jnp.concatenate([(v_ref[rows[n], head_cols[hd]].astype(F32) * b).astype(BF16),
                                (a * e).astype(BF16)], axis=1)
               for (n, hd), b, a, e in zip(probs, beta, kb, eg)]
        sol = [_dot(x.astype(BF16), y).astype(BF16) for x, y in zip(invs, rhs)]
        au = [_dot(a, s) for a, s in zip(attn, sol)]
        g_last = [x[CHUNK - 1:CHUNK, :] for x in gcc]
        kg = [(x * jnp.exp(gl - g)).astype(BF16) for x, gl, g in zip(kf, g_last, gcc)]
        kn = [_dot_tn(a, s) for a, s in zip(kg, sol)]
        for idx, (n, hd) in enumerate(probs):
            ci = chunks[n]
            n_s[ci, hd] = kn[idx][:, 0:DN_HEAD_DIM]
            p_s[ci, hd] = kn[idx][:, DN_HEAD_DIM:2 * DN_HEAD_DIM].astype(BF16)
            op_s[ci, hd] = au[idx][:, 0:DN_HEAD_DIM]
            qp_s[ci, hd] = (q16[idx].astype(F32) * eg[idx]
                            - au[idx][:, DN_HEAD_DIM:2 * DN_HEAD_DIM]).astype(BF16)
            dec_s[ci, hd] = jnp.broadcast_to(jnp.exp(g_last[idx]), (8, DN_HEAD_DIM))
        return carry

    def scan(ci, carry):
        rows = pl.ds(pl.multiple_of(ci * CHUNK, CHUNK), CHUNK)
        for hd, hc in zip(heads, head_cols):
            s = state[hd]
            s16 = s.astype(BF16)
            o = _dot(qp_s[ci, hd], s16) + op_s[ci, hd]
            state[hd] = s * dec_s[ci, hd][0:1, :] + n_s[ci, hd] - _dot(p_s[ci, hd], s16)
            ms = jnp.mean(o * o, axis=-1, keepdims=True)
            o = o * lax.rsqrt(ms + EPS) * ng_ref[0:1, :] * gate_ref[rows, hc].astype(F32)
            o_ref[rows, hc] = o.astype(BF16)
        return carry

    lax.fori_loop(0, n_chunks // PREP_CHUNKS, prepare, 0)
    lax.fori_loop(0, n_chunks, scan, 0, unroll=4)


def _delta(q, k, v, gate, bg, norm_g, block=512):
    bsz, seq, _ = q.shape
    n = seq // CHUNK
    nc = block // CHUNK
    gc_rows = bg[:, :, DN_HEADS:2 * DN_HEADS].reshape(bsz, n, CHUNK, DN_HEADS).transpose(0, 1, 3, 2)
    seq_spec = lambda width: pl.BlockSpec((None, block, width), lambda b, i: (b, i, 0))
    per_head = lambda rows, dtype: pltpu.VMEM((nc, DN_HEADS, rows, DN_HEAD_DIM), dtype)
    return pl.pallas_call(
        _delta_body,
        grid=(bsz, seq // block),
        in_specs=[seq_spec(DN_WIDTH)] * 4 + [seq_spec(LANES),
                  pl.BlockSpec((None, nc, DN_HEADS, CHUNK), lambda b, i: (b, i, 0, 0)),
                  pl.BlockSpec((1, DN_HEAD_DIM), lambda b, i: (0, 0))],
        out_specs=seq_spec(DN_WIDTH),
        out_shape=jax.ShapeDtypeStruct((bsz, seq, DN_WIDTH), BF16),
        scratch_shapes=[pltpu.VMEM((DN_HEADS, DN_HEAD_DIM, DN_HEAD_DIM), F32),
                        per_head(DN_HEAD_DIM, BF16), per_head(DN_HEAD_DIM, F32),
                        per_head(CHUNK, BF16), per_head(CHUNK, F32), per_head(8, F32)],
        compiler_params=_params("arbitrary", "arbitrary"),
        name="gated_delta",
    )(q, k, v, gate, bg, gc_rows, norm_g.reshape(1, DN_HEAD_DIM))


def _route(hf, rw):
    tm = hf.shape[0]
    lane = lax.broadcasted_iota(jnp.int32, (tm, LANES), 1)
    logits = jnp.where(lane < N_EXPERTS, _dot_split(hf, rw), NEG_INF)
    m1 = jnp.max(logits, axis=-1, keepdims=True)
    i1 = jnp.min(jnp.where(logits == m1, lane, LANES), axis=-1, keepdims=True)
    rest = jnp.where(lane == i1, NEG_INF, logits)
    m2 = jnp.max(rest, axis=-1, keepdims=True)
    i2 = jnp.min(jnp.where(rest == m2, lane, LANES), axis=-1, keepdims=True)
    e2 = jnp.exp(m2 - m1)
    w1 = 1.0 / (1.0 + e2)
    w2 = e2 / (1.0 + e2)
    combine = jnp.where(lane == i1, w1, 0.0) + jnp.where(lane == i2, w2, 0.0)
    chosen = jnp.where((lane == i1) | (lane == i2), 1.0, 0.0)
    return combine, chosen


def _out1_body(a_ref, x_ref, vec_ref, w_ref, rw_ref,
               x_out, h_out, cw_out, rank_out, cnt_out, running, *, tiles_per_block):
    i = pl.program_id(1)
    tm = x_ref.shape[0]

    @pl.when(i % tiles_per_block == 0)
    def _():
        running[...] = jnp.zeros_like(running)

    y = _dot(a_ref[...], w_ref[...])
    x2 = _post_residual(x_ref[...], y, vec_ref[0:1, :], vec_ref[1:2, :])
    x_out[...] = x2
    hf = _norm_mod(x2, vec_ref[2:3, :], vec_ref[3:4, :], vec_ref[4:5, :])
    h_out[...] = hf.astype(BF16)
    combine, chosen = _route(hf, rw_ref[...])
    cw_out[...] = combine
    r = lax.broadcasted_iota(jnp.int32, (tm, tm), 0)
    c = lax.broadcasted_iota(jnp.int32, (tm, tm), 1)
    before = jnp.where(c < r, 1.0, 0.0).astype(BF16)
    rank = _dot(before, chosen.astype(BF16)) + running[0:1, :]
    rank_out[...] = jnp.where(chosen > 0.0, rank, -1.0)
    running[...] = running[...] + jnp.sum(chosen, axis=0, keepdims=True)
    cnt_out[...] = running[...]


def _out1(a, x, vec, w_out, router_w, tm=512, route_block=ROUTE_BLOCK):
    bsz, seq, _ = x.shape
    tiles_per_block = route_block // tm
    blocks_per_seq = seq // route_block
    rw = jnp.zeros((D_MODEL, LANES), F32).at[:, :N_EXPERTS].set(router_w)
    seq_spec = lambda width: pl.BlockSpec((None, tm, width), lambda b, i: (b, i, 0))
    return pl.pallas_call(
        functools.partial(_out1_body, tiles_per_block=tiles_per_block),
        grid=(bsz, seq // tm),
        in_specs=[seq_spec(DN_WIDTH), seq_spec(D_MODEL),
                  pl.BlockSpec((None, 8, D_MODEL), lambda b, i: (b, 0, 0)),
                  pl.BlockSpec((DN_WIDTH, D_MODEL), lambda b, i: (0, 0)),
                  pl.BlockSpec((D_MODEL, LANES), lambda b, i: (0, 0))],
        out_specs=[seq_spec(D_MODEL), seq_spec(D_MODEL), seq_spec(LANES), seq_spec(LANES),
                   pl.BlockSpec((None, 8, LANES),
                                lambda b, i: (b * blocks_per_seq + i // tiles_per_block, 0, 0))],
        out_shape=[jax.ShapeDtypeStruct((bsz, seq, D_MODEL), F32),
                   jax.ShapeDtypeStruct((bsz, seq, D_MODEL), BF16),
                   jax.ShapeDtypeStruct((bsz, seq, LANES), F32),
                   jax.ShapeDtypeStruct((bsz, seq, LANES), F32),
                   jax.ShapeDtypeStruct((bsz * blocks_per_seq, 8, LANES), F32)],
        scratch_shapes=[pltpu.VMEM((8, LANES), F32)],
        compiler_params=_params("arbitrary", "arbitrary"),
        name="l1_out_proj_route",
    )(a, x, vec, w_out.astype(BF16), rw)


def _moe_body(cnt_ref, h_ref, rrow_ref, wrow_ref, rcol_ref, x_ref, vec_ref, wg_ref, wu_ref, wd_ref,
              out_ref, xs, ys, wslot):
    b, e, f = pl.program_id(0), pl.program_id(1), pl.program_id(2)
    tb = h_ref.shape[0]

    @pl.when((e == 0) & (f == 0))
    def _():
        out_ref[...] = jnp.zeros_like(out_ref)

    count = cnt_ref[b * N_EXPERTS + e]
    padded = (count + SEGMENTS[-1] - 1) // SEGMENTS[-1] * SEGMENTS[-1]
    n_large = padded // SEGMENTS[0]

    def for_each_segment(fn):
        def large(t, carry):
            fn(pl.multiple_of(t * SEGMENTS[0], SEGMENTS[0]), SEGMENTS[0])
            return carry

        lax.fori_loop(0, n_large, large, 0)
        done = n_large * SEGMENTS[0]
        for size in SEGMENTS[1:]:
            take = ((padded - done) // size) > 0

            @pl.when(take)
            def _(done=done, size=size):
                fn(pl.multiple_of(done, SEGMENTS[-1]), size)

            done = done + jnp.where(take, size, 0)

    @pl.when(f == 0)
    def _():
        rrow = rrow_ref[pl.ds(e, 1), :]
        wrow = wrow_ref[pl.ds(e, 1), :]

        def pack(base, size):
            rows = pl.ds(base, size)
            slot = (base + lax.broadcasted_iota(jnp.int32, (size, 1), 0)).astype(F32)
            hit = rrow == slot
            xs[rows, :] = _dot(jnp.where(hit, 1.0, 0.0).astype(BF16), h_ref[...]).astype(BF16)
            wslot[rows, :] = jnp.sum(jnp.where(hit, wrow, 0.0), axis=-1, keepdims=True)
            ys[rows, :] = jnp.zeros((size, D_MODEL), F32)

        for_each_segment(pack)

    def expert(base, size):
        rows = pl.ds(base, size)
        x = xs[rows, :]
        act = _silu(_dot(x, wg_ref[...].astype(BF16))) * _dot(x, wu_ref[...].astype(BF16))
        ys[rows, :] += _dot(act.astype(BF16), wd_ref[...].astype(BF16))

    for_each_segment(expert)

    @pl.when(f == pl.num_programs(2) - 1)
    def _():
        lane = lax.broadcasted_iota(jnp.int32, (tb, LANES), 1)
        rcol = jnp.sum(jnp.where(lane == e, rcol_ref[...], 0.0), axis=-1, keepdims=True)

        def unpack(base, size):
            rows = pl.ds(base, size)
            y = (ys[rows, :] * wslot[rows, :]).astype(BF16)
            slot = (base + lax.broadcasted_iota(jnp.int32, (1, size), 1)).astype(F32)
            back = jnp.where(rcol == slot, 1.0, 0.0).astype(BF16)
            out_ref[...] += _dot(back, y)

        for_each_segment(unpack)

        @pl.when(e == pl.num_programs(1) - 1)
        def _():
            out_ref[...] = _post_residual(x_ref[...], out_ref[...], vec_ref[0:1, :], vec_ref[1:2, :])


def _moe(x, vec, h, combine, rank, counts, w_gate, w_up, w_down, route_block=ROUTE_BLOCK, tf=512):
    bsz, seq, _ = h.shape
    tokens = bsz * seq
    n_blocks = tokens // route_block
    per_block_rows = lambda t: (t.reshape(n_blocks, route_block, LANES)[:, :, :N_EXPERTS]
                                .transpose(0, 2, 1))
    cnt = counts[:, 0, :N_EXPERTS].astype(jnp.int32).reshape(n_blocks * N_EXPERTS)
    assert route_block % SEGMENTS[-1] == 0
    blocks_per_seq = seq // route_block
    once = pl.Buffered(1)
    per_block = lambda width: pl.BlockSpec((route_block, width), lambda b, e, f, cnt: (b, 0),
                                           pipeline_mode=once)
    per_expert = pl.BlockSpec((None, N_EXPERTS, route_block), lambda b, e, f, cnt: (b, 0, 0),
                              pipeline_mode=once)
    grid_spec = pltpu.PrefetchScalarGridSpec(
        num_scalar_prefetch=1,
        grid=(n_blocks, N_EXPERTS, FFN_DIM // tf),
        in_specs=[per_block(D_MODEL), per_expert, per_expert, per_block(LANES), per_block(D_MODEL),
                  pl.BlockSpec((None, 8, D_MODEL), lambda b, e, f, cnt: (b // blocks_per_seq, 0, 0)),
                  pl.BlockSpec((None, D_MODEL, tf), lambda b, e, f, cnt: (e, 0, f)),
                  pl.BlockSpec((None, D_MODEL, tf), lambda b, e, f, cnt: (e, 0, f)),
                  pl.BlockSpec((None, tf, D_MODEL), lambda b, e, f, cnt: (e, f, 0))],
        out_specs=per_block(D_MODEL),
        scratch_shapes=[pltpu.VMEM((route_block, D_MODEL), BF16), pltpu.VMEM((route_block, D_MODEL), F32),
                        pltpu.VMEM((route_block, 1), F32)])
    out = pl.pallas_call(
        _moe_body,
        grid_spec=grid_spec,
        out_shape=jax.ShapeDtypeStruct((tokens, D_MODEL), F32),
        compiler_params=_params("arbitrary", "arbitrary", "arbitrary"),
        name="l1_moe",
    )(cnt, h.reshape(tokens, D_MODEL), per_block_rows(rank), per_block_rows(combine),
      rank.reshape(tokens, LANES), x.reshape(tokens, D_MODEL), vec, w_gate, w_up, w_down)
    return out.reshape(bsz, seq, D_MODEL)


def kernel(x, c, positions, ada_w, ada_b, mix_pre_g, mix_post_g, ffn_pre_g, ffn_post_g, even_w_in, even_pool_w, even_pool_scale, even_w_out, even_ffn_w_gate, even_ffn_w_up, even_ffn_w_down, odd_w_in, odd_conv_w, odd_a_log, odd_dt_bias, odd_norm_g, odd_w_out, odd_router_w, odd_moe_w_gate, odd_moe_w_up, odd_moe_w_down):
    bsz = x.shape[0]
    mod = _adaln_mod(c, ada_w, ada_b)
    sh1, sc1, gt1, sh2, sc2, gt2 = (mod[:, :, n * D_MODEL:(n + 1) * D_MODEL] for n in range(6))

    vec = _vec_rows([mix_pre_g[0], sc1[0], sh1[0]], bsz)
    qs, ks, vs, o_b = _in0(x, positions, vec, even_w_in[0], even_pool_w[0], even_pool_scale[0])
    branches = [_banded_attention(q, k, v, d) for q, k, v, d in zip(qs, ks, vs, DILATIONS)]
    vec = _vec_rows([mix_post_g[0], gt1[0]], bsz)
    x = _out0([o for o, _ in branches], [l for _, l in branches], o_b, x, vec, even_w_out[0])
    vec = _vec_rows([ffn_pre_g[0], sc2[0], sh2[0], ffn_post_g[0], gt2[0]], bsz)
    x = _ffn(x, vec, even_ffn_w_gate[0], even_ffn_w_up[0], even_ffn_w_down[0])

    vec = _vec_rows([mix_pre_g[1], sc1[1], sh1[1]], bsz)
    q, k, v, gate, bg = _in1(x, vec, odd_w_in[0], odd_conv_w[0], odd_a_log[0], odd_dt_bias[0])
    o = _delta(q, k, v, gate, bg, odd_norm_g[0])
    vec = _vec_rows([mix_post_g[1], gt1[1], ffn_pre_g[1], sc2[1], sh2[1]], bsz)
    x, h, combine, rank, counts = _out1(o, x, vec, odd_w_out[0], odd_router_w[0])
    vec = _vec_rows([ffn_post_g[1], gt2[1]], bsz)
    return _moe(x, vec, h, combine, rank, counts,
                odd_moe_w_gate[0], odd_moe_w_up[0], odd_moe_w_down[0])
```

```python
import functools

import numpy as np
import jax
import jax.numpy as jnp
from jax import lax
from jax.experimental import pallas as pl
from jax.experimental.pallas import tpu as pltpu

F32 = jnp.float32
BF16 = jnp.bfloat16
HIGHEST = lax.Precision.HIGHEST

D_MODEL = 1024
DEPTH = 2
A_HEADS = 8
A_HEAD_DIM = 64
A_WIDTH = 512
DILATIONS = (1, 4, 16)
BAND = 128
ROT_DIM = 16
ROPE_THETA = 500000.0
POOL_WINDOWS = (2, 4, 8, 16)
POOL_GROUP_DIM = 128
POOL_WIDTH = 512
POOL_HALO = 16
DN_HEADS = 8
DN_HEAD_DIM = 128
DN_WIDTH = 1024
CONV_WIDTH = 4
CONV_HALO = 8
CHUNK = 64
PREP_CHUNKS = 4
FFN_DIM = 3584
N_EXPERTS = 8
EPS = 1e-6
LANES = 128
ROUTE_BLOCK = 2048
SEGMENTS = (512, 256, 128)
ROUTE_ROWS = 256
NEG_INF = float("-inf")

VMEM_LIMIT = 56 * 1024 * 1024


def _params(*sem):
    return pltpu.CompilerParams(dimension_semantics=sem, vmem_limit_bytes=VMEM_LIMIT)


def _dot(a, b, precision=None):
    return jnp.dot(a, b, preferred_element_type=F32, precision=precision)


def _dot_nt(a, b, precision=None):
    return lax.dot_general(a, b, (((1,), (1,)), ((), ())), preferred_element_type=F32,
                           precision=precision)


def _dot_tn(a, b, precision=None):
    return lax.dot_general(a, b, (((0,), (0,)), ((), ())), preferred_element_type=F32,
                           precision=precision)


def _silu(x):
    return x * jax.nn.sigmoid(x)


def _norm_mod(x, g, sc, sh):
    ms = jnp.mean(x * x, axis=-1, keepdims=True)
    return x * lax.rsqrt(ms + EPS) * g * (1.0 + sc) + sh


def _post_residual(x, y, g, gt):
    ms = jnp.mean(y * y, axis=-1, keepdims=True)
    return x + gt * (y * lax.rsqrt(ms + EPS) * g)


def _mod_body(c_ref, w_ref, b_ref, o_ref):
    c = c_ref[...]
    o_ref[...] = _dot(_silu(c), w_ref[...], precision=HIGHEST) + b_ref[...]


def _adaln_mod(c, ada_w, ada_b):
    bsz = c.shape[0]
    rows = 8
    c_pad = jnp.zeros((rows, D_MODEL), F32).at[:bsz].set(c)
    tn = 1536
    out = pl.pallas_call(
        _mod_body,
        grid=(DEPTH, 6 * D_MODEL // tn),
        in_specs=[pl.BlockSpec((rows, D_MODEL), lambda l, n: (0, 0)),
                  pl.BlockSpec((None, D_MODEL, tn), lambda l, n: (l, 0, n)),
                  pl.BlockSpec((None, 1, tn), lambda l, n: (l, 0, n))],
        out_specs=pl.BlockSpec((None, rows, tn), lambda l, n: (l, 0, n)),
        out_shape=jax.ShapeDtypeStruct((DEPTH, rows, 6 * D_MODEL), F32),
        compiler_params=_params("arbitrary", "arbitrary"),
        name="adaln_mod",
    )(c_pad, ada_w, ada_b.reshape(DEPTH, 1, 6 * D_MODEL))
    return out[:, :bsz]


def _vec_rows(rows, bsz):
    rows = [jnp.broadcast_to(r.astype(F32), (bsz, D_MODEL)) for r in rows]
    rows = rows + [jnp.zeros((bsz, D_MODEL), F32)] * (8 - len(rows))
    return jnp.stack(rows, axis=1)


def _rope(t, cosv, sin_lo, sin_hi):
    return (t * cosv + pltpu.roll(t, LANES - ROT_DIM // 2, axis=1) * sin_lo
            + pltpu.roll(t, ROT_DIM // 2, axis=1) * sin_hi)


def _store_folded(nat, refs):
    tm = nat.shape[1]
    for d, ref in zip(DILATIONS, refs):
        for r in range(d):
            rows = pl.ds(r, tm // d, stride=d) if d > 1 else slice(None)
            for t in range(A_WIDTH // LANES):
                col = r * A_WIDTH + t * LANES
                ref[:, col:col + LANES] = nat[t, rows, :].astype(BF16)


def _in0_body(x_ref, pos_ref, vec_ref, w_ref, freq_ref, exp_ref, plain_ref, pw_ref, ps_ref,
              q1_ref, q4_ref, q16_ref, k1_ref, k4_ref, k16_ref, v1_ref, v4_ref, v16_ref, ob_ref,
              ubuf, nat):
    i = pl.program_id(1)
    tm = x_ref.shape[0]

    @pl.when(i == 0)
    def _():
        ubuf[0:POOL_HALO, :] = jnp.zeros((POOL_HALO, POOL_WIDTH), F32)

    h = _norm_mod(x_ref[...], vec_ref[0:1, :], vec_ref[1:2, :], vec_ref[2:3, :]).astype(BF16)

    ang = freq_ref[...] * pos_ref[...].astype(F32)

    def spread(v, e):
        lead = v.astype(BF16).astype(F32)
        return _dot_tn(lead, e) + _dot_tn(v - lead, e)

    cos_t = jnp.cos(ang)
    sin_t = jnp.sin(ang)
    cosv = spread(cos_t, exp_ref[0]) + plain_ref[...]
    sin_lo = -spread(sin_t, exp_ref[1])
    sin_hi = spread(sin_t, exp_ref[2])

    pq = _dot(h, w_ref[:, 0:A_WIDTH])
    for t in range(A_WIDTH // LANES):
        cols = slice(t * LANES, (t + 1) * LANES)
        nat[t] = _rope(pq[:, cols], cosv, sin_lo, sin_hi) * (A_HEAD_DIM ** -0.5)
    _store_folded(nat, (q1_ref, q4_ref, q16_ref))
    pk = _dot(h, w_ref[:, A_WIDTH:2 * A_WIDTH])
    for t in range(A_WIDTH // LANES):
        cols = slice(t * LANES, (t + 1) * LANES)
        nat[t] = _rope(pk[:, cols], cosv, sin_lo, sin_hi)
    _store_folded(nat, (k1_ref, k4_ref, k16_ref))
    pv = _dot(h, w_ref[:, 2 * A_WIDTH:3 * A_WIDTH])
    for t in range(A_WIDTH // LANES):
        nat[t] = pv[:, t * LANES:(t + 1) * LANES]
    _store_folded(nat, (v1_ref, v4_ref, v16_ref))

    ubuf[POOL_HALO:POOL_HALO + tm, :] = _dot(h, w_ref[:, 3 * A_WIDTH:3 * A_WIDTH + POOL_WIDTH])
    tpos = i * tm + lax.broadcasted_iota(jnp.int32, (tm, 1), 0)
    for g, win in enumerate(POOL_WINDOWS):
        cols = slice(g * POOL_GROUP_DIM, (g + 1) * POOL_GROUP_DIM)
        cur = ubuf[POOL_HALO:POOL_HALO + tm, cols]
        acc = cur
        for j in range(1, win):
            acc = acc + ubuf[POOL_HALO - j:POOL_HALO - j + tm, cols]
        cnt = jnp.minimum(tpos + 1, win).astype(F32)
        pooled = acc / cnt - cur
        mixed = _dot(pooled.astype(BF16), pw_ref[g]) * ps_ref[0:1, cols]
        ob_ref[:, cols] = mixed.astype(BF16)
    ubuf[0:POOL_HALO, :] = ubuf[tm:tm + POOL_HALO, :]


def _rope_tables(tm):
    half = ROT_DIM // 2
    inv_freq = ROPE_THETA ** (-jnp.arange(0, ROT_DIM, 2, dtype=F32) / ROT_DIM)
    freq = jnp.broadcast_to(inv_freq[:, None], (half, tm))
    lane = np.arange(LANES) % A_HEAD_DIM
    first, second = lane < half, (lane >= half) & (lane < ROT_DIM)
    uses = lane[None, :] % half == np.arange(half)[:, None]
    expand = np.stack([uses & (first | second), uses & first, uses & second]).astype(np.float32)
    plain = (lane >= ROT_DIM).astype(np.float32).reshape(1, LANES)
    return freq, jnp.asarray(expand), jnp.asarray(plain)


def _in0(x, positions, vec, w_in, pool_w, pool_scale, tm=512):
    bsz, seq, _ = x.shape
    n_out = 3 * A_WIDTH + POOL_WIDTH
    freq, expand, plain = _rope_tables(tm)
    seq_spec = lambda width: pl.BlockSpec((None, tm, width), lambda b, i: (b, i, 0))
    full = lambda shape: pl.BlockSpec(shape, lambda b, i: (0,) * len(shape))
    folded_spec = lambda d: pl.BlockSpec((None, tm // d, d * A_WIDTH), lambda b, i: (b, i, 0))
    folded_sds = lambda d: jax.ShapeDtypeStruct((bsz, seq // d, d * A_WIDTH), BF16)
    outs = pl.pallas_call(
        _in0_body,
        grid=(bsz, seq // tm),
        in_specs=[seq_spec(D_MODEL), pl.BlockSpec((None, None, 1, tm), lambda b, i: (b, i, 0, 0)),
                  pl.BlockSpec((None, 8, D_MODEL), lambda b, i: (b, 0, 0)),
                  full((D_MODEL, n_out)), full(freq.shape), full(expand.shape), full(plain.shape),
                  full((len(POOL_WINDOWS), POOL_GROUP_DIM, POOL_GROUP_DIM)), full((1, POOL_WIDTH))],
        out_specs=[folded_spec(d) for d in DILATIONS] * 3 + [seq_spec(POOL_WIDTH)],
        out_shape=[folded_sds(d) for d in DILATIONS] * 3 + [folded_sds(1)],
        scratch_shapes=[pltpu.VMEM((tm + POOL_HALO, POOL_WIDTH), F32),
                        pltpu.VMEM((A_WIDTH // LANES, tm, LANES), F32)],
        compiler_params=_params("arbitrary", "arbitrary"),
        name="l0_in_proj",
    )(x, positions.reshape(bsz, seq // tm, 1, tm), vec, w_in.astype(BF16), freq, expand, plain,
      pool_w.astype(BF16), pool_scale.reshape(1, POOL_WIDTH))
    n = len(DILATIONS)
    return outs[0:n], outs[n:2 * n], outs[2 * n:3 * n], outs[3 * n]


def _attn_body(q_ref, kp_ref, kc_ref, vp_ref, vc_ref, o_ref, lse_ref):
    i = pl.program_id(2)
    n_bands = q_ref.shape[0] // BAND
    qi = lax.broadcasted_iota(jnp.int32, (BAND, BAND), 0)
    kj = lax.broadcasted_iota(jnp.int32, (BAND, BAND), 1)
    back = kj >= qi
    mask_cur = kj <= qi
    lane = lax.broadcasted_iota(jnp.int32, (BAND, LANES), 1)
    first = lane < A_HEAD_DIM
    pairs = range(A_WIDTH // LANES)
    cols = [slice(g * LANES, (g + 1) * LANES) for g in pairs]
    rows = [slice(s * BAND, (s + 1) * BAND) for s in range(n_bands)]
    probs = [(s, g, half) for s in range(n_bands) for g in pairs for half in range(2)]

    def prev_of(ref_prev, ref_cur, s, g):
        return ref_prev[:, cols[g]] if s == 0 else ref_cur[rows[s - 1], cols[g]]

    qh = []
    for s, g, half in probs:
        qp = q_ref[rows[s], cols[g]]
        qh.append(jnp.where(first if half == 0 else jnp.logical_not(first), qp, jnp.zeros_like(qp)))
    sp = [jnp.where(back & (i > 0) if s == 0 else back, _dot_nt(q, prev_of(kp_ref, kc_ref, s, g)), NEG_INF)
          for q, (s, g, _) in zip(qh, probs)]
    sc = [jnp.where(mask_cur, _dot_nt(q, kc_ref[rows[s], cols[g]]), NEG_INF)
          for q, (s, g, _) in zip(qh, probs)]
    m = [jnp.maximum(jnp.max(a, axis=-1, keepdims=True), jnp.max(b, axis=-1, keepdims=True))
         for a, b in zip(sp, sc)]
    pp = [jnp.exp(a - mm) for a, mm in zip(sp, m)]
    pc = [jnp.exp(a - mm) for a, mm in zip(sc, m)]
    den = [jnp.sum(a, axis=-1, keepdims=True) + jnp.sum(b, axis=-1, keepdims=True)
           for a, b in zip(pp, pc)]
    outs = [(_dot(a.astype(BF16), prev_of(vp_ref, vc_ref, s, g))
             + _dot(b.astype(BF16), vc_ref[rows[s], cols[g]])) / d
            for a, b, d, (s, g, _) in zip(pp, pc, den, probs)]
    per_band = 2 * len(pairs)
    for s in range(n_bands):
        lse_tile = jnp.zeros((BAND, LANES), F32)
        for idx in range(per_band):
            n = s * per_band + idx
            lse_tile = jnp.where(lane == idx, m[n] + jnp.log(den[n]), lse_tile)
        lse_ref[rows[s], :] = lse_tile
        for g in pairs:
            n = s * per_band + 2 * g
            o_ref[rows[s], cols[g]] = jnp.where(first, outs[n], outs[n + 1]).astype(BF16)


def _banded_attention(q, k, v, dilation, bands=2):
    bsz, n_sub, _ = q.shape
    rows = bands * BAND
    cur = pl.BlockSpec((None, rows, A_WIDTH), lambda b, r, i: (b, i, r))
    prev = pl.BlockSpec((None, BAND, A_WIDTH), lambda b, r, i: (b, jnp.maximum(i * bands - 1, 0), r))
    o, lse = pl.pallas_call(
        _attn_body,
        grid=(bsz, dilation, n_sub // rows),
        in_specs=[cur, prev, cur, prev, cur],
        out_specs=[cur, pl.BlockSpec((None, rows, LANES), lambda b, r, i: (b, i, r))],
        out_shape=[jax.ShapeDtypeStruct((bsz, n_sub, dilation * A_WIDTH), BF16),
                   jax.ShapeDtypeStruct((bsz, n_sub, dilation * LANES), F32)],
        compiler_params=_params("arbitrary", "arbitrary", "arbitrary"),
        name=f"dilated_attn_d{dilation}",
    )(q, k, k, v, v)
    return o, lse


def _out0_body(o1_ref, o4_ref, o16_ref, l1_ref, l4_ref, l16_ref, ob_ref, x_ref, vec_ref, w_ref,
               out_ref, o_nat, l_nat):
    tm = x_ref.shape[0]
    for slot, (d, o_ref, l_ref) in enumerate(zip(DILATIONS, (o1_ref, o4_ref, o16_ref),
                                                  (l1_ref, l4_ref, l16_ref))):
        for r in range(d):
            rows = pl.ds(r, tm // d, stride=d) if d > 1 else slice(None)
            for t in range(A_WIDTH // LANES):
                col = r * A_WIDTH + t * LANES
                o_nat[slot, t, rows, :] = o_ref[:, col:col + LANES].astype(F32)
            l_nat[slot, rows, :] = l_ref[:, r * LANES:(r + 1) * LANES]
    la, lb, lc = l_nat[0], l_nat[1], l_nat[2]
    m = jnp.maximum(jnp.maximum(la, lb), lc)
    ea, eb, ec = jnp.exp(la - m), jnp.exp(lb - m), jnp.exp(lc - m)
    tot = ea + eb + ec
    weights = (ea / tot, eb / tot, ec / tot)
    lane = lax.broadcasted_iota(jnp.int32, (tm, LANES), 1)
    first = lane < A_HEAD_DIM
    pieces = []
    for g in range(A_WIDTH // LANES):
        cols = slice(g * LANES, (g + 1) * LANES)
        acc = jnp.zeros((tm, LANES), F32)
        for slot, wt in enumerate(weights):
            w_pair = jnp.where(first,
                               jnp.broadcast_to(wt[:, 2 * g:2 * g + 1], (tm, LANES)),
                               jnp.broadcast_to(wt[:, 2 * g + 1:2 * g + 2], (tm, LANES)))
            acc = acc + w_pair * o_nat[slot, g]
        pieces.append(acc.astype(BF16))
    o_a = jnp.concatenate(pieces, axis=-1)
    y = _dot(o_a, w_ref[0:A_WIDTH, :]) + _dot(ob_ref[...], w_ref[A_WIDTH:A_WIDTH + POOL_WIDTH, :])
    out_ref[...] = _post_residual(x_ref[...], y, vec_ref[0:1, :], vec_ref[1:2, :])


def _out0(os, lses, ob, x, vec, w_out, tm=512):
    bsz, seq, _ = x.shape
    seq_spec = lambda width: pl.BlockSpec((None, tm, width), lambda b, i: (b, i, 0))
    folded = lambda width: [pl.BlockSpec((None, tm // d, d * width), lambda b, i: (b, i, 0))
                            for d in DILATIONS]
    n = len(DILATIONS)
    return pl.pallas_call(
        _out0_body,
        grid=(bsz, seq // tm),
        in_specs=folded(A_WIDTH) + folded(LANES) + [seq_spec(POOL_WIDTH),
                  seq_spec(D_MODEL), pl.BlockSpec((None, 8, D_MODEL), lambda b, i: (b, 0, 0)),
                  pl.BlockSpec((A_WIDTH + POOL_WIDTH, D_MODEL), lambda b, i: (0, 0))],
        out_specs=seq_spec(D_MODEL),
        out_shape=jax.ShapeDtypeStruct((bsz, seq, D_MODEL), F32),
        scratch_shapes=[pltpu.VMEM((n, A_WIDTH // LANES, tm, LANES), F32),
                        pltpu.VMEM((n, tm, LANES), F32)],
        compiler_params=_params("arbitrary", "arbitrary"),
        name="l0_out_proj",
    )(*os, *lses, ob, x, vec, w_out.astype(BF16))


def _ffn_body(x_ref, vec_ref, wg_ref, wu_ref, wd_ref, out_ref, hbuf, acc):
    f = pl.program_id(1)

    @pl.when(f == 0)
    def _():
        hbuf[...] = _norm_mod(x_ref[...], vec_ref[0:1, :], vec_ref[1:2, :],
                              vec_ref[2:3, :]).astype(BF16)
        acc[...] = jnp.zeros_like(acc)

    h = hbuf[...]
    act = _silu(_dot(h, wg_ref[...].astype(BF16))) * _dot(h, wu_ref[...].astype(BF16))
    acc[...] += _dot(act.astype(BF16), wd_ref[...].astype(BF16))

    @pl.when(f == pl.num_programs(1) - 1)
    def _():
        out_ref[...] = _post_residual(x_ref[...], acc[...], vec_ref[3:4, :], vec_ref[4:5, :])


def _ffn(x, vec, w_gate, w_up, w_down, tm=1024, tf=512):
    bsz, seq, _ = x.shape
    tiles_per_seq = seq // tm
    xt = x.reshape(bsz * seq, D_MODEL)
    row = pl.BlockSpec((tm, D_MODEL), lambda i, f: (i, 0))
    out = pl.pallas_call(
        _ffn_body,
        grid=(bsz * seq // tm, FFN_DIM // tf),
        in_specs=[row, pl.BlockSpec((None, 8, D_MODEL), lambda i, f: (i // tiles_per_seq, 0, 0)),
                  pl.BlockSpec((D_MODEL, tf), lambda i, f: (0, f)),
                  pl.BlockSpec((D_MODEL, tf), lambda i, f: (0, f)),
                  pl.BlockSpec((tf, D_MODEL), lambda i, f: (f, 0))],
        out_specs=row,
        out_shape=jax.ShapeDtypeStruct((bsz * seq, D_MODEL), F32),
        scratch_shapes=[pltpu.VMEM((tm, D_MODEL), BF16), pltpu.VMEM((tm, D_MODEL), F32)],
        compiler_params=_params("arbitrary", "arbitrary"),
        name="l0_swiglu",
    )(xt, vec, w_gate, w_up, w_down)
    return out.reshape(bsz, seq, D_MODEL)


def _split3(a):
    a1 = a.astype(BF16)
    r1 = a - a1.astype(F32)
    a2 = r1.astype(BF16)
    a3 = (r1 - a2.astype(F32)).astype(BF16)
    return a1, a2, a3


def _dot_split(a, b):
    a_hi = a.astype(BF16)
    a_lo = (a - a_hi.astype(F32)).astype(BF16)
    b_hi = b.astype(BF16)
    b_lo = (b - b_hi.astype(F32)).astype(BF16)
    return _dot(a_hi, b_hi) + (_dot(a_hi, b_lo) + _dot(a_lo, b_hi))


def _in1_body(x_ref, vec_ref, w_ref, wba_ref, cw_ref, hp_ref,
              q_ref, k_ref, v_ref, gate_ref, bg_ref, cbuf):
    i = pl.program_id(1)
    tm = x_ref.shape[0]

    @pl.when(i == 0)
    def _():
        cbuf[:, 0:CONV_HALO, :] = jnp.zeros((3, CONV_HALO, DN_WIDTH), F32)

    hf = _norm_mod(x_ref[...], vec_ref[0:1, :], vec_ref[1:2, :], vec_ref[2:3, :])
    h = hf.astype(BF16)

    for idx, dst in enumerate((q_ref, k_ref, v_ref)):
        cols = slice(idx * DN_WIDTH, (idx + 1) * DN_WIDTH)
        cbuf[idx, CONV_HALO:CONV_HALO + tm, :] = _dot(h, w_ref[:, cols])
        y = jnp.zeros((tm, DN_WIDTH), F32)
        for j in range(CONV_WIDTH):
            off = CONV_HALO - (CONV_WIDTH - 1) + j
            y = y + cw_ref[j:j + 1, cols] * cbuf[idx, off:off + tm, :]
        y = _silu(y)
        cbuf[idx, 0:CONV_HALO, :] = cbuf[idx, tm:tm + CONV_HALO, :]
        if idx < 2:
            scale = DN_HEAD_DIM ** -0.5 if idx == 0 else 1.0
            for hd in range(DN_HEADS):
                hc = slice(hd * DN_HEAD_DIM, (hd + 1) * DN_HEAD_DIM)
                blk = y[:, hc]
                ss = jnp.sum(blk * blk, axis=-1, keepdims=True)
                dst[:, hc] = (blk * (lax.rsqrt(ss + EPS) * scale)).astype(BF16)
        else:
            dst[...] = y.astype(BF16)

    gate_ref[...] = _silu(_dot(h, w_ref[:, 3 * DN_WIDTH:4 * DN_WIDTH])).astype(BF16)

    ba = _dot_split(hf, wba_ref[...])
    beta = jax.nn.sigmoid(ba)
    z = ba + hp_ref[1:2, :]
    softplus = jnp.maximum(z, 0.0) + jnp.log1p(jnp.exp(-jnp.abs(z)))
    g = -jnp.exp(hp_ref[0:1, :]) * softplus
    r = lax.broadcasted_iota(jnp.int32, (tm, tm), 0)
    c = lax.broadcasted_iota(jnp.int32, (tm, tm), 1)
    tri = jnp.where((r // CHUNK == c // CHUNK) & (c <= r), 1.0, 0.0).astype(BF16)
    g1, g2, g3 = _split3(g)
    gc = _dot(tri, g1) + _dot(tri, g2) + _dot(tri, g3)
    lane = lax.broadcasted_iota(jnp.int32, (tm, LANES), 1)
    bg_ref[...] = jnp.where(lane < DN_HEADS, beta, gc)


def _in1(x, vec, w_in, conv_w, a_log, dt_bias, tm=256):
    bsz, seq, _ = x.shape
    w_main = w_in[:, :4 * DN_WIDTH].astype(BF16)
    w_ba = jnp.zeros((D_MODEL, LANES), F32).at[:, :2 * DN_HEADS].set(w_in[:, 4 * DN_WIDTH:])
    hp = jnp.zeros((8, LANES), F32)
    hp = hp.at[0, DN_HEADS:2 * DN_HEADS].set(a_log).at[1, DN_HEADS:2 * DN_HEADS].set(dt_bias)
    seq_spec = lambda width: pl.BlockSpec((None, tm, width), lambda b, i: (b, i, 0))
    full = lambda shape: pl.BlockSpec(shape, lambda b, i: (0,) * len(shape))
    wide = jax.ShapeDtypeStruct((bsz, seq, DN_WIDTH), BF16)
    return pl.pallas_call(
        _in1_body,
        grid=(bsz, seq // tm),
        in_specs=[seq_spec(D_MODEL), pl.BlockSpec((None, 8, D_MODEL), lambda b, i: (b, 0, 0)),
                  full((D_MODEL, 4 * DN_WIDTH)), full((D_MODEL, LANES)),
                  full((CONV_WIDTH, 3 * DN_WIDTH)), full((8, LANES))],
        out_specs=[seq_spec(DN_WIDTH)] * 4 + [seq_spec(LANES)],
        out_shape=[wide] * 4 + [jax.ShapeDtypeStruct((bsz, seq, LANES), F32)],
        scratch_shapes=[pltpu.VMEM((3, tm + CONV_HALO, DN_WIDTH), F32)],
        compiler_params=_params("arbitrary", "arbitrary"),
        name="l1_in_proj",
    )(x, vec, w_main, w_ba, conv_w, hp)


def _unit_lower_inverses(lmats):
    n = lmats[0].shape[0]
    r = lax.broadcasted_iota(jnp.int32, (n, n), 0)
    c = lax.broadcasted_iota(jnp.int32, (n, n), 1)

    def below(size):
        return (r // (2 * size) == c // (2 * size)) & (r % (2 * size) >= size) & (c % (2 * size) < size)

    eye = jnp.where(r == c, 1.0, 0.0).astype(F32)
    first = below(1)
    invs = [eye - jnp.where(first, lm, 0.0) for lm in lmats]
    size = 2
    while size < n:
        mask = below(size)
        offs = [jnp.where(mask, lm, 0.0).astype(BF16) for lm in lmats]
        inv16 = [x.astype(BF16) for x in invs]
        xc = [_dot(x, o).astype(BF16) for x, o in zip(inv16, offs)]
        invs = [x - _dot(t, x16) for x, t, x16 in zip(invs, xc, inv16)]
        size *= 2
    return invs


def _delta_body(q_ref, k_ref, v_ref, gate_ref, bg_ref, gcr_ref, ng_ref, o_ref,
                state, p_s, n_s, qp_s, op_s, dec_s):
    i = pl.program_id(1)
    n_chunks = q_ref.shape[0] // CHUNK
    heads = range(DN_HEADS)
    head_cols = [slice(hd * DN_HEAD_DIM, (hd + 1) * DN_HEAD_DIM) for hd in heads]

    @pl.when(i == 0)
    def _():
        state[...] = jnp.zeros_like(state)

    r = lax.broadcasted_iota(jnp.int32, (CHUNK, CHUNK), 0)
    c = lax.broadcasted_iota(jnp.int32, (CHUNK, CHUNK), 1)
    causal = r >= c
    strict = r > c

    def prepare(step, carry):
        chunks = [step * PREP_CHUNKS + n for n in range(PREP_CHUNKS)]
        rows = [pl.ds(pl.multiple_of(ci * CHUNK, CHUNK), CHUNK) for ci in chunks]
        bg = [bg_ref[r, :] for r in rows]
        gcr_all = [gcr_ref[ci] for ci in chunks]
        probs = [(n, hd) for n in range(PREP_CHUNKS) for hd in heads]
        k16 = [k_ref[rows[n], head_cols[hd]] for n, hd in probs]
        q16 = [q_ref[rows[n], head_cols[hd]] for n, hd in probs]
        kf = [x.astype(F32) for x in k16]
        beta = [bg[n][:, hd:hd + 1] for n, hd in probs]
        gcc = [bg[n][:, DN_HEADS + hd:DN_HEADS + hd + 1] for n, hd in probs]
        gcr = [gcr_all[n][hd:hd + 1, :] for n, hd in probs]
        kb = [x * b for x, b in zip(kf, beta)]
        both = [_dot_nt(jnp.concatenate([a.astype(BF16), b], axis=0), x)
                for a, b, x in zip(kb, q16, k16)]
        decay = [jnp.exp(jnp.where(causal, a - b, NEG_INF)) for a, b in zip(gcc, gcr)]
        lmats = [jnp.where(strict, m[0:CHUNK] * d, 0.0) for m, d in zip(both, decay)]
        attn = [(m[CHUNK:2 * CHUNK] * d).astype(BF16) for m, d in zip(both, decay)]
        invs = _unit_lower_inverses(lmats)
        eg = [jnp.exp(x) for x in gcc]
        rhs = [jnp.concatenate([(v_ref[rows[n], head_cols[hd]].astype(F32) * b).astype(BF16),
                                (a * e).astype(BF16)], axis=1)
               for (n, hd), b, a, e in zip(probs, beta, kb, eg)]
        sol = [_dot(x.astype(BF16), y).astype(BF16) for x, y in zip(invs, rhs)]
        au = [_dot(a, s) for a, s in zip(attn, sol)]
        g_last = [x[CHUNK - 1:CHUNK, :] for x in gcc]
        kg = [(x * jnp.exp(gl - g)).astype(BF16) for x, gl, g in zip(kf, g_last, gcc)]
        kn = [_dot_tn(a, s) for a, s in zip(kg, sol)]
        for idx, (n, hd) in enumerate(probs):
            ci = chunks[n]
            n_s[ci, hd] = kn[idx][:, 0:DN_HEAD_DIM]
            p_s[ci, hd] = kn[idx][:, DN_HEAD_DIM:2 * DN_HEAD_DIM].astype(BF16)
            op_s[ci, hd] = au[idx][:, 0:DN_HEAD_DIM]
            qp_s[ci, hd] = (q16[idx].astype(F32) * eg[idx]
                            - au[idx][:, DN_HEAD_DIM:2 * DN_HEAD_DIM]).astype(BF16)
            dec_s[ci, hd] = jnp.broadcast_to(jnp.exp(g_last[idx]), (8, DN_HEAD_DIM))
        return carry

    def scan(ci, carry):
        rows = pl.ds(pl.multiple_of(ci * CHUNK, CHUNK), CHUNK)
        for hd, hc in zip(heads, head_cols):
            s = state[hd]
            s16 = s.astype(BF16)
            o = _dot(qp_s[ci, hd], s16) + op_s[ci, hd]
            state[hd] = s * dec_s[ci, hd][0:1, :] + n_s[ci, hd] - _dot(p_s[ci, hd], s16)
            ms = jnp.mean(o * o, axis=-1, keepdims=True)
            o = o * lax.rsqrt(ms + EPS) * ng_ref[0:1, :] * gate_ref[rows, hc].astype(F32)
            o_ref[rows, hc] = o.astype(BF16)
        return carry

    lax.fori_loop(0, n_chunks // PREP_CHUNKS, prepare, 0)
    lax.fori_loop(0, n_chunks, scan, 0, unroll=4)


def _delta(q, k, v, gate, bg, norm_g, block=512):
    bsz, seq, _ = q.shape
    n = seq // CHUNK
    nc = block // CHUNK
    gc_rows = bg[:, :, DN_HEADS:2 * DN_HEADS].reshape(bsz, n, CHUNK, DN_HEADS).transpose(0, 1, 3, 2)
    seq_spec = lambda width: pl.BlockSpec((None, block, width), lambda b, i: (b, i, 0))
    per_head = lambda rows, dtype: pltpu.VMEM((nc, DN_HEADS, rows, DN_HEAD_DIM), dtype)
    return pl.pallas_call(
        _delta_body,
        grid=(bsz, seq // block),
        in_specs=[seq_spec(DN_WIDTH)] * 4 + [seq_spec(LANES),
                  pl.BlockSpec((None, nc, DN_HEADS, CHUNK), lambda b, i: (b, i, 0, 0)),
                  pl.BlockSpec((1, DN_HEAD_DIM), lambda b, i: (0, 0))],
        out_specs=seq_spec(DN_WIDTH),
        out_shape=jax.ShapeDtypeStruct((bsz, seq, DN_WIDTH), BF16),
        scratch_shapes=[pltpu.VMEM((DN_HEADS, DN_HEAD_DIM, DN_HEAD_DIM), F32),
                        per_head(DN_HEAD_DIM, BF16), per_head(DN_HEAD_DIM, F32),
                        per_head(CHUNK, BF16), per_head(CHUNK, F32), per_head(8, F32)],
        compiler_params=_params("arbitrary", "arbitrary"),
        name="gated_delta",
    )(q, k, v, gate, bg, gc_rows, norm_g.reshape(1, DN_HEAD_DIM))


def _route(hf, rw):
    tm = hf.shape[0]
    lane = lax.broadcasted_iota(jnp.int32, (tm, LANES), 1)
    logits = jnp.where(lane < N_EXPERTS, _dot_split(hf, rw), NEG_INF)
    m1 = jnp.max(logits, axis=-1, keepdims=True)
    i1 = jnp.min(jnp.where(logits == m1, lane, LANES), axis=-1, keepdims=True)
    rest = jnp.where(lane == i1, NEG_INF, logits)
    m2 = jnp.max(rest, axis=-1, keepdims=True)
    i2 = jnp.min(jnp.where(rest == m2, lane, LANES), axis=-1, keepdims=True)
    e2 = jnp.exp(m2 - m1)
    w1 = 1.0 / (1.0 + e2)
    w2 = e2 / (1.0 + e2)
    combine = jnp.where(lane == i1, w1, 0.0) + jnp.where(lane == i2, w2, 0.0)
    chosen = jnp.where((lane == i1) | (lane == i2), 1.0, 0.0)
    return combine, chosen


def _out1_body(a_ref, x_ref, vec_ref, w_ref, rw_ref,
               x_out, h_out, cw_out, rank_out, cnt_out, running, *, tiles_per_block):
    i = pl.program_id(1)
    tm = x_ref.shape[0]

    @pl.when(i % tiles_per_block == 0)
    def _():
        running[...] = jnp.zeros_like(running)

    rows = [slice(n * ROUTE_ROWS, (n + 1) * ROUTE_ROWS) for n in range(tm // ROUTE_ROWS)]
    y = [_dot(a_ref[rs, :], w_ref[...]) for rs in rows]
    x2 = [_post_residual(x_ref[rs, :], yy, vec_ref[0:1, :], vec_ref[1:2, :]) for rs, yy in zip(rows, y)]
    hf = [_norm_mod(xx, vec_ref[2:3, :], vec_ref[3:4, :], vec_ref[4:5, :]) for xx in x2]
    routed = [_route(hh, rw_ref[...]) for hh in hf]
    r = lax.broadcasted_iota(jnp.int32, (ROUTE_ROWS, ROUTE_ROWS), 0)
    c = lax.broadcasted_iota(jnp.int32, (ROUTE_ROWS, ROUTE_ROWS), 1)
    before = jnp.where(c < r, 1.0, 0.0).astype(BF16)
    local = [_dot(before, chosen.astype(BF16)) for _, chosen in routed]
    offset = running[0:1, :]
    for rs, xx, hh, (combine, chosen), loc in zip(rows, x2, hf, routed, local):
        x_out[rs, :] = xx
        h_out[rs, :] = hh.astype(BF16)
        cw_out[rs, :] = combine
        rank_out[rs, :] = jnp.where(chosen > 0.0, loc + offset, -1.0)
        offset = offset + jnp.sum(chosen, axis=0, keepdims=True)
    running[...] = jnp.broadcast_to(offset, running.shape)
    cnt_out[...] = running[...]


def _out1(a, x, vec, w_out, router_w, tm=512, route_block=ROUTE_BLOCK):
    bsz, seq, _ = x.shape
    tiles_per_block = route_block // tm
    blocks_per_seq = seq // route_block
    rw = jnp.zeros((D_MODEL, LANES), F32).at[:, :N_EXPERTS].set(router_w)
    seq_spec = lambda width: pl.BlockSpec((None, tm, width), lambda b, i: (b, i, 0))
    return pl.pallas_call(
        functools.partial(_out1_body, tiles_per_block=tiles_per_block),
        grid=(bsz, seq // tm),
        in_specs=[seq_spec(DN_WIDTH), seq_spec(D_MODEL),
                  pl.BlockSpec((None, 8, D_MODEL), lambda b, i: (b, 0, 0)),
                  pl.BlockSpec((DN_WIDTH, D_MODEL), lambda b, i: (0, 0)),
                  pl.BlockSpec((D_MODEL, LANES), lambda b, i: (0, 0))],
        out_specs=[seq_spec(D_MODEL), seq_spec(D_MODEL), seq_spec(LANES), seq_spec(LANES),
                   pl.BlockSpec((None, 8, LANES),
                                lambda b, i: (b * blocks_per_seq + i // tiles_per_block, 0, 0))],
        out_shape=[jax.ShapeDtypeStruct((bsz, seq, D_MODEL), F32),
                   jax.ShapeDtypeStruct((bsz, seq, D_MODEL), BF16),
                   jax.ShapeDtypeStruct((bsz, seq, LANES), F32),
                   jax.ShapeDtypeStruct((bsz, seq, LANES), F32),
                   jax.ShapeDtypeStruct((bsz * blocks_per_seq, 8, LANES), F32)],
        scratch_shapes=[pltpu.VMEM((8, LANES), F32)],
        compiler_params=_params("arbitrary", "arbitrary"),
        name="l1_out_proj_route",
    )(a, x, vec, w_out.astype(BF16), rw)


def _moe_body(cnt_ref, h_ref, rrow_ref, wrow_ref, rcol_ref, x_ref, vec_ref, wg_ref, wu_ref, wd_ref,
              out_ref, xs, ys, wslot):
    b, e, f = pl.program_id(0), pl.program_id(1), pl.program_id(2)
    tb = h_ref.shape[0]

    @pl.when((e == 0) & (f == 0))
    def _():
        out_ref[...] = jnp.zeros_like(out_ref)

    count = cnt_ref[b * N_EXPERTS + e]
    padded = (count + SEGMENTS[-1] - 1) // SEGMENTS[-1] * SEGMENTS[-1]
    n_large = padded // SEGMENTS[0]

    def for_each_segment(fn):
        def large(t, carry):
            fn(pl.multiple_of(t * SEGMENTS[0], SEGMENTS[0]), SEGMENTS[0])
            return carry

        lax.fori_loop(0, n_large, large, 0)
        done = n_large * SEGMENTS[0]
        for size in SEGMENTS[1:]:
            take = ((padded - done) // size) > 0

            @pl.when(take)
            def _(done=done, size=size):
                fn(pl.multiple_of(done, SEGMENTS[-1]), size)

            done = done + jnp.where(take, size, 0)

    @pl.when(f == 0)
    def _():
        rrow = rrow_ref[pl.ds(e, 1), :]
        wrow = wrow_ref[pl.ds(e, 1), :]

        def pack(base, size):
            rows = pl.ds(base, size)
            slot = (base + lax.broadcasted_iota(jnp.int32, (size, 1), 0)).astype(F32)
            hit = rrow == slot
            xs[rows, :] = _dot(jnp.where(hit, 1.0, 0.0).astype(BF16), h_ref[...]).astype(BF16)
            wslot[rows, :] = jnp.sum(jnp.where(hit, wrow, 0.0), axis=-1, keepdims=True)
            ys[rows, :] = jnp.zeros((size, D_MODEL), F32)

        for_each_segment(pack)

    def expert(base, size):
        rows = pl.ds(base, size)
        x = xs[rows, :]
        act = _silu(_dot(x, wg_ref[...])) * _dot(x, wu_ref[...])
        ys[rows, :] += _dot(act.astype(BF16), wd_ref[...])

    for_each_segment(expert)

    @pl.when(f == pl.num_programs(2) - 1)
    def _():
        lane = lax.broadcasted_iota(jnp.int32, (tb, LANES), 1)
        rcol = jnp.sum(jnp.where(lane == e, rcol_ref[...], 0.0), axis=-1, keepdims=True)

        def unpack(base, size):
            rows = pl.ds(base, size)
            y = (ys[rows, :] * wslot[rows, :]).astype(BF16)
            slot = (base + lax.broadcasted_iota(jnp.int32, (1, size), 1)).astype(F32)
            back = jnp.where(rcol == slot, 1.0, 0.0).astype(BF16)
            out_ref[...] += _dot(back, y)

        for_each_segment(unpack)

        @pl.when(e == pl.num_programs(1) - 1)
        def _():
            out_ref[...] = _post_residual(x_ref[...], out_ref[...], vec_ref[0:1, :], vec_ref[1:2, :])


def _moe(x, vec, h, combine, rank, counts, w_gate, w_up, w_down, route_block=ROUTE_BLOCK, tf=512):
    bsz, seq, _ = h.shape
    tokens = bsz * seq
    n_blocks = tokens // route_block
    per_block_rows = lambda t: (t.reshape(n_blocks, route_block, LANES)[:, :, :N_EXPERTS]
                                .transpose(0, 2, 1))
    cnt = counts[:, 0, :N_EXPERTS].astype(jnp.int32).reshape(n_blocks * N_EXPERTS)
    assert route_block % SEGMENTS[-1] == 0
    blocks_per_seq = seq // route_block
    once = pl.Buffered(1)
    per_block = lambda width: pl.BlockSpec((route_block, width), lambda b, e, f, cnt: (b, 0),
                                           pipeline_mode=once)
    per_expert = pl.BlockSpec((None, N_EXPERTS, route_block), lambda b, e, f, cnt: (b, 0, 0),
                              pipeline_mode=once)
    grid_spec = pltpu.PrefetchScalarGridSpec(
        num_scalar_prefetch=1,
        grid=(n_blocks, N_EXPERTS, FFN_DIM // tf),
        in_specs=[per_block(D_MODEL), per_expert, per_expert, per_block(LANES), per_block(D_MODEL),
                  pl.BlockSpec((None, 8, D_MODEL), lambda b, e, f, cnt: (b // blocks_per_seq, 0, 0)),
                  pl.BlockSpec((None, D_MODEL, tf), lambda b, e, f, cnt: (e, 0, f)),
                  pl.BlockSpec((None, D_MODEL, tf), lambda b, e, f, cnt: (e, 0, f)),
                  pl.BlockSpec((None, tf, D_MODEL), lambda b, e, f, cnt: (e, f, 0))],
        out_specs=per_block(D_MODEL),
        scratch_shapes=[pltpu.VMEM((route_block, D_MODEL), BF16), pltpu.VMEM((route_block, D_MODEL), F32),
                        pltpu.VMEM((route_block, 1), F32)])
    out = pl.pallas_call(
        _moe_body,
        grid_spec=grid_spec,
        out_shape=jax.ShapeDtypeStruct((tokens, D_MODEL), F32),
        compiler_params=_params("arbitrary", "arbitrary", "arbitrary"),
        name="l1_moe",
    )(cnt, h.reshape(tokens, D_MODEL), per_block_rows(rank), per_block_rows(combine),
      rank.reshape(tokens, LANES), x.reshape(tokens, D_MODEL), vec,
      w_gate.astype(BF16), w_up.astype(BF16), w_down.astype(BF16))
    return out.reshape(bsz, seq, D_MODEL)


def kernel(x, c, positions, ada_w, ada_b, mix_pre_g, mix_post_g, ffn_pre_g, ffn_post_g, even_w_in, even_pool_w, even_pool_scale, even_w_out, even_ffn_w_gate, even_ffn_w_up, even_ffn_w_down, odd_w_in, odd_conv_w, odd_a_log, odd_dt_bias, odd_norm_g, odd_w_out, odd_router_w, odd_moe_w_gate, odd_moe_w_up, odd_moe_w_down):
    bsz = x.shape[0]
    mod = _adaln_mod(c, ada_w, ada_b)
    sh1, sc1, gt1, sh2, sc2, gt2 = (mod[:, :, n * D_MODEL:(n + 1) * D_MODEL] for n in range(6))

    vec = _vec_rows([mix_pre_g[0], sc1[0], sh1[0]], bsz)
    qs, ks, vs, o_b = _in0(x, positions, vec, even_w_in[0], even_pool_w[0], even_pool_scale[0])
    branches = [_banded_attention(q, k, v, d) for q, k, v, d in zip(qs, ks, vs, DILATIONS)]
    vec = _vec_rows([mix_post_g[0], gt1[0]], bsz)
    x = _out0([o for o, _ in branches], [l for _, l in branches], o_b, x, vec, even_w_out[0])
    vec = _vec_rows([ffn_pre_g[0], sc2[0], sh2[0], ffn_post_g[0], gt2[0]], bsz)
    x = _ffn(x, vec, even_ffn_w_gate[0], even_ffn_w_up[0], even_ffn_w_down[0])

    vec = _vec_rows([mix_pre_g[1], sc1[1], sh1[1]], bsz)
    q, k, v, gate, bg = _in1(x, vec, odd_w_in[0], odd_conv_w[0], odd_a_log[0], odd_dt_bias[0])
    o = _delta(q, k, v, gate, bg, odd_norm_g[0])
    vec = _vec_rows([mix_post_g[1], gt1[1], ffn_pre_g[1], sc2[1], sh2[1]], bsz)
    x, h, combine, rank, counts = _out1(o, x, vec, odd_w_out[0], odd_router_w[0])
    vec = _vec_rows([ffn_post_g[1], gt2[1]], bsz)
    return _moe(x, vec, h, combine, rank, counts,
                odd_moe_w_gate[0], odd_moe_w_up[0], odd_moe_w_down[0])
```

```python
import functools

import numpy as np
import jax
import jax.numpy as jnp
from jax import lax
from jax.experimental import pallas as pl
from jax.experimental.pallas import tpu as pltpu

F32 = jnp.float32
BF16 = jnp.bfloat16
HIGHEST = lax.Precision.HIGHEST

D_MODEL = 1024
DEPTH = 2
A_HEADS = 8
A_HEAD_DIM = 64
A_WIDTH = 512
DILATIONS = (1, 4, 16)
BAND = 128
ROT_DIM = 16
ROPE_THETA = 500000.0
POOL_WINDOWS = (2, 4, 8, 16)
POOL_GROUP_DIM = 128
POOL_WIDTH = 512
POOL_HALO = 16
DN_HEADS = 8
DN_HEAD_DIM = 128
DN_WIDTH = 1024
CONV_WIDTH = 4
CONV_HALO = 8
CHUNK = 64
PREP_CHUNKS = 4
FFN_DIM = 3584
N_EXPERTS = 8
EPS = 1e-6
LANES = 128
ROUTE_BLOCK = 2048
SEGMENTS = (512, 256, 128)
ROUTE_ROWS = 256
NEG_INF = float("-inf")

VMEM_LIMIT = 56 * 1024 * 1024


def _params(*sem):
    return pltpu.CompilerParams(dimension_semantics=sem, vmem_limit_bytes=VMEM_LIMIT)


def _dot(a, b, precision=None):
    return jnp.dot(a, b, preferred_element_type=F32, precision=precision)


def _dot_nt(a, b, precision=None):
    return lax.dot_general(a, b, (((1,), (1,)), ((), ())), preferred_element_type=F32,
                           precision=precision)


def _dot_tn(a, b, precision=None):
    return lax.dot_general(a, b, (((0,), (0,)), ((), ())), preferred_element_type=F32,
                           precision=precision)


def _silu(x):
    return x * jax.nn.sigmoid(x)


def _norm_mod(x, g, sc, sh):
    ms = jnp.mean(x * x, axis=-1, keepdims=True)
    return x * lax.rsqrt(ms + EPS) * g * (1.0 + sc) + sh


def _post_residual(x, y, g, gt):
    ms = jnp.mean(y * y, axis=-1, keepdims=True)
    return x + gt * (y * lax.rsqrt(ms + EPS) * g)


def _mod_body(c_ref, w_ref, b_ref, o_ref):
    c = c_ref[...]
    o_ref[...] = _dot(_silu(c), w_ref[...], precision=HIGHEST) + b_ref[...]


def _adaln_mod(c, ada_w, ada_b):
    bsz = c.shape[0]
    rows = 8
    c_pad = jnp.zeros((rows, D_MODEL), F32).at[:bsz].set(c)
    tn = 1536
    out = pl.pallas_call(
        _mod_body,
        grid=(DEPTH, 6 * D_MODEL // tn),
        in_specs=[pl.BlockSpec((rows, D_MODEL), lambda l, n: (0, 0)),
                  pl.BlockSpec((None, D_MODEL, tn), lambda l, n: (l, 0, n)),
                  pl.BlockSpec((None, 1, tn), lambda l, n: (l, 0, n))],
        out_specs=pl.BlockSpec((None, rows, tn), lambda l, n: (l, 0, n)),
        out_shape=jax.ShapeDtypeStruct((DEPTH, rows, 6 * D_MODEL), F32),
        compiler_params=_params("arbitrary", "arbitrary"),
        name="adaln_mod",
    )(c_pad, ada_w, ada_b.reshape(DEPTH, 1, 6 * D_MODEL))
    return out[:, :bsz]


def _vec_rows(rows, bsz):
    rows = [jnp.broadcast_to(r.astype(F32), (bsz, D_MODEL)) for r in rows]
    rows = rows + [jnp.zeros((bsz, D_MODEL), F32)] * (8 - len(rows))
    return jnp.stack(rows, axis=1)


def _rope(t, cosv, sin_lo, sin_hi):
    return (t * cosv + pltpu.roll(t, LANES - ROT_DIM // 2, axis=1) * sin_lo
            + pltpu.roll(t, ROT_DIM // 2, axis=1) * sin_hi)


def _store_folded(nat, refs):
    tm = nat.shape[1]
    for d, ref in zip(DILATIONS, refs):
        for r in range(d):
            rows = pl.ds(r, tm // d, stride=d) if d > 1 else slice(None)
            for t in range(A_WIDTH // LANES):
                col = r * A_WIDTH + t * LANES
                ref[:, col:col + LANES] = nat[t, rows, :].astype(BF16)


def _in0_body(x_ref, pos_ref, vec_ref, w_ref, freq_ref, exp_ref, plain_ref, pw_ref, ps_ref,
              q1_ref, q4_ref, q16_ref, k1_ref, k4_ref, k16_ref, v1_ref, v4_ref, v16_ref, ob_ref,
              ubuf, nat):
    i = pl.program_id(1)
    tm = x_ref.shape[0]

    @pl.when(i == 0)
    def _():
        ubuf[0:POOL_HALO, :] = jnp.zeros((POOL_HALO, POOL_WIDTH), F32)

    h = _norm_mod(x_ref[...], vec_ref[0:1, :], vec_ref[1:2, :], vec_ref[2:3, :]).astype(BF16)

    ang = freq_ref[...] * pos_ref[...].astype(F32)

    def spread(v, e):
        lead = v.astype(BF16).astype(F32)
        return _dot_tn(lead, e) + _dot_tn(v - lead, e)

    cos_t = jnp.cos(ang)
    sin_t = jnp.sin(ang)
    cosv = spread(cos_t, exp_ref[0]) + plain_ref[...]
    sin_lo = -spread(sin_t, exp_ref[1])
    sin_hi = spread(sin_t, exp_ref[2])

    pq = _dot(h, w_ref[:, 0:A_WIDTH])
    for t in range(A_WIDTH // LANES):
        cols = slice(t * LANES, (t + 1) * LANES)
        nat[t] = _rope(pq[:, cols], cosv, sin_lo, sin_hi) * (A_HEAD_DIM ** -0.5)
    _store_folded(nat, (q1_ref, q4_ref, q16_ref))
    pk = _dot(h, w_ref[:, A_WIDTH:2 * A_WIDTH])
    for t in range(A_WIDTH // LANES):
        cols = slice(t * LANES, (t + 1) * LANES)
        nat[t] = _rope(pk[:, cols], cosv, sin_lo, sin_hi)
    _store_folded(nat, (k1_ref, k4_ref, k16_ref))
    pv = _dot(h, w_ref[:, 2 * A_WIDTH:3 * A_WIDTH])
    for t in range(A_WIDTH // LANES):
        nat[t] = pv[:, t * LANES:(t + 1) * LANES]
    _store_folded(nat, (v1_ref, v4_ref, v16_ref))

    ubuf[POOL_HALO:POOL_HALO + tm, :] = _dot(h, w_ref[:, 3 * A_WIDTH:3 * A_WIDTH + POOL_WIDTH])
    tpos = i * tm + lax.broadcasted_iota(jnp.int32, (tm, 1), 0)
    for g, win in enumerate(POOL_WINDOWS):
        cols = slice(g * POOL_GROUP_DIM, (g + 1) * POOL_GROUP_DIM)
        cur = ubuf[POOL_HALO:POOL_HALO + tm, cols]
        acc = cur
        for j in range(1, win):
            acc = acc + ubuf[POOL_HALO - j:POOL_HALO - j + tm, cols]
        cnt = jnp.minimum(tpos + 1, win).astype(F32)
        pooled = acc / cnt - cur
        mixed = _dot(pooled.astype(BF16), pw_ref[g]) * ps_ref[0:1, cols]
        ob_ref[:, cols] = mixed.astype(BF16)
    ubuf[0:POOL_HALO, :] = ubuf[tm:tm + POOL_HALO, :]


def _rope_tables(tm):
    half = ROT_DIM // 2
    inv_freq = ROPE_THETA ** (-jnp.arange(0, ROT_DIM, 2, dtype=F32) / ROT_DIM)
    freq = jnp.broadcast_to(inv_freq[:, None], (half, tm))
    lane = np.arange(LANES) % A_HEAD_DIM
    first, second = lane < half, (lane >= half) & (lane < ROT_DIM)
    uses = lane[None, :] % half == np.arange(half)[:, None]
    expand = np.stack([uses & (first | second), uses & first, uses & second]).astype(np.float32)
    plain = (lane >= ROT_DIM).astype(np.float32).reshape(1, LANES)
    return freq, jnp.asarray(expand), jnp.asarray(plain)


def _in0(x, positions, vec, w_in, pool_w, pool_scale, tm=512):
    bsz, seq, _ = x.shape
    n_out = 3 * A_WIDTH + POOL_WIDTH
    freq, expand, plain = _rope_tables(tm)
    seq_spec = lambda width: pl.BlockSpec((None, tm, width), lambda b, i: (b, i, 0))
    full = lambda shape: pl.BlockSpec(shape, lambda b, i: (0,) * len(shape))
    folded_spec = lambda d: pl.BlockSpec((None, tm // d, d * A_WIDTH), lambda b, i: (b, i, 0))
    folded_sds = lambda d: jax.ShapeDtypeStruct((bsz, seq // d, d * A_WIDTH), BF16)
    outs = pl.pallas_call(
        _in0_body,
        grid=(bsz, seq // tm),
        in_specs=[seq_spec(D_MODEL), pl.BlockSpec((None, None, 1, tm), lambda b, i: (b, i, 0, 0)),
                  pl.BlockSpec((None, 8, D_MODEL), lambda b, i: (b, 0, 0)),
                  full((D_MODEL, n_out)), full(freq.shape), full(expand.shape), full(plain.shape),
                  full((len(POOL_WINDOWS), POOL_GROUP_DIM, POOL_GROUP_DIM)), full((1, POOL_WIDTH))],
        out_specs=[folded_spec(d) for d in DILATIONS] * 3 + [seq_spec(POOL_WIDTH)],
        out_shape=[folded_sds(d) for d in DILATIONS] * 3 + [folded_sds(1)],
        scratch_shapes=[pltpu.VMEM((tm + POOL_HALO, POOL_WIDTH), F32),
                        pltpu.VMEM((A_WIDTH // LANES, tm, LANES), F32)],
        compiler_params=_params("arbitrary", "arbitrary"),
        name="l0_in_proj",
    )(x, positions.reshape(bsz, seq // tm, 1, tm), vec, w_in.astype(BF16), freq, expand, plain,
      pool_w.astype(BF16), pool_scale.reshape(1, POOL_WIDTH))
    n = len(DILATIONS)
    return outs[0:n], outs[n:2 * n], outs[2 * n:3 * n], outs[3 * n]


def _attn_body(q_ref, kp_ref, kc_ref, vp_ref, vc_ref, o_ref, lse_ref):
    i = pl.program_id(2)
    n_bands = q_ref.shape[0] // BAND
    qi = lax.broadcasted_iota(jnp.int32, (BAND, BAND), 0)
    kj = lax.broadcasted_iota(jnp.int32, (BAND, BAND), 1)
    back = kj >= qi
    mask_cur = kj <= qi
    lane = lax.broadcasted_iota(jnp.int32, (BAND, LANES), 1)
    first = lane < A_HEAD_DIM
    pairs = range(A_WIDTH // LANES)
    cols = [slice(g * LANES, (g + 1) * LANES) for g in pairs]
    rows = [slice(s * BAND, (s + 1) * BAND) for s in range(n_bands)]
    probs = [(s, g, half) for s in range(n_bands) for g in pairs for half in range(2)]

    def prev_of(ref_prev, ref_cur, s, g):
        return ref_prev[:, cols[g]] if s == 0 else ref_cur[rows[s - 1], cols[g]]

    qh = []
    for s, g, half in probs:
        qp = q_ref[rows[s], cols[g]]
        qh.append(jnp.where(first if half == 0 else jnp.logical_not(first), qp, jnp.zeros_like(qp)))
    sp = [jnp.where(back & (i > 0) if s == 0 else back, _dot_nt(q, prev_of(kp_ref, kc_ref, s, g)), NEG_INF)
          for q, (s, g, _) in zip(qh, probs)]
    sc = [jnp.where(mask_cur, _dot_nt(q, kc_ref[rows[s], cols[g]]), NEG_INF)
          for q, (s, g, _) in zip(qh, probs)]
    m = [jnp.maximum(jnp.max(a, axis=-1, keepdims=True), jnp.max(b, axis=-1, keepdims=True))
         for a, b in zip(sp, sc)]
    pp = [jnp.exp(a - mm) for a, mm in zip(sp, m)]
    pc = [jnp.exp(a - mm) for a, mm in zip(sc, m)]
    den = [jnp.sum(a, axis=-1, keepdims=True) + jnp.sum(b, axis=-1, keepdims=True)
           for a, b in zip(pp, pc)]
    outs = [(_dot(a.astype(BF16), prev_of(vp_ref, vc_ref, s, g))
             + _dot(b.astype(BF16), vc_ref[rows[s], cols[g]])) / d
            for a, b, d, (s, g, _) in zip(pp, pc, den, probs)]
    per_band = 2 * len(pairs)
    for s in range(n_bands):
        lse_tile = jnp.zeros((BAND, LANES), F32)
        for idx in range(per_band):
            n = s * per_band + idx
            lse_tile = jnp.where(lane == idx, m[n] + jnp.log(den[n]), lse_tile)
        lse_ref[rows[s], :] = lse_tile
        for g in pairs:
            n = s * per_band + 2 * g
            o_ref[rows[s], cols[g]] = jnp.where(first, outs[n], outs[n + 1]).astype(BF16)


def _banded_attention(q, k, v, dilation, bands=4):
    bsz, n_sub, _ = q.shape
    rows = bands * BAND
    cur = pl.BlockSpec((None, rows, A_WIDTH), lambda b, r, i: (b, i, r))
    prev = pl.BlockSpec((None, BAND, A_WIDTH), lambda b, r, i: (b, jnp.maximum(i * bands - 1, 0), r))
    o, lse = pl.pallas_call(
        _attn_body,
        grid=(bsz, dilation, n_sub // rows),
        in_specs=[cur, prev, cur, prev, cur],
        out_specs=[cur, pl.BlockSpec((None, rows, LANES), lambda b, r, i: (b, i, r))],
        out_shape=[jax.ShapeDtypeStruct((bsz, n_sub, dilation * A_WIDTH), BF16),
                   jax.ShapeDtypeStruct((bsz, n_sub, dilation * LANES), F32)],
        compiler_params=_params("arbitrary", "arbitrary", "arbitrary"),
        name=f"dilated_attn_d{dilation}",
    )(q, k, k, v, v)
    return o, lse


def _out0_body(o1_ref, o4_ref, o16_ref, l1_ref, l4_ref, l16_ref, ob_ref, x_ref, vec_ref, w_ref,
               out_ref, o_nat, l_nat):
    tm = x_ref.shape[0]
    for slot, (d, o_ref, l_ref) in enumerate(zip(DILATIONS, (o1_ref, o4_ref, o16_ref),
                                                  (l1_ref, l4_ref, l16_ref))):
        for r in range(d):
            rows = pl.ds(r, tm // d, stride=d) if d > 1 else slice(None)
            for t in range(A_WIDTH // LANES):
                col = r * A_WIDTH + t * LANES
                o_nat[slot, t, rows, :] = o_ref[:, col:col + LANES].astype(F32)
            l_nat[slot, rows, :] = l_ref[:, r * LANES:(r + 1) * LANES]
    la, lb, lc = l_nat[0], l_nat[1], l_nat[2]
    m = jnp.maximum(jnp.maximum(la, lb), lc)
    ea, eb, ec = jnp.exp(la - m), jnp.exp(lb - m), jnp.exp(lc - m)
    tot = ea + eb + ec
    weights = (ea / tot, eb / tot, ec / tot)
    lane = lax.broadcasted_iota(jnp.int32, (tm, LANES), 1)
    first = lane < A_HEAD_DIM
    pieces = []
    for g in range(A_WIDTH // LANES):
        cols = slice(g * LANES, (g + 1) * LANES)
        acc = jnp.zeros((tm, LANES), F32)
        for slot, wt in enumerate(weights):
            w_pair = jnp.where(first,
                               jnp.broadcast_to(wt[:, 2 * g:2 * g + 1], (tm, LANES)),
                               jnp.broadcast_to(wt[:, 2 * g + 1:2 * g + 2], (tm, LANES)))
            acc = acc + w_pair * o_nat[slot, g]
        pieces.append(acc.astype(BF16))
    o_a = jnp.concatenate(pieces, axis=-1)
    y = _dot(o_a, w_ref[0:A_WIDTH, :]) + _dot(ob_ref[...], w_ref[A_WIDTH:A_WIDTH + POOL_WIDTH, :])
    out_ref[...] = _post_residual(x_ref[...], y, vec_ref[0:1, :], vec_ref[1:2, :])


def _out0(os, lses, ob, x, vec, w_out, tm=512):
    bsz, seq, _ = x.shape
    seq_spec = lambda width: pl.BlockSpec((None, tm, width), lambda b, i: (b, i, 0))
    folded = lambda width: [pl.BlockSpec((None, tm // d, d * width), lambda b, i: (b, i, 0))
                            for d in DILATIONS]
    n = len(DILATIONS)
    return pl.pallas_call(
        _out0_body,
        grid=(bsz, seq // tm),
        in_specs=folded(A_WIDTH) + folded(LANES) + [seq_spec(POOL_WIDTH),
                  seq_spec(D_MODEL), pl.BlockSpec((None, 8, D_MODEL), lambda b, i: (b, 0, 0)),
                  pl.BlockSpec((A_WIDTH + POOL_WIDTH, D_MODEL), lambda b, i: (0, 0))],
        out_specs=seq_spec(D_MODEL),
        out_shape=jax.ShapeDtypeStruct((bsz, seq, D_MODEL), F32),
        scratch_shapes=[pltpu.VMEM((n, A_WIDTH // LANES, tm, LANES), F32),
                        pltpu.VMEM((n, tm, LANES), F32)],
        compiler_params=_params("arbitrary", "arbitrary"),
        name="l0_out_proj",
    )(*os, *lses, ob, x, vec, w_out.astype(BF16))


def _ffn_body(x_ref, vec_ref, wg_ref, wu_ref, wd_ref, out_ref, hbuf, acc):
    f = pl.program_id(1)

    @pl.when(f == 0)
    def _():
        hbuf[...] = _norm_mod(x_ref[...], vec_ref[0:1, :], vec_ref[1:2, :],
                              vec_ref[2:3, :]).astype(BF16)
        acc[...] = jnp.zeros_like(acc)

    h = hbuf[...]
    act = _silu(_dot(h, wg_ref[...].astype(BF16))) * _dot(h, wu_ref[...].astype(BF16))
    acc[...] += _dot(act.astype(BF16), wd_ref[...].astype(BF16))

    @pl.when(f == pl.num_programs(1) - 1)
    def _():
        out_ref[...] = _post_residual(x_ref[...], acc[...], vec_ref[3:4, :], vec_ref[4:5, :])


def _ffn(x, vec, w_gate, w_up, w_down, tm=1024, tf=512):
    bsz, seq, _ = x.shape
    tiles_per_seq = seq // tm
    xt = x.reshape(bsz * seq, D_MODEL)
    row = pl.BlockSpec((tm, D_MODEL), lambda i, f: (i, 0))
    out = pl.pallas_call(
        _ffn_body,
        grid=(bsz * seq // tm, FFN_DIM // tf),
        in_specs=[row, pl.BlockSpec((None, 8, D_MODEL), lambda i, f: (i // tiles_per_seq, 0, 0)),
                  pl.BlockSpec((D_MODEL, tf), lambda i, f: (0, f)),
                  pl.BlockSpec((D_MODEL, tf), lambda i, f: (0, f)),
                  pl.BlockSpec((tf, D_MODEL), lambda i, f: (f, 0))],
        out_specs=row,
        out_shape=jax.ShapeDtypeStruct((bsz * seq, D_MODEL), F32),
        scratch_shapes=[pltpu.VMEM((tm, D_MODEL), BF16), pltpu.VMEM((tm, D_MODEL), F32)],
        compiler_params=_params("arbitrary", "arbitrary"),
        name="l0_swiglu",
    )(xt, vec, w_gate, w_up, w_down)
    return out.reshape(bsz, seq, D_MODEL)


def _split3(a):
    a1 = a.astype(BF16)
    r1 = a - a1.astype(F32)
    a2 = r1.astype(BF16)
    a3 = (r1 - a2.astype(F32)).astype(BF16)
    return a1, a2, a3


def _dot_split(a, b):
    a_hi = a.astype(BF16)
    a_lo = (a - a_hi.astype(F32)).astype(BF16)
    b_hi = b.astype(BF16)
    b_lo = (b - b_hi.astype(F32)).astype(BF16)
    return _dot(a_hi, b_hi) + (_dot(a_hi, b_lo) + _dot(a_lo, b_hi))


def _in1_body(x_ref, vec_ref, w_ref, wba_ref, cw_ref, hp_ref,
              q_ref, k_ref, v_ref, gate_ref, bg_ref, cbuf):
    i = pl.program_id(1)
    tm = x_ref.shape[0]

    @pl.when(i == 0)
    def _():
        cbuf[:, 0:CONV_HALO, :] = jnp.zeros((3, CONV_HALO, DN_WIDTH), F32)

    hf = _norm_mod(x_ref[...], vec_ref[0:1, :], vec_ref[1:2, :], vec_ref[2:3, :])
    h = hf.astype(BF16)

    for idx, dst in enumerate((q_ref, k_ref, v_ref)):
        cols = slice(idx * DN_WIDTH, (idx + 1) * DN_WIDTH)
        cbuf[idx, CONV_HALO:CONV_HALO + tm, :] = _dot(h, w_ref[:, cols])
        y = jnp.zeros((tm, DN_WIDTH), F32)
        for j in range(CONV_WIDTH):
            off = CONV_HALO - (CONV_WIDTH - 1) + j
            y = y + cw_ref[j:j + 1, cols] * cbuf[idx, off:off + tm, :]
        y = _silu(y)
        cbuf[idx, 0:CONV_HALO, :] = cbuf[idx, tm:tm + CONV_HALO, :]
        if idx < 2:
            scale = DN_HEAD_DIM ** -0.5 if idx == 0 else 1.0
            for hd in range(DN_HEADS):
                hc = slice(hd * DN_HEAD_DIM, (hd + 1) * DN_HEAD_DIM)
                blk = y[:, hc]
                ss = jnp.sum(blk * blk, axis=-1, keepdims=True)
                dst[:, hc] = (blk * (lax.rsqrt(ss + EPS) * scale)).astype(BF16)
        else:
            dst[...] = y.astype(BF16)

    gate_ref[...] = _silu(_dot(h, w_ref[:, 3 * DN_WIDTH:4 * DN_WIDTH])).astype(BF16)

    ba = _dot_split(hf, wba_ref[...])
    beta = jax.nn.sigmoid(ba)
    z = ba + hp_ref[1:2, :]
    softplus = jnp.maximum(z, 0.0) + jnp.log1p(jnp.exp(-jnp.abs(z)))
    g = -jnp.exp(hp_ref[0:1, :]) * softplus
    r = lax.broadcasted_iota(jnp.int32, (tm, tm), 0)
    c = lax.broadcasted_iota(jnp.int32, (tm, tm), 1)
    tri = jnp.where((r // CHUNK == c // CHUNK) & (c <= r), 1.0, 0.0).astype(BF16)
    g1, g2, g3 = _split3(g)
    gc = _dot(tri, g1) + _dot(tri, g2) + _dot(tri, g3)
    lane = lax.broadcasted_iota(jnp.int32, (tm, LANES), 1)
    bg_ref[...] = jnp.where(lane < DN_HEADS, beta, gc)


def _in1(x, vec, w_in, conv_w, a_log, dt_bias, tm=256):
    bsz, seq, _ = x.shape
    w_main = w_in[:, :4 * DN_WIDTH].astype(BF16)
    w_ba = jnp.zeros((D_MODEL, LANES), F32).at[:, :2 * DN_HEADS].set(w_in[:, 4 * DN_WIDTH:])
    hp = jnp.zeros((8, LANES), F32)
    hp = hp.at[0, DN_HEADS:2 * DN_HEADS].set(a_log).at[1, DN_HEADS:2 * DN_HEADS].set(dt_bias)
    seq_spec = lambda width: pl.BlockSpec((None, tm, width), lambda b, i: (b, i, 0))
    full = lambda shape: pl.BlockSpec(shape, lambda b, i: (0,) * len(shape))
    wide = jax.ShapeDtypeStruct((bsz, seq, DN_WIDTH), BF16)
    return pl.pallas_call(
        _in1_body,
        grid=(bsz, seq // tm),
        in_specs=[seq_spec(D_MODEL), pl.BlockSpec((None, 8, D_MODEL), lambda b, i: (b, 0, 0)),
                  full((D_MODEL, 4 * DN_WIDTH)), full((D_MODEL, LANES)),
                  full((CONV_WIDTH, 3 * DN_WIDTH)), full((8, LANES))],
        out_specs=[seq_spec(DN_WIDTH)] * 4 + [seq_spec(LANES)],
        out_shape=[wide] * 4 + [jax.ShapeDtypeStruct((bsz, seq, LANES), F32)],
        scratch_shapes=[pltpu.VMEM((3, tm + CONV_HALO, DN_WIDTH), F32)],
        compiler_params=_params("arbitrary", "arbitrary"),
        name="l1_in_proj",
    )(x, vec, w_main, w_ba, conv_w, hp)


def _unit_lower_inverses(lmats):
    n = lmats[0].shape[0]
    r = lax.broadcasted_iota(jnp.int32, (n, n), 0)
    c = lax.broadcasted_iota(jnp.int32, (n, n), 1)

    def below(size):
        return (r // (2 * size) == c // (2 * size)) & (r % (2 * size) >= size) & (c % (2 * size) < size)

    eye = jnp.where(r == c, 1.0, 0.0).astype(F32)
    first = below(1)
    invs = [eye - jnp.where(first, lm, 0.0) for lm in lmats]
    size = 2
    while size < n:
        mask = below(size)
        offs = [jnp.where(mask, lm, 0.0).astype(BF16) for lm in lmats]
        inv16 = [x.astype(BF16) for x in invs]
        xc = [_dot(x, o).astype(BF16) for x, o in zip(inv16, offs)]
        invs = [x - _dot(t, x16) for x, t, x16 in zip(invs, xc, inv16)]
        size *= 2
    return invs


def _delta_body(q_ref, k_ref, v_ref, gate_ref, bg_ref, gcr_ref, ng_ref, o_ref,
                state, p_s, n_s, qp_s, op_s, dec_s):
    i = pl.program_id(1)
    n_chunks = q_ref.shape[0] // CHUNK
    heads = range(DN_HEADS)
    head_cols = [slice(hd * DN_HEAD_DIM, (hd + 1) * DN_HEAD_DIM) for hd in heads]

    @pl.when(i == 0)
    def _():
        state[...] = jnp.zeros_like(state)

    r = lax.broadcasted_iota(jnp.int32, (CHUNK, CHUNK), 0)
    c = lax.broadcasted_iota(jnp.int32, (CHUNK, CHUNK), 1)
    causal = r >= c
    strict = r > c

    def prepare(step, carry):
        chunks = [step * PREP_CHUNKS + n for n in range(PREP_CHUNKS)]
        rows = [pl.ds(pl.multiple_of(ci * CHUNK, CHUNK), CHUNK) for ci in chunks]
        bg = [bg_ref[r, :] for r in rows]
        gcr_all = [gcr_ref[ci] for ci in chunks]
        probs = [(n, hd) for n in range(PREP_CHUNKS) for hd in heads]
        k16 = [k_ref[rows[n], head_cols[hd]] for n, hd in probs]
        q16 = [q_ref[rows[n], head_cols[hd]] for n, hd in probs]
        kf = [x.astype(F32) for x in k16]
        beta = [bg[n][:, hd:hd + 1] for n, hd in probs]
        gcc = [bg[n][:, DN_HEADS + hd:DN_HEADS + hd + 1] for n, hd in probs]
        gcr = [gcr_all[n][hd:hd + 1, :] for n, hd in probs]
        kb = [x * b for x, b in zip(kf, beta)]
        both = [_dot_nt(jnp.concatenate([a.astype(BF16), b], axis=0), x)
                for a, b, x in zip(kb, q16, k16)]
        decay = [jnp.exp(jnp.where(causal, a - b, NEG_INF)) for a, b in zip(gcc, gcr)]
        lmats = [jnp.where(strict, m[0:CHUNK] * d, 0.0) for m, d in zip(both, decay)]
        attn = [(m[CHUNK:2 * CHUNK] * d).astype(BF16) for m, d in zip(both, decay)]
        invs = _unit_lower_inverses(lmats)
        eg = [jnp.exp(x) for x in gcc]
        rhs = [jnp.concatenate([(v_ref[rows[n], head_cols[hd]].astype(F32) * b).astype(BF16),
                                (a * e).astype(BF16)], axis=1)
               for (n, hd), b, a, e in zip(probs, beta, kb, eg)]
        sol = [_dot(x.astype(BF16), y).astype(BF16) for x, y in zip(invs, rhs)]
        au = [_dot(a, s) for a, s in zip(attn, sol)]
        g_last = [x[CHUNK - 1:CHUNK, :] for x in gcc]
        kg = [(x * jnp.exp(gl - g)).astype(BF16) for x, gl, g in zip(kf, g_last, gcc)]
        kn = [_dot_tn(a, s) for a, s in zip(kg, sol)]
        for idx, (n, hd) in enumerate(probs):
            ci = chunks[n]
            n_s[ci, hd] = kn[idx][:, 0:DN_HEAD_DIM]
            p_s[ci, hd] = kn[idx][:, DN_HEAD_DIM:2 * DN_HEAD_DIM].astype(BF16)
            op_s[ci, hd] = au[idx][:, 0:DN_HEAD_DIM]
            qp_s[ci, hd] = (q16[idx].astype(F32) * eg[idx]
                            - au[idx][:, DN_HEAD_DIM:2 * DN_HEAD_DIM]).astype(BF16)
            dec_s[ci, hd] = jnp.broadcast_to(jnp.exp(g_last[idx]), (8, DN_HEAD_DIM))
        return carry

    def scan(ci, carry):
        rows = pl.ds(pl.multiple_of(ci * CHUNK, CHUNK), CHUNK)
        for hd, hc in zip(heads, head_cols):
            s = state[hd]
            s16 = s.astype(BF16)
            o = _dot(qp_s[ci, hd], s16) + op_s[ci, hd]
            state[hd] = s * dec_s[ci, hd][0:1, :] + n_s[ci, hd] - _dot(p_s[ci, hd], s16)
            ms = jnp.mean(o * o, axis=-1, keepdims=True)
            o = o * lax.rsqrt(ms + EPS) * ng_ref[0:1, :] * gate_ref[rows, hc].astype(F32)
            o_ref[rows, hc] = o.astype(BF16)
        return carry

    lax.fori_loop(0, n_chunks // PREP_CHUNKS, prepare, 0)
    lax.fori_loop(0, n_chunks, scan, 0, unroll=4)


def _delta(q, k, v, gate, bg, norm_g, block=512):
    bsz, seq, _ = q.shape
    n = seq // CHUNK
    nc = block // CHUNK
    gc_rows = bg[:, :, DN_HEADS:2 * DN_HEADS].reshape(bsz, n, CHUNK, DN_HEADS).transpose(0, 1, 3, 2)
    seq_spec = lambda width: pl.BlockSpec((None, block, width), lambda b, i: (b, i, 0))
    per_head = lambda rows, dtype: pltpu.VMEM((nc, DN_HEADS, rows, DN_HEAD_DIM), dtype)
    return pl.pallas_call(
        _delta_body,
        grid=(bsz, seq // block),
        in_specs=[seq_spec(DN_WIDTH)] * 4 + [seq_spec(LANES),
                  pl.BlockSpec((None, nc, DN_HEADS, CHUNK), lambda b, i: (b, i, 0, 0)),
                  pl.BlockSpec((1, DN_HEAD_DIM), lambda b, i: (0, 0))],
        out_specs=seq_spec(DN_WIDTH),
        out_shape=jax.ShapeDtypeStruct((bsz, seq, DN_WIDTH), BF16),
        scratch_shapes=[pltpu.VMEM((DN_HEADS, DN_HEAD_DIM, DN_HEAD_DIM), F32),
                        per_head(DN_HEAD_DIM, BF16), per_head(DN_HEAD_DIM, F32),
                        per_head(CHUNK, BF16), per_head(CHUNK, F32), per_head(8, F32)],
        compiler_params=_params("arbitrary", "arbitrary"),
        name="gated_delta",
    )(q, k, v, gate, bg, gc_rows, norm_g.reshape(1, DN_HEAD_DIM))


def _route(hf, rw):
    tm = hf.shape[0]
    lane = lax.broadcasted_iota(jnp.int32, (tm, LANES), 1)
    logits = jnp.where(lane < N_EXPERTS, _dot_split(hf, rw), NEG_INF)
    m1 = jnp.max(logits, axis=-1, keepdims=True)
    i1 = jnp.min(jnp.where(logits == m1, lane, LANES), axis=-1, keepdims=True)
    rest = jnp.where(lane == i1, NEG_INF, logits)
    m2 = jnp.max(rest, axis=-1, keepdims=True)
    i2 = jnp.min(jnp.where(rest == m2, lane, LANES), axis=-1, keepdims=True)
    e2 = jnp.exp(m2 - m1)
    w1 = 1.0 / (1.0 + e2)
    w2 = e2 / (1.0 + e2)
    combine = jnp.where(lane == i1, w1, 0.0) + jnp.where(lane == i2, w2, 0.0)
    chosen = jnp.where((lane == i1) | (lane == i2), 1.0, 0.0)
    return combine, chosen


def _out1_body(a_ref, x_ref, vec_ref, w_ref, rw_ref,
               x_out, h_out, cw_out, rank_out, cnt_out, running, *, tiles_per_block):
    i = pl.program_id(1)
    tm = x_ref.shape[0]

    @pl.when(i % tiles_per_block == 0)
    def _():
        running[...] = jnp.zeros_like(running)

    rows = [slice(n * ROUTE_ROWS, (n + 1) * ROUTE_ROWS) for n in range(tm // ROUTE_ROWS)]
    y = [_dot(a_ref[rs, :], w_ref[...]) for rs in rows]
    x2 = [_post_residual(x_ref[rs, :], yy, vec_ref[0:1, :], vec_ref[1:2, :]) for rs, yy in zip(rows, y)]
    hf = [_norm_mod(xx, vec_ref[2:3, :], vec_ref[3:4, :], vec_ref[4:5, :]) for xx in x2]
    routed = [_route(hh, rw_ref[...]) for hh in hf]
    r = lax.broadcasted_iota(jnp.int32, (ROUTE_ROWS, ROUTE_ROWS), 0)
    c = lax.broadcasted_iota(jnp.int32, (ROUTE_ROWS, ROUTE_ROWS), 1)
    before = jnp.where(c < r, 1.0, 0.0).astype(BF16)
    local = [_dot(before, chosen.astype(BF16)) for _, chosen in routed]
    offset = running[0:1, :]
    for rs, xx, hh, (combine, chosen), loc in zip(rows, x2, hf, routed, local):
        x_out[rs, :] = xx
        h_out[rs, :] = hh.astype(BF16)
        cw_out[rs, :] = combine
        rank_out[rs, :] = jnp.where(chosen > 0.0, loc + offset, -1.0)
        offset = offset + jnp.sum(chosen, axis=0, keepdims=True)
    running[...] = jnp.broadcast_to(offset, running.shape)
    cnt_out[...] = running[...]


def _out1(a, x, vec, w_out, router_w, tm=512, route_block=ROUTE_BLOCK):
    bsz, seq, _ = x.shape
    tiles_per_block = route_block // tm
    blocks_per_seq = seq // route_block
    rw = jnp.zeros((D_MODEL, LANES), F32).at[:, :N_EXPERTS].set(router_w)
    seq_spec = lambda width: pl.BlockSpec((None, tm, width), lambda b, i: (b, i, 0))
    return pl.pallas_call(
        functools.partial(_out1_body, tiles_per_block=tiles_per_block),
        grid=(bsz, seq // tm),
        in_specs=[seq_spec(DN_WIDTH), seq_spec(D_MODEL),
                  pl.BlockSpec((None, 8, D_MODEL), lambda b, i: (b, 0, 0)),
                  pl.BlockSpec((DN_WIDTH, D_MODEL), lambda b, i: (0, 0)),
                  pl.BlockSpec((D_MODEL, LANES), lambda b, i: (0, 0))],
        out_specs=[seq_spec(D_MODEL), seq_spec(D_MODEL), seq_spec(LANES), seq_spec(LANES),
                   pl.BlockSpec((None, 8, LANES),
                                lambda b, i: (b * blocks_per_seq + i // tiles_per_block, 0, 0))],
        out_shape=[jax.ShapeDtypeStruct((bsz, seq, D_MODEL), F32),
                   jax.ShapeDtypeStruct((bsz, seq, D_MODEL), BF16),
                   jax.ShapeDtypeStruct((bsz, seq, LANES), F32),
                   jax.ShapeDtypeStruct((bsz, seq, LANES), F32),
                   jax.ShapeDtypeStruct((bsz * blocks_per_seq, 8, LANES), F32)],
        scratch_shapes=[pltpu.VMEM((8, LANES), F32)],
        compiler_params=_params("arbitrary", "arbitrary"),
        name="l1_out_proj_route",
    )(a, x, vec, w_out.astype(BF16), rw)


def _moe_body(cnt_ref, h_ref, rrow_ref, wrow_ref, rcol_ref, x_ref, vec_ref, wg_ref, wu_ref, wd_ref,
              out_ref, xs, ys, wslot):
    b, e, f = pl.program_id(0), pl.program_id(1), pl.program_id(2)
    tb = h_ref.shape[0]

    @pl.when((e == 0) & (f == 0))
    def _():
        out_ref[...] = jnp.zeros_like(out_ref)

    count = cnt_ref[b * N_EXPERTS + e]
    padded = (count + SEGMENTS[-1] - 1) // SEGMENTS[-1] * SEGMENTS[-1]
    n_large = padded // SEGMENTS[0]

    def for_each_segment(fn):
        def large(t, carry):
            fn(pl.multiple_of(t * SEGMENTS[0], SEGMENTS[0]), SEGMENTS[0])
            return carry

        lax.fori_loop(0, n_large, large, 0)
        done = n_large * SEGMENTS[0]
        for size in SEGMENTS[1:]:
            take = ((padded - done) // size) > 0

            @pl.when(take)
            def _(done=done, size=size):
                fn(pl.multiple_of(done, SEGMENTS[-1]), size)

            done = done + jnp.where(take, size, 0)

    @pl.when(f == 0)
    def _():
        rrow = rrow_ref[pl.ds(e, 1), :]
        wrow = wrow_ref[pl.ds(e, 1), :]

        def pack(base, size):
            rows = pl.ds(base, size)
            slot = (base + lax.broadcasted_iota(jnp.int32, (size, 1), 0)).astype(F32)
            hit = rrow == slot
            xs[rows, :] = _dot(jnp.where(hit, 1.0, 0.0).astype(BF16), h_ref[...]).astype(BF16)
            wslot[rows, :] = jnp.sum(jnp.where(hit, wrow, 0.0), axis=-1, keepdims=True)
            ys[rows, :] = jnp.zeros((size, D_MODEL), F32)

        for_each_segment(pack)

    def expert(base, size):
        rows = pl.ds(base, size)
        x = xs[rows, :]
        act = _silu(_dot(x, wg_ref[...].astype(BF16))) * _dot(x, wu_ref[...].astype(BF16))
        ys[rows, :] += _dot(act.astype(BF16), wd_ref[...].astype(BF16))

    for_each_segment(expert)

    @pl.when(f == pl.num_programs(2) - 1)
    def _():
        lane = lax.broadcasted_iota(jnp.int32, (tb, LANES), 1)
        rcol = jnp.sum(jnp.where(lane == e, rcol_ref[...], 0.0), axis=-1, keepdims=True)

        def unpack(base, size):
            rows = pl.ds(base, size)
            y = (ys[rows, :] * wslot[rows, :]).astype(BF16)
            slot = (base + lax.broadcasted_iota(jnp.int32, (1, size), 1)).astype(F32)
            back = jnp.where(rcol == slot, 1.0, 0.0).astype(BF16)
            out_ref[...] += _dot(back, y)

        for_each_segment(unpack)

        @pl.when(e == pl.num_programs(1) - 1)
        def _():
            out_ref[...] = _post_residual(x_ref[...], out_ref[...], vec_ref[0:1, :], vec_ref[1:2, :])


def _moe(x, vec, h, combine, rank, counts, w_gate, w_up, w_down, route_block=ROUTE_BLOCK, tf=512):
    bsz, seq, _ = h.shape
    tokens = bsz * seq
    n_blocks = tokens // route_block
    per_block_rows = lambda t: (t.reshape(n_blocks, route_block, LANES)[:, :, :N_EXPERTS]
                                .transpose(0, 2, 1))
    cnt = counts[:, 0, :N_EXPERTS].astype(jnp.int32).reshape(n_blocks * N_EXPERTS)
    assert route_block % SEGMENTS[-1] == 0
    blocks_per_seq = seq // route_block
    once = pl.Buffered(1)
    per_block = lambda width: pl.BlockSpec((route_block, width), lambda b, e, f, cnt: (b, 0),
                                           pipeline_mode=once)
    per_expert = pl.BlockSpec((None, N_EXPERTS, route_block), lambda b, e, f, cnt: (b, 0, 0),
                              pipeline_mode=once)
    grid_spec = pltpu.PrefetchScalarGridSpec(
        num_scalar_prefetch=1,
        grid=(n_blocks, N_EXPERTS, FFN_DIM // tf),
        in_specs=[per_block(D_MODEL), per_expert, per_expert, per_block(LANES), per_block(D_MODEL),
                  pl.BlockSpec((None, 8, D_MODEL), lambda b, e, f, cnt: (b // blocks_per_seq, 0, 0)),
                  pl.BlockSpec((None, D_MODEL, tf), lambda b, e, f, cnt: (e, 0, f)),
                  pl.BlockSpec((None, D_MODEL, tf), lambda b, e, f, cnt: (e, 0, f)),
                  pl.BlockSpec((None, tf, D_MODEL), lambda b, e, f, cnt: (e, f, 0))],
        out_specs=per_block(D_MODEL),
        scratch_shapes=[pltpu.VMEM((route_block, D_MODEL), BF16), pltpu.VMEM((route_block, D_MODEL), F32),
                        pltpu.VMEM((route_block, 1), F32)])
    out = pl.pallas_call(
        _moe_body,
        grid_spec=grid_spec,
        out_shape=jax.ShapeDtypeStruct((tokens, D_MODEL), F32),
        compiler_params=_params("arbitrary", "arbitrary", "arbitrary"),
        name="l1_moe",
    )(cnt, h.reshape(tokens, D_MODEL), per_block_rows(rank), per_block_rows(combine),
      rank.reshape(tokens, LANES), x.reshape(tokens, D_MODEL), vec, w_gate, w_up, w_down)
    return out.reshape(bsz, seq, D_MODEL)


def kernel(x, c, positions, ada_w, ada_b, mix_pre_g, mix_post_g, ffn_pre_g, ffn_post_g, even_w_in, even_pool_w, even_pool_scale, even_w_out, even_ffn_w_gate, even_ffn_w_up, even_ffn_w_down, odd_w_in, odd_conv_w, odd_a_log, odd_dt_bias, odd_norm_g, odd_w_out, odd_router_w, odd_moe_w_gate, odd_moe_w_up, odd_moe_w_down):
    bsz = x.shape[0]
    mod = _adaln_mod(c, ada_w, ada_b)
    sh1, sc1, gt1, sh2, sc2, gt2 = (mod[:, :, n * D_MODEL:(n + 1) * D_MODEL] for n in range(6))

    vec = _vec_rows([mix_pre_g[0], sc1[0], sh1[0]], bsz)
    qs, ks, vs, o_b = _in0(x, positions, vec, even_w_in[0], even_pool_w[0], even_pool_scale[0])
    branches = [_banded_attention(q, k, v, d) for q, k, v, d in zip(qs, ks, vs, DILATIONS)]
    vec = _vec_rows([mix_post_g[0], gt1[0]], bsz)
    x = _out0([o for o, _ in branches], [l for _, l in branches], o_b, x, vec, even_w_out[0])
    vec = _vec_rows([ffn_pre_g[0], sc2[0], sh2[0], ffn_post_g[0], gt2[0]], bsz)
    x = _ffn(x, vec, even_ffn_w_gate[0], even_ffn_w_up[0], even_ffn_w_down[0])

    vec = _vec_rows([mix_pre_g[1], sc1[1], sh1[1]], bsz)
    q, k, v, gate, bg = _in1(x, vec, odd_w_in[0], odd_conv_w[0], odd_a_log[0], odd_dt_bias[0])
    o = _delta(q, k, v, gate, bg, odd_norm_g[0])
    vec = _vec_rows([mix_post_g[1], gt1[1], ffn_pre_g[1], sc2[1], sh2[1]], bsz)
    x, h, combine, rank, counts = _out1(o, x, vec, odd_w_out[0], odd_router_w[0])
    vec = _vec_rows([ffn_post_g[1], gt2[1]], bsz)
    return _moe(x, vec, h, combine, rank, counts,
                odd_moe_w_gate[0], odd_moe_w_up[0], odd_moe_w_down[0])
```

```python
import functools

import numpy as np
import jax
import jax.numpy as jnp
from jax import lax
from jax.experimental import pallas as pl
from jax.experimental.pallas import tpu as pltpu

F32 = jnp.float32
BF16 = jnp.bfloat16
HIGHEST = lax.Precision.HIGHEST

D_MODEL = 1024
DEPTH = 2
A_HEADS = 8
A_HEAD_DIM = 64
A_WIDTH = 512
DILATIONS = (1, 4, 16)
BAND = 128
ROT_DIM = 16
ROPE_THETA = 500000.0
POOL_WINDOWS = (2, 4, 8, 16)
POOL_GROUP_DIM = 128
POOL_WIDTH = 512
POOL_HALO = 16
DN_HEADS = 8
DN_HEAD_DIM = 128
DN_WIDTH = 1024
CONV_WIDTH = 4
CONV_HALO = 8
CHUNK = 64
PREP_CHUNKS = 4
FFN_DIM = 3584
N_EXPERTS = 8
EPS = 1e-6
LANES = 128
ROUTE_BLOCK = 2048
SEGMENTS = (512, 256, 128)
ROUTE_ROWS = 256
NEG_INF = float("-inf")

VMEM_LIMIT = 56 * 1024 * 1024


def _params(*sem):
    return pltpu.CompilerParams(dimension_semantics=sem, vmem_limit_bytes=VMEM_LIMIT)


def _dot(a, b, precision=None):
    return jnp.dot(a, b, preferred_element_type=F32, precision=precision)


def _dot_nt(a, b, precision=None):
    return lax.dot_general(a, b, (((1,), (1,)), ((), ())), preferred_element_type=F32,
                           precision=precision)


def _dot_tn(a, b, precision=None):
    return lax.dot_general(a, b, (((0,), (0,)), ((), ())), preferred_element_type=F32,
                           precision=precision)


def _silu(x):
    return x * jax.nn.sigmoid(x)


def _norm_mod(x, g, sc, sh):
    ms = jnp.mean(x * x, axis=-1, keepdims=True)
    return x * lax.rsqrt(ms + EPS) * g * (1.0 + sc) + sh


def _post_residual(x, y, g, gt):
    ms = jnp.mean(y * y, axis=-1, keepdims=True)
    return x + gt * (y * lax.rsqrt(ms + EPS) * g)


def _mod_body(c_ref, w_ref, b_ref, o_ref):
    c = c_ref[...]
    o_ref[...] = _dot(_silu(c), w_ref[...], precision=HIGHEST) + b_ref[...]


def _adaln_mod(c, ada_w, ada_b):
    bsz = c.shape[0]
    rows = 8
    c_pad = jnp.zeros((rows, D_MODEL), F32).at[:bsz].set(c)
    tn = 1536
    out = pl.pallas_call(
        _mod_body,
        grid=(DEPTH, 6 * D_MODEL // tn),
        in_specs=[pl.BlockSpec((rows, D_MODEL), lambda l, n: (0, 0)),
                  pl.BlockSpec((None, D_MODEL, tn), lambda l, n: (l, 0, n)),
                  pl.BlockSpec((None, 1, tn), lambda l, n: (l, 0, n))],
        out_specs=pl.BlockSpec((None, rows, tn), lambda l, n: (l, 0, n)),
        out_shape=jax.ShapeDtypeStruct((DEPTH, rows, 6 * D_MODEL), F32),
        compiler_params=_params("arbitrary", "arbitrary"),
        name="adaln_mod",
    )(c_pad, ada_w, ada_b.reshape(DEPTH, 1, 6 * D_MODEL))
    return out[:, :bsz]


def _vec_rows(rows, bsz):
    rows = [jnp.broadcast_to(r.astype(F32), (bsz, D_MODEL)) for r in rows]
    rows = rows + [jnp.zeros((bsz, D_MODEL), F32)] * (8 - len(rows))
    return jnp.stack(rows, axis=1)


def _rope(t, cosv, sin_lo, sin_hi):
    return (t * cosv + pltpu.roll(t, LANES - ROT_DIM // 2, axis=1) * sin_lo
            + pltpu.roll(t, ROT_DIM // 2, axis=1) * sin_hi)


def _store_folded(nat, refs):
    tm = nat.shape[1]
    for d, ref in zip(DILATIONS, refs):
        for r in range(d):
            rows = pl.ds(r, tm // d, stride=d) if d > 1 else slice(None)
            for t in range(A_WIDTH // LANES):
                col = r * A_WIDTH + t * LANES
                ref[:, col:col + LANES] = nat[t, rows, :].astype(BF16)


def _in0_body(x_ref, pos_ref, vec_ref, w_ref, freq_ref, exp_ref, plain_ref, pw_ref, ps_ref,
              q1_ref, q4_ref, q16_ref, k1_ref, k4_ref, k16_ref, v1_ref, v4_ref, v16_ref, ob_ref,
              ubuf, nat):
    i = pl.program_id(1)
    tm = x_ref.shape[0]

    @pl.when(i == 0)
    def _():
        ubuf[0:POOL_HALO, :] = jnp.zeros((POOL_HALO, POOL_WIDTH), F32)

    h = _norm_mod(x_ref[...], vec_ref[0:1, :], vec_ref[1:2, :], vec_ref[2:3, :]).astype(BF16)

    ang = freq_ref[...] * pos_ref[...].astype(F32)

    def spread(v, e):
        lead = v.astype(BF16).astype(F32)
        return _dot_tn(lead, e) + _dot_tn(v - lead, e)

    cos_t = jnp.cos(ang)
    sin_t = jnp.sin(ang)
    cosv = spread(cos_t, exp_ref[0]) + plain_ref[...]
    sin_lo = -spread(sin_t, exp_ref[1])
    sin_hi = spread(sin_t, exp_ref[2])

    pq = _dot(h, w_ref[:, 0:A_WIDTH])
    for t in range(A_WIDTH // LANES):
        cols = slice(t * LANES, (t + 1) * LANES)
        nat[t] = _rope(pq[:, cols], cosv, sin_lo, sin_hi) * (A_HEAD_DIM ** -0.5)
    _store_folded(nat, (q1_ref, q4_ref, q16_ref))
    pk = _dot(h, w_ref[:, A_WIDTH:2 * A_WIDTH])
    for t in range(A_WIDTH // LANES):
        cols = slice(t * LANES, (t + 1) * LANES)
        nat[t] = _rope(pk[:, cols], cosv, sin_lo, sin_hi)
    _store_folded(nat, (k1_ref, k4_ref, k16_ref))
    pv = _dot(h, w_ref[:, 2 * A_WIDTH:3 * A_WIDTH])
    for t in range(A_WIDTH // LANES):
        nat[t] = pv[:, t * LANES:(t + 1) * LANES]
    _store_folded(nat, (v1_ref, v4_ref, v16_ref))

    ubuf[POOL_HALO:POOL_HALO + tm, :] = _dot(h, w_ref[:, 3 * A_WIDTH:3 * A_WIDTH + POOL_WIDTH])
    tpos = i * tm + lax.broadcasted_iota(jnp.int32, (tm, 1), 0)
    for g, win in enumerate(POOL_WINDOWS):
        cols = slice(g * POOL_GROUP_DIM, (g + 1) * POOL_GROUP_DIM)
        cur = ubuf[POOL_HALO:POOL_HALO + tm, cols]
        acc = cur
        for j in range(1, win):
            acc = acc + ubuf[POOL_HALO - j:POOL_HALO - j + tm, cols]
        cnt = jnp.minimum(tpos + 1, win).astype(F32)
        pooled = acc / cnt - cur
        mixed = _dot(pooled.astype(BF16), pw_ref[g]) * ps_ref[0:1, cols]
        ob_ref[:, cols] = mixed.astype(BF16)
    ubuf[0:POOL_HALO, :] = ubuf[tm:tm + POOL_HALO, :]


def _rope_tables(tm):
    half = ROT_DIM // 2
    inv_freq = ROPE_THETA ** (-jnp.arange(0, ROT_DIM, 2, dtype=F32) / ROT_DIM)
    freq = jnp.broadcast_to(inv_freq[:, None], (half, tm))
    lane = np.arange(LANES) % A_HEAD_DIM
    first, second = lane < half, (lane >= half) & (lane < ROT_DIM)
    uses = lane[None, :] % half == np.arange(half)[:, None]
    expand = np.stack([uses & (first | second), uses & first, uses & second]).astype(np.float32)
    plain = (lane >= ROT_DIM).astype(np.float32).reshape(1, LANES)
    return freq, jnp.asarray(expand), jnp.asarray(plain)


def _in0(x, positions, vec, w_in, pool_w, pool_scale, tm=512):
    bsz, seq, _ = x.shape
    n_out = 3 * A_WIDTH + POOL_WIDTH
    freq, expand, plain = _rope_tables(tm)
    seq_spec = lambda width: pl.BlockSpec((None, tm, width), lambda b, i: (b, i, 0))
    full = lambda shape: pl.BlockSpec(shape, lambda b, i: (0,) * len(shape))
    folded_spec = lambda d: pl.BlockSpec((None, tm // d, d * A_WIDTH), lambda b, i: (b, i, 0))
    folded_sds = lambda d: jax.ShapeDtypeStruct((bsz, seq // d, d * A_WIDTH), BF16)
    outs = pl.pallas_call(
        _in0_body,
        grid=(bsz, seq // tm),
        in_specs=[seq_spec(D_MODEL), pl.BlockSpec((None, None, 1, tm), lambda b, i: (b, i, 0, 0)),
                  pl.BlockSpec((None, 8, D_MODEL), lambda b, i: (b, 0, 0)),
                  full((D_MODEL, n_out)), full(freq.shape), full(expand.shape), full(plain.shape),
                  full((len(POOL_WINDOWS), POOL_GROUP_DIM, POOL_GROUP_DIM)), full((1, POOL_WIDTH))],
        out_specs=[folded_spec(d) for d in DILATIONS] * 3 + [seq_spec(POOL_WIDTH)],
        out_shape=[folded_sds(d) for d in DILATIONS] * 3 + [folded_sds(1)],
        scratch_shapes=[pltpu.VMEM((tm + POOL_HALO, POOL_WIDTH), F32),
                        pltpu.VMEM((A_WIDTH // LANES, tm, LANES), F32)],
        compiler_params=_params("arbitrary", "arbitrary"),
        name="l0_in_proj",
    )(x, positions.reshape(bsz, seq // tm, 1, tm), vec, w_in.astype(BF16), freq, expand, plain,
      pool_w.astype(BF16), pool_scale.reshape(1, POOL_WIDTH))
    n = len(DILATIONS)
    return outs[0:n], outs[n:2 * n], outs[2 * n:3 * n], outs[3 * n]


def _attn_body(q_ref, kp_ref, kc_ref, vp_ref, vc_ref, o_ref, lse_ref):
    i = pl.program_id(2)
    n_bands = q_ref.shape[0] // BAND
    qi = lax.broadcasted_iota(jnp.int32, (BAND, BAND), 0)
    kj = lax.broadcasted_iota(jnp.int32, (BAND, BAND), 1)
    back = kj >= qi
    mask_cur = kj <= qi
    lane = lax.broadcasted_iota(jnp.int32, (BAND, LANES), 1)
    first = lane < A_HEAD_DIM
    pairs = range(A_WIDTH // LANES)
    cols = [slice(g * LANES, (g + 1) * LANES) for g in pairs]
    rows = [slice(s * BAND, (s + 1) * BAND) for s in range(n_bands)]
    probs = [(s, g, half) for s in range(n_bands) for g in pairs for half in range(2)]

    def prev_of(ref_prev, ref_cur, s, g):
        return ref_prev[:, cols[g]] if s == 0 else ref_cur[rows[s - 1], cols[g]]

    qh = []
    for s, g, half in probs:
        qp = q_ref[rows[s], cols[g]]
        qh.append(jnp.where(first if half == 0 else jnp.logical_not(first), qp, jnp.zeros_like(qp)))
    sp = [jnp.where(back & (i > 0) if s == 0 else back, _dot_nt(q, prev_of(kp_ref, kc_ref, s, g)), NEG_INF)
          for q, (s, g, _) in zip(qh, probs)]
    sc = [jnp.where(mask_cur, _dot_nt(q, kc_ref[rows[s], cols[g]]), NEG_INF)
          for q, (s, g, _) in zip(qh, probs)]
    m = [jnp.maximum(jnp.max(a, axis=-1, keepdims=True), jnp.max(b, axis=-1, keepdims=True))
         for a, b in zip(sp, sc)]
    pp = [jnp.exp(a - mm) for a, mm in zip(sp, m)]
    pc = [jnp.exp(a - mm) for a, mm in zip(sc, m)]
    den = [jnp.sum(a, axis=-1, keepdims=True) + jnp.sum(b, axis=-1, keepdims=True)
           for a, b in zip(pp, pc)]
    outs = [(_dot(a.astype(BF16), prev_of(vp_ref, vc_ref, s, g))
             + _dot(b.astype(BF16), vc_ref[rows[s], cols[g]])) / d
            for a, b, d, (s, g, _) in zip(pp, pc, den, probs)]
    per_band = 2 * len(pairs)
    for s in range(n_bands):
        lse_tile = jnp.zeros((BAND, LANES), F32)
        for idx in range(per_band):
            n = s * per_band + idx
            lse_tile = jnp.where(lane == idx, m[n] + jnp.log(den[n]), lse_tile)
        lse_ref[rows[s], :] = lse_tile
        for g in pairs:
            n = s * per_band + 2 * g
            o_ref[rows[s], cols[g]] = jnp.where(first, outs[n], outs[n + 1]).astype(BF16)


def _banded_attention(q, k, v, dilation, bands=4):
    bsz, n_sub, _ = q.shape
    rows = bands * BAND
    cur = pl.BlockSpec((None, rows, A_WIDTH), lambda b, r, i: (b, i, r))
    prev = pl.BlockSpec((None, BAND, A_WIDTH), lambda b, r, i: (b, jnp.maximum(i * bands - 1, 0), r))
    o, lse = pl.pallas_call(
        _attn_body,
        grid=(bsz, dilation, n_sub // rows),
        in_specs=[cur, prev, cur, prev, cur],
        out_specs=[cur, pl.BlockSpec((None, rows, LANES), lambda b, r, i: (b, i, r))],
        out_shape=[jax.ShapeDtypeStruct((bsz, n_sub, dilation * A_WIDTH), BF16),
                   jax.ShapeDtypeStruct((bsz, n_sub, dilation * LANES), F32)],
        compiler_params=_params("arbitrary", "arbitrary", "arbitrary"),
        name=f"dilated_attn_d{dilation}",
    )(q, k, k, v, v)
    return o, lse


def _out0_body(o1_ref, o4_ref, o16_ref, l1_ref, l4_ref, l16_ref, ob_ref, x_ref, vec_ref, w_ref,
               out_ref, o_nat, l_nat):
    tm = x_ref.shape[0]
    for slot, (d, o_ref, l_ref) in enumerate(zip(DILATIONS, (o1_ref, o4_ref, o16_ref),
                                                  (l1_ref, l4_ref, l16_ref))):
        for r in range(d):
            rows = pl.ds(r, tm // d, stride=d) if d > 1 else slice(None)
            for t in range(A_WIDTH // LANES):
                col = r * A_WIDTH + t * LANES
                o_nat[slot, t, rows, :] = o_ref[:, col:col + LANES].astype(F32)
            l_nat[slot, rows, :] = l_ref[:, r * LANES:(r + 1) * LANES]
    la, lb, lc = l_nat[0], l_nat[1], l_nat[2]
    m = jnp.maximum(jnp.maximum(la, lb), lc)
    ea, eb, ec = jnp.exp(la - m), jnp.exp(lb - m), jnp.exp(lc - m)
    tot = ea + eb + ec
    weights = (ea / tot, eb / tot, ec / tot)
    lane = lax.broadcasted_iota(jnp.int32, (tm, LANES), 1)
    first = lane < A_HEAD_DIM
    pieces = []
    for g in range(A_WIDTH // LANES):
        cols = slice(g * LANES, (g + 1) * LANES)
        acc = jnp.zeros((tm, LANES), F32)
        for slot, wt in enumerate(weights):
            w_pair = jnp.where(first,
                               jnp.broadcast_to(wt[:, 2 * g:2 * g + 1], (tm, LANES)),
                               jnp.broadcast_to(wt[:, 2 * g + 1:2 * g + 2], (tm, LANES)))
            acc = acc + w_pair * o_nat[slot, g]
        pieces.append(acc.astype(BF16))
    o_a = jnp.concatenate(pieces, axis=-1)
    y = _dot(o_a, w_ref[0:A_WIDTH, :]) + _dot(ob_ref[...], w_ref[A_WIDTH:A_WIDTH + POOL_WIDTH, :])
    out_ref[...] = _post_residual(x_ref[...], y, vec_ref[0:1, :], vec_ref[1:2, :])


def _out0(os, lses, ob, x, vec, w_out, tm=512):
    bsz, seq, _ = x.shape
    seq_spec = lambda width: pl.BlockSpec((None, tm, width), lambda b, i: (b, i, 0))
    folded = lambda width: [pl.BlockSpec((None, tm // d, d * width), lambda b, i: (b, i, 0))
                            for d in DILATIONS]
    n = len(DILATIONS)
    return pl.pallas_call(
        _out0_body,
        grid=(bsz, seq // tm),
        in_specs=folded(A_WIDTH) + folded(LANES) + [seq_spec(POOL_WIDTH),
                  seq_spec(D_MODEL), pl.BlockSpec((None, 8, D_MODEL), lambda b, i: (b, 0, 0)),
                  pl.BlockSpec((A_WIDTH + POOL_WIDTH, D_MODEL), lambda b, i: (0, 0))],
        out_specs=seq_spec(D_MODEL),
        out_shape=jax.ShapeDtypeStruct((bsz, seq, D_MODEL), F32),
        scratch_shapes=[pltpu.VMEM((n, A_WIDTH // LANES, tm, LANES), F32),
                        pltpu.VMEM((n, tm, LANES), F32)],
        compiler_params=_params("arbitrary", "arbitrary"),
        name="l0_out_proj",
    )(*os, *lses, ob, x, vec, w_out.astype(BF16))


def _ffn_body(x_ref, vec_ref, wg_ref, wu_ref, wd_ref, out_ref, hbuf, acc):
    f = pl.program_id(1)

    @pl.when(f == 0)
    def _():
        hbuf[...] = _norm_mod(x_ref[...], vec_ref[0:1, :], vec_ref[1:2, :],
                              vec_ref[2:3, :]).astype(BF16)
        acc[...] = jnp.zeros_like(acc)

    h = hbuf[...]
    act = _silu(_dot(h, wg_ref[...].astype(BF16))) * _dot(h, wu_ref[...].astype(BF16))
    acc[...] += _dot(act.astype(BF16), wd_ref[...].astype(BF16))

    @pl.when(f == pl.num_programs(1) - 1)
    def _():
        out_ref[...] = _post_residual(x_ref[...], acc[...], vec_ref[3:4, :], vec_ref[4:5, :])


def _ffn(x, vec, w_gate, w_up, w_down, tm=1024, tf=512):
    bsz, seq, _ = x.shape
    tiles_per_seq = seq // tm
    xt = x.reshape(bsz * seq, D_MODEL)
    row = pl.BlockSpec((tm, D_MODEL), lambda i, f: (i, 0))
    out = pl.pallas_call(
        _ffn_body,
        grid=(bsz * seq // tm, FFN_DIM // tf),
        in_specs=[row, pl.BlockSpec((None, 8, D_MODEL), lambda i, f: (i // tiles_per_seq, 0, 0)),
                  pl.BlockSpec((D_MODEL, tf), lambda i, f: (0, f)),
                  pl.BlockSpec((D_MODEL, tf), lambda i, f: (0, f)),
                  pl.BlockSpec((tf, D_MODEL), lambda i, f: (f, 0))],
        out_specs=row,
        out_shape=jax.ShapeDtypeStruct((bsz * seq, D_MODEL), F32),
        scratch_shapes=[pltpu.VMEM((tm, D_MODEL), BF16), pltpu.VMEM((tm, D_MODEL), F32)],
        compiler_params=_params("arbitrary", "arbitrary"),
        name="l0_swiglu",
    )(xt, vec, w_gate, w_up, w_down)
    return out.reshape(bsz, seq, D_MODEL)


def _split3(a):
    a1 = a.astype(BF16)
    r1 = a - a1.astype(F32)
    a2 = r1.astype(BF16)
    a3 = (r1 - a2.astype(F32)).astype(BF16)
    return a1, a2, a3


def _dot_split(a, b):
    a_hi = a.astype(BF16)
    a_lo = (a - a_hi.astype(F32)).astype(BF16)
    b_hi = b.astype(BF16)
    b_lo = (b - b_hi.astype(F32)).astype(BF16)
    return _dot(a_hi, b_hi) + (_dot(a_hi, b_lo) + _dot(a_lo, b_hi))


def _in1_body(x_ref, vec_ref, w_ref, wba_ref, cw_ref, hp_ref,
              q_ref, k_ref, v_ref, gate_ref, bg_ref, cbuf):
    i = pl.program_id(1)
    tm = x_ref.shape[0]

    @pl.when(i == 0)
    def _():
        cbuf[:, 0:CONV_HALO, :] = jnp.zeros((3, CONV_HALO, DN_WIDTH), F32)

    hf = _norm_mod(x_ref[...], vec_ref[0:1, :], vec_ref[1:2, :], vec_ref[2:3, :])
    h = hf.astype(BF16)

    for idx, dst in enumerate((q_ref, k_ref, v_ref)):
        cols = slice(idx * DN_WIDTH, (idx + 1) * DN_WIDTH)
        cbuf[idx, CONV_HALO:CONV_HALO + tm, :] = _dot(h, w_ref[:, cols])
        y = jnp.zeros((tm, DN_WIDTH), F32)
        for j in range(CONV_WIDTH):
            off = CONV_HALO - (CONV_WIDTH - 1) + j
            y = y + cw_ref[j:j + 1, cols] * cbuf[idx, off:off + tm, :]
        y = _silu(y)
        cbuf[idx, 0:CONV_HALO, :] = cbuf[idx, tm:tm + CONV_HALO, :]
        if idx < 2:
            scale = DN_HEAD_DIM ** -0.5 if idx == 0 else 1.0
            for hd in range(DN_HEADS):
                hc = slice(hd * DN_HEAD_DIM, (hd + 1) * DN_HEAD_DIM)
                blk = y[:, hc]
                ss = jnp.sum(blk * blk, axis=-1, keepdims=True)
                dst[:, hc] = (blk * (lax.rsqrt(ss + EPS) * scale)).astype(BF16)
        else:
            dst[...] = y.astype(BF16)

    gate_ref[...] = _silu(_dot(h, w_ref[:, 3 * DN_WIDTH:4 * DN_WIDTH])).astype(BF16)

    ba = _dot_split(hf, wba_ref[...])
    beta = jax.nn.sigmoid(ba)
    z = ba + hp_ref[1:2, :]
    softplus = jnp.maximum(z, 0.0) + jnp.log1p(jnp.exp(-jnp.abs(z)))
    g = -jnp.exp(hp_ref[0:1, :]) * softplus
    r = lax.broadcasted_iota(jnp.int32, (tm, tm), 0)
    c = lax.broadcasted_iota(jnp.int32, (tm, tm), 1)
    tri = jnp.where((r // CHUNK == c // CHUNK) & (c <= r), 1.0, 0.0).astype(BF16)
    g1, g2, g3 = _split3(g)
    gc = _dot(tri, g1) + _dot(tri, g2) + _dot(tri, g3)
    lane = lax.broadcasted_iota(jnp.int32, (tm, LANES), 1)
    bg_ref[...] = jnp.where(lane < DN_HEADS, beta, gc)


def _in1(x, vec, w_in, conv_w, a_log, dt_bias, tm=256):
    bsz, seq, _ = x.shape
    w_main = w_in[:, :4 * DN_WIDTH].astype(BF16)
    w_ba = jnp.zeros((D_MODEL, LANES), F32).at[:, :2 * DN_HEADS].set(w_in[:, 4 * DN_WIDTH:])
    hp = jnp.zeros((8, LANES), F32)
    hp = hp.at[0, DN_HEADS:2 * DN_HEADS].set(a_log).at[1, DN_HEADS:2 * DN_HEADS].set(dt_bias)
    seq_spec = lambda width: pl.BlockSpec((None, tm, width), lambda b, i: (b, i, 0))
    full = lambda shape: pl.BlockSpec(shape, lambda b, i: (0,) * len(shape))
    wide = jax.ShapeDtypeStruct((bsz, seq, DN_WIDTH), BF16)
    return pl.pallas_call(
        _in1_body,
        grid=(bsz, seq // tm),
        in_specs=[seq_spec(D_MODEL), pl.BlockSpec((None, 8, D_MODEL), lambda b, i: (b, 0, 0)),
                  full((D_MODEL, 4 * DN_WIDTH)), full((D_MODEL, LANES)),
                  full((CONV_WIDTH, 3 * DN_WIDTH)), full((8, LANES))],
        out_specs=[seq_spec(DN_WIDTH)] * 4 + [seq_spec(LANES)],
        out_shape=[wide] * 4 + [jax.ShapeDtypeStruct((bsz, seq, LANES), F32)],
        scratch_shapes=[pltpu.VMEM((3, tm + CONV_HALO, DN_WIDTH), F32)],
        compiler_params=_params("arbitrary", "arbitrary"),
        name="l1_in_proj",
    )(x, vec, w_main, w_ba, conv_w, hp)


def _unit_lower_inverses(lmats):
    n = lmats[0].shape[0]
    r = lax.broadcasted_iota(jnp.int32, (n, n), 0)
    c = lax.broadcasted_iota(jnp.int32, (n, n), 1)

    def below(size):
        return (r // (2 * size) == c // (2 * size)) & (r % (2 * size) >= size) & (c % (2 * size) < size)

    eye = jnp.where(r == c, 1.0, 0.0).astype(F32)
    first = below(1)
    invs = [eye - jnp.where(first, lm, 0.0) for lm in lmats]
    size = 2
    while size < n:
        mask = below(size)
        offs = [jnp.where(mask, lm, 0.0).astype(BF16) for lm in lmats]
        inv16 = [x.astype(BF16) for x in invs]
        xc = [_dot(x, o).astype(BF16) for x, o in zip(inv16, offs)]
        invs = [x - _dot(t, x16) for x, t, x16 in zip(invs, xc, inv16)]
        size *= 2
    return invs


def _delta_body(q_ref, k_ref, v_ref, gate_ref, bg_ref, gcr_ref, ng_ref, wa_ref, wb_ref, wc_ref,
                o_ref, wa_out, wb_out, wc_out, state, p_s, n_s, qp_s, op_s, dec_s):
    for src, dst in ((wa_ref, wa_out), (wb_ref, wb_out), (wc_ref, wc_out)):
        dst[...] = src[...].astype(BF16)

    i = pl.program_id(1)
    n_chunks = q_ref.shape[0] // CHUNK
    heads = range(DN_HEADS)
    head_cols = [slice(hd * DN_HEAD_DIM, (hd + 1) * DN_HEAD_DIM) for hd in heads]

    @pl.when(i == 0)
    def _():
        state[...] = jnp.zeros_like(state)

    r = lax.broadcasted_iota(jnp.int32, (CHUNK, CHUNK), 0)
    c = lax.broadcasted_iota(jnp.int32, (CHUNK, CHUNK), 1)
    causal = r >= c
    strict = r > c

    def prepare(step, carry):
        chunks = [step * PREP_CHUNKS + n for n in range(PREP_CHUNKS)]
        rows = [pl.ds(pl.multiple_of(ci * CHUNK, CHUNK), CHUNK) for ci in chunks]
        bg = [bg_ref[r, :] for r in rows]
        gcr_all = [gcr_ref[ci] for ci in chunks]
        probs = [(n, hd) for n in range(PREP_CHUNKS) for hd in heads]
        k16 = [k_ref[rows[n], head_cols[hd]] for n, hd in probs]
        q16 = [q_ref[rows[n], head_cols[hd]] for n, hd in probs]
        kf = [x.astype(F32) for x in k16]
        beta = [bg[n][:, hd:hd + 1] for n, hd in probs]
        gcc = [bg[n][:, DN_HEADS + hd:DN_HEADS + hd + 1] for n, hd in probs]
        gcr = [gcr_all[n][hd:hd + 1, :] for n, hd in probs]
        kb = [x * b for x, b in zip(kf, beta)]
        both = [_dot_nt(jnp.concatenate([a.astype(BF16), b], axis=0), x)
                for a, b, x in zip(kb, q16, k16)]
        decay = [jnp.exp(jnp.where(causal, a - b, NEG_INF)) for a, b in zip(gcc, gcr)]
        lmats = [jnp.where(strict, m[0:CHUNK] * d, 0.0) for m, d in zip(both, decay)]
        attn = [(m[CHUNK:2 * CHUNK] * d).astype(BF16) for m, d in zip(both, decay)]
        invs = _unit_lower_inverses(lmats)
        eg = [jnp.exp(x) for x in gcc]
        rhs = [jnp.concatenate([(v_ref[rows[n], head_cols[hd]].astype(F32) * b).astype(BF16),
                                (a * e).astype(BF16)], axis=1)
               for (n, hd), b, a, e in zip(probs, beta, kb, eg)]
        sol = [_dot(x.astype(BF16), y).astype(BF16) for x, y in zip(invs, rhs)]
        au = [_dot(a, s) for a, s in zip(attn, sol)]
        g_last = [x[CHUNK - 1:CHUNK, :] for x in gcc]
        kg = [(x * jnp.exp(gl - g)).astype(BF16) for x, gl, g in zip(kf, g_last, gcc)]
        kn = [_dot_tn(a, s) for a, s in zip(kg, sol)]
        for idx, (n, hd) in enumerate(probs):
            ci = chunks[n]
            n_s[ci, hd] = kn[idx][:, 0:DN_HEAD_DIM]
            p_s[ci, hd] = kn[idx][:, DN_HEAD_DIM:2 * DN_HEAD_DIM].astype(BF16)
            op_s[ci, hd] = au[idx][:, 0:DN_HEAD_DIM]
            qp_s[ci, hd] = (q16[idx].astype(F32) * eg[idx]
                            - au[idx][:, DN_HEAD_DIM:2 * DN_HEAD_DIM]).astype(BF16)
            dec_s[ci, hd] = jnp.broadcast_to(jnp.exp(g_last[idx]), (8, DN_HEAD_DIM))
        return carry

    def scan(ci, carry):
        rows = pl.ds(pl.multiple_of(ci * CHUNK, CHUNK), CHUNK)
        for hd, hc in zip(heads, head_cols):
            s = state[hd]
            s16 = s.astype(BF16)
            o = _dot(qp_s[ci, hd], s16) + op_s[ci, hd]
            state[hd] = s * dec_s[ci, hd][0:1, :] + n_s[ci, hd] - _dot(p_s[ci, hd], s16)
            ms = jnp.mean(o * o, axis=-1, keepdims=True)
            o = o * lax.rsqrt(ms + EPS) * ng_ref[0:1, :] * gate_ref[rows, hc].astype(F32)
            o_ref[rows, hc] = o.astype(BF16)
        return carry

    lax.fori_loop(0, n_chunks // PREP_CHUNKS, prepare, 0)
    lax.fori_loop(0, n_chunks, scan, 0, unroll=4)


def _delta(q, k, v, gate, bg, norm_g, to_narrow, block=256):
    bsz, seq, _ = q.shape
    n = seq // CHUNK
    nc = block // CHUNK
    steps = bsz * (seq // block)
    gc_rows = bg[:, :, DN_HEADS:2 * DN_HEADS].reshape(bsz, n, CHUNK, DN_HEADS).transpose(0, 1, 3, 2)
    seq_spec = lambda width: pl.BlockSpec((None, block, width), lambda b, i: (b, i, 0))
    per_head = lambda rows, dtype: pltpu.VMEM((nc, DN_HEADS, rows, DN_HEAD_DIM), dtype)
    flat = [w.reshape(-1, D_MODEL) for w in to_narrow]
    slab = flat[0].shape[0] // steps
    assert all(w.shape[0] == slab * steps for w in flat) and slab % 16 == 0
    slab_spec = pl.BlockSpec((slab, D_MODEL), lambda b, i: (b * (seq // block) + i, 0))
    outs = pl.pallas_call(
        _delta_body,
        grid=(bsz, seq // block),
        in_specs=[seq_spec(DN_WIDTH)] * 4 + [seq_spec(LANES),
                  pl.BlockSpec((None, nc, DN_HEADS, CHUNK), lambda b, i: (b, i, 0, 0)),
                  pl.BlockSpec((1, DN_HEAD_DIM), lambda b, i: (0, 0))] + [slab_spec] * 3,
        out_specs=[seq_spec(DN_WIDTH)] + [slab_spec] * 3,
        out_shape=[jax.ShapeDtypeStruct((bsz, seq, DN_WIDTH), BF16)]
                  + [jax.ShapeDtypeStruct(w.shape, BF16) for w in flat],
        scratch_shapes=[pltpu.VMEM((DN_HEADS, DN_HEAD_DIM, DN_HEAD_DIM), F32),
                        per_head(DN_HEAD_DIM, BF16), per_head(DN_HEAD_DIM, F32),
                        per_head(CHUNK, BF16), per_head(CHUNK, F32), per_head(8, F32)],
        compiler_params=_params("arbitrary", "arbitrary"),
        name="gated_delta",
    )(q, k, v, gate, bg, gc_rows, norm_g.reshape(1, DN_HEAD_DIM), *flat)
    return outs[0], [w16.reshape(w.shape) for w16, w in zip(outs[1:], to_narrow)]


def _route(hf, rw):
    tm = hf.shape[0]
    lane = lax.broadcasted_iota(jnp.int32, (tm, LANES), 1)
    logits = jnp.where(lane < N_EXPERTS, _dot_split(hf, rw), NEG_INF)
    m1 = jnp.max(logits, axis=-1, keepdims=True)
    i1 = jnp.min(jnp.where(logits == m1, lane, LANES), axis=-1, keepdims=True)
    rest = jnp.where(lane == i1, NEG_INF, logits)
    m2 = jnp.max(rest, axis=-1, keepdims=True)
    i2 = jnp.min(jnp.where(rest == m2, lane, LANES), axis=-1, keepdims=True)
    e2 = jnp.exp(m2 - m1)
    w1 = 1.0 / (1.0 + e2)
    w2 = e2 / (1.0 + e2)
    combine = jnp.where(lane == i1, w1, 0.0) + jnp.where(lane == i2, w2, 0.0)
    chosen = jnp.where((lane == i1) | (lane == i2), 1.0, 0.0)
    return combine, chosen


def _out1_body(a_ref, x_ref, vec_ref, w_ref, rw_ref,
               x_out, h_out, cw_out, rank_out, cnt_out, running, *, tiles_per_block):
    i = pl.program_id(1)
    tm = x_ref.shape[0]

    @pl.when(i % tiles_per_block == 0)
    def _():
        running[...] = jnp.zeros_like(running)

    rows = [slice(n * ROUTE_ROWS, (n + 1) * ROUTE_ROWS) for n in range(tm // ROUTE_ROWS)]
    y = [_dot(a_ref[rs, :], w_ref[...]) for rs in rows]
    x2 = [_post_residual(x_ref[rs, :], yy, vec_ref[0:1, :], vec_ref[1:2, :]) for rs, yy in zip(rows, y)]
    hf = [_norm_mod(xx, vec_ref[2:3, :], vec_ref[3:4, :], vec_ref[4:5, :]) for xx in x2]
    routed = [_route(hh, rw_ref[...]) for hh in hf]
    r = lax.broadcasted_iota(jnp.int32, (ROUTE_ROWS, ROUTE_ROWS), 0)
    c = lax.broadcasted_iota(jnp.int32, (ROUTE_ROWS, ROUTE_ROWS), 1)
    before = jnp.where(c < r, 1.0, 0.0).astype(BF16)
    local = [_dot(before, chosen.astype(BF16)) for _, chosen in routed]
    offset = running[0:1, :]
    for rs, xx, hh, (combine, chosen), loc in zip(rows, x2, hf, routed, local):
        x_out[rs, :] = xx
        h_out[rs, :] = hh.astype(BF16)
        cw_out[rs, :] = combine
        rank_out[rs, :] = jnp.where(chosen > 0.0, loc + offset, -1.0)
        offset = offset + jnp.sum(chosen, axis=0, keepdims=True)
    running[...] = jnp.broadcast_to(offset, running.shape)
    cnt_out[...] = running[...]


def _out1(a, x, vec, w_out, router_w, tm=512, route_block=ROUTE_BLOCK):
    bsz, seq, _ = x.shape
    tiles_per_block = route_block // tm
    blocks_per_seq = seq // route_block
    rw = jnp.zeros((D_MODEL, LANES), F32).at[:, :N_EXPERTS].set(router_w)
    seq_spec = lambda width: pl.BlockSpec((None, tm, width), lambda b, i: (b, i, 0))
    return pl.pallas_call(
        functools.partial(_out1_body, tiles_per_block=tiles_per_block),
        grid=(bsz, seq // tm),
        in_specs=[seq_spec(DN_WIDTH), seq_spec(D_MODEL),
                  pl.BlockSpec((None, 8, D_MODEL), lambda b, i: (b, 0, 0)),
                  pl.BlockSpec((DN_WIDTH, D_MODEL), lambda b, i: (0, 0)),
                  pl.BlockSpec((D_MODEL, LANES), lambda b, i: (0, 0))],
        out_specs=[seq_spec(D_MODEL), seq_spec(D_MODEL), seq_spec(LANES), seq_spec(LANES),
                   pl.BlockSpec((None, 8, LANES),
                                lambda b, i: (b * blocks_per_seq + i // tiles_per_block, 0, 0))],
        out_shape=[jax.ShapeDtypeStruct((bsz, seq, D_MODEL), F32),
                   jax.ShapeDtypeStruct((bsz, seq, D_MODEL), BF16),
                   jax.ShapeDtypeStruct((bsz, seq, LANES), F32),
                   jax.ShapeDtypeStruct((bsz, seq, LANES), F32),
                   jax.ShapeDtypeStruct((bsz * blocks_per_seq, 8, LANES), F32)],
        scratch_shapes=[pltpu.VMEM((8, LANES), F32)],
        compiler_params=_params("arbitrary", "arbitrary"),
        name="l1_out_proj_route",
    )(a, x, vec, w_out.astype(BF16), rw)


def _moe_body(cnt_ref, h_ref, rrow_ref, wrow_ref, rcol_ref, x_ref, vec_ref, wg_ref, wu_ref, wd_ref,
              out_ref, xs, ys, wslot):
    b, e, f = pl.program_id(0), pl.program_id(1), pl.program_id(2)
    tb = h_ref.shape[0]

    @pl.when((e == 0) & (f == 0))
    def _():
        out_ref[...] = jnp.zeros_like(out_ref)

    count = cnt_ref[b * N_EXPERTS + e]
    padded = (count + SEGMENTS[-1] - 1) // SEGMENTS[-1] * SEGMENTS[-1]
    n_large = padded // SEGMENTS[0]

    def for_each_segment(fn):
        def large(t, carry):
            fn(pl.multiple_of(t * SEGMENTS[0], SEGMENTS[0]), SEGMENTS[0])
            return carry

        lax.fori_loop(0, n_large, large, 0)
        done = n_large * SEGMENTS[0]
        for size in SEGMENTS[1:]:
            take = ((padded - done) // size) > 0

            @pl.when(take)
            def _(done=done, size=size):
                fn(pl.multiple_of(done, SEGMENTS[-1]), size)

            done = done + jnp.where(take, size, 0)

    @pl.when(f == 0)
    def _():
        rrow = rrow_ref[pl.ds(e, 1), :]
        wrow = wrow_ref[pl.ds(e, 1), :]

        def pack(base, size):
            rows = pl.ds(base, size)
            slot = (base + lax.broadcasted_iota(jnp.int32, (size, 1), 0)).astype(F32)
            hit = rrow == slot
            xs[rows, :] = _dot(jnp.where(hit, 1.0, 0.0).astype(BF16), h_ref[...]).astype(BF16)
            wslot[rows, :] = jnp.sum(jnp.where(hit, wrow, 0.0), axis=-1, keepdims=True)
            ys[rows, :] = jnp.zeros((size, D_MODEL), F32)

        for_each_segment(pack)

    def expert(base, size):
        rows = pl.ds(base, size)
        x = xs[rows, :]
        act = _silu(_dot(x, wg_ref[...])) * _dot(x, wu_ref[...])
        ys[rows, :] += _dot(act.astype(BF16), wd_ref[...])

    for_each_segment(expert)

    @pl.when(f == pl.num_programs(2) - 1)
    def _():
        lane = lax.broadcasted_iota(jnp.int32, (tb, LANES), 1)
        rcol = jnp.sum(jnp.where(lane == e, rcol_ref[...], 0.0), axis=-1, keepdims=True)

        def unpack(base, size):
            rows = pl.ds(base, size)
            y = (ys[rows, :] * wslot[rows, :]).astype(BF16)
            slot = (base + lax.broadcasted_iota(jnp.int32, (1, size), 1)).astype(F32)
            back = jnp.where(rcol == slot, 1.0, 0.0).astype(BF16)
            out_ref[...] += _dot(back, y)

        for_each_segment(unpack)

        @pl.when(e == pl.num_programs(1) - 1)
        def _():
            out_ref[...] = _post_residual(x_ref[...], out_ref[...], vec_ref[0:1, :], vec_ref[1:2, :])


def _moe(x, vec, h, combine, rank, counts, w_gate, w_up, w_down, route_block=ROUTE_BLOCK, tf=512):
    bsz, seq, _ = h.shape
    tokens = bsz * seq
    n_blocks = tokens // route_block
    per_block_rows = lambda t: (t.reshape(n_blocks, route_block, LANES)[:, :, :N_EXPERTS]
                                .transpose(0, 2, 1))
    cnt = counts[:, 0, :N_EXPERTS].astype(jnp.int32).reshape(n_blocks * N_EXPERTS)
    assert route_block % SEGMENTS[-1] == 0
    blocks_per_seq = seq // route_block
    once = pl.Buffered(1)
    per_block = lambda width: pl.BlockSpec((route_block, width), lambda b, e, f, cnt: (b, 0),
                                           pipeline_mode=once)
    per_expert = pl.BlockSpec((None, N_EXPERTS, route_block), lambda b, e, f, cnt: (b, 0, 0),
                              pipeline_mode=once)
    grid_spec = pltpu.PrefetchScalarGridSpec(
        num_scalar_prefetch=1,
        grid=(n_blocks, N_EXPERTS, FFN_DIM // tf),
        in_specs=[per_block(D_MODEL), per_expert, per_expert, per_block(LANES), per_block(D_MODEL),
                  pl.BlockSpec((None, 8, D_MODEL), lambda b, e, f, cnt: (b // blocks_per_seq, 0, 0)),
                  pl.BlockSpec((None, D_MODEL, tf), lambda b, e, f, cnt: (e, 0, f)),
                  pl.BlockSpec((None, D_MODEL, tf), lambda b, e, f, cnt: (e, 0, f)),
                  pl.BlockSpec((None, tf, D_MODEL), lambda b, e, f, cnt: (e, f, 0))],
        out_specs=per_block(D_MODEL),
        scratch_shapes=[pltpu.VMEM((route_block, D_MODEL), BF16), pltpu.VMEM((route_block, D_MODEL), F32),
                        pltpu.VMEM((route_block, 1), F32)])
    out = pl.pallas_call(
        _moe_body,
        grid_spec=grid_spec,
        out_shape=jax.ShapeDtypeStruct((tokens, D_MODEL), F32),
        compiler_params=_params("arbitrary", "arbitrary", "arbitrary"),
        name="l1_moe",
    )(cnt, h.reshape(tokens, D_MODEL), per_block_rows(rank), per_block_rows(combine),
      rank.reshape(tokens, LANES), x.reshape(tokens, D_MODEL), vec, w_gate, w_up, w_down)
    return out.reshape(bsz, seq, D_MODEL)


def kernel(x, c, positions, ada_w, ada_b, mix_pre_g, mix_post_g, ffn_pre_g, ffn_post_g, even_w_in, even_pool_w, even_pool_scale, even_w_out, even_ffn_w_gate, even_ffn_w_up, even_ffn_w_down, odd_w_in, odd_conv_w, odd_a_log, odd_dt_bias, odd_norm_g, odd_w_out, odd_router_w, odd_moe_w_gate, odd_moe_w_up, odd_moe_w_down):
    bsz = x.shape[0]
    mod = _adaln_mod(c, ada_w, ada_b)
    sh1, sc1, gt1, sh2, sc2, gt2 = (mod[:, :, n * D_MODEL:(n + 1) * D_MODEL] for n in range(6))

    vec = _vec_rows([mix_pre_g[0], sc1[0], sh1[0]], bsz)
    qs, ks, vs, o_b = _in0(x, positions, vec, even_w_in[0], even_pool_w[0], even_pool_scale[0])
    branches = [_banded_attention(q, k, v, d) for q, k, v, d in zip(qs, ks, vs, DILATIONS)]
    vec = _vec_rows([mix_post_g[0], gt1[0]], bsz)
    x = _out0([o for o, _ in branches], [l for _, l in branches], o_b, x, vec, even_w_out[0])
    vec = _vec_rows([ffn_pre_g[0], sc2[0], sh2[0], ffn_post_g[0], gt2[0]], bsz)
    x = _ffn(x, vec, even_ffn_w_gate[0], even_ffn_w_up[0], even_ffn_w_down[0])

    vec = _vec_rows([mix_pre_g[1], sc1[1], sh1[1]], bsz)
    q, k, v, gate, bg = _in1(x, vec, odd_w_in[0], odd_conv_w[0], odd_a_log[0], odd_dt_bias[0])
    o, expert_w = _delta(q, k, v, gate, bg, odd_norm_g[0],
                         (odd_moe_w_gate[0], odd_moe_w_up[0], odd_moe_w_down[0]))
    vec = _vec_rows([mix_post_g[1], gt1[1], ffn_pre_g[1], sc2[1], sh2[1]], bsz)
    x, h, combine, rank, counts = _out1(o, x, vec, odd_w_out[0], odd_router_w[0])
    vec = _vec_rows([ffn_post_g[1], gt2[1]], bsz)
    return _moe(x, vec, h, combine, rank, counts, *expert_w)
```

```python
import functools

import numpy as np
import jax
import jax.numpy as jnp
from jax import lax
from jax.experimental import pallas as pl
from jax.experimental.pallas import tpu as pltpu

F32 = jnp.float32
BF16 = jnp.bfloat16
HIGHEST = lax.Precision.HIGHEST

D_MODEL = 1024
DEPTH = 2
A_HEADS = 8
A_HEAD_DIM = 64
A_WIDTH = 512
DILATIONS = (1, 4, 16)
BAND = 128
ROT_DIM = 16
ROPE_THETA = 500000.0
POOL_WINDOWS = (2, 4, 8, 16)
POOL_GROUP_DIM = 128
POOL_WIDTH = 512
POOL_HALO = 16
DN_HEADS = 8
DN_HEAD_DIM = 128
DN_WIDTH = 1024
CONV_WIDTH = 4
CONV_HALO = 8
CHUNK = 64
PREP_CHUNKS = 4
FFN_DIM = 3584
N_EXPERTS = 8
EPS = 1e-6
LANES = 128
ROUTE_BLOCK = 2048
SEGMENTS = (512, 256, 128)
ROUTE_ROWS = 256
NEG_INF = float("-inf")

VMEM_LIMIT = 56 * 1024 * 1024


def _params(*sem):
    return pltpu.CompilerParams(dimension_semantics=sem, vmem_limit_bytes=VMEM_LIMIT)


def _dot(a, b, precision=None):
    return jnp.dot(a, b, preferred_element_type=F32, precision=precision)


def _dot_nt(a, b, precision=None):
    return lax.dot_general(a, b, (((1,), (1,)), ((), ())), preferred_element_type=F32,
                           precision=precision)


def _dot_tn(a, b, precision=None):
    return lax.dot_general(a, b, (((0,), (0,)), ((), ())), preferred_element_type=F32,
                           precision=precision)


def _silu(x):
    return x * jax.nn.sigmoid(x)


def _norm_mod(x, g, sc, sh):
    ms = jnp.mean(x * x, axis=-1, keepdims=True)
    return x * lax.rsqrt(ms + EPS) * g * (1.0 + sc) + sh


def _post_residual(x, y, g, gt):
    ms = jnp.mean(y * y, axis=-1, keepdims=True)
    return x + gt * (y * lax.rsqrt(ms + EPS) * g)


def _mod_body(c_ref, w_ref, b_ref, o_ref):
    c = c_ref[...]
    o_ref[...] = _dot(_silu(c), w_ref[...], precision=HIGHEST) + b_ref[...]


def _adaln_mod(c, ada_w, ada_b):
    bsz = c.shape[0]
    rows = 8
    c_pad = jnp.zeros((rows, D_MODEL), F32).at[:bsz].set(c)
    tn = 1536
    out = pl.pallas_call(
        _mod_body,
        grid=(DEPTH, 6 * D_MODEL // tn),
        in_specs=[pl.BlockSpec((rows, D_MODEL), lambda l, n: (0, 0)),
                  pl.BlockSpec((None, D_MODEL, tn), lambda l, n: (l, 0, n)),
                  pl.BlockSpec((None, 1, tn), lambda l, n: (l, 0, n))],
        out_specs=pl.BlockSpec((None, rows, tn), lambda l, n: (l, 0, n)),
        out_shape=jax.ShapeDtypeStruct((DEPTH, rows, 6 * D_MODEL), F32),
        compiler_params=_params("arbitrary", "arbitrary"),
        name="adaln_mod",
    )(c_pad, ada_w, ada_b.reshape(DEPTH, 1, 6 * D_MODEL))
    return out[:, :bsz]


def _vec_rows(rows, bsz):
    rows = [jnp.broadcast_to(r.astype(F32), (bsz, D_MODEL)) for r in rows]
    rows = rows + [jnp.zeros((bsz, D_MODEL), F32)] * (8 - len(rows))
    return jnp.stack(rows, axis=1)


def _rope(t, cosv, sin_lo, sin_hi):
    return (t * cosv + pltpu.roll(t, LANES - ROT_DIM // 2, axis=1) * sin_lo
            + pltpu.roll(t, ROT_DIM // 2, axis=1) * sin_hi)


def _store_folded(nat, refs):
    tm = nat.shape[1]
    for d, ref in zip(DILATIONS, refs):
        for r in range(d):
            rows = pl.ds(r, tm // d, stride=d) if d > 1 else slice(None)
            for t in range(A_WIDTH // LANES):
                col = r * A_WIDTH + t * LANES
                ref[:, col:col + LANES] = nat[t, rows, :].astype(BF16)


def _in0_body(x_ref, pos_ref, vec_ref, w_ref, freq_ref, exp_ref, plain_ref, pw_ref, ps_ref,
              q1_ref, q4_ref, q16_ref, k1_ref, k4_ref, k16_ref, v1_ref, v4_ref, v16_ref, ob_ref,
              ubuf, nat):
    i = pl.program_id(1)
    tm = x_ref.shape[0]

    @pl.when(i == 0)
    def _():
        ubuf[0:POOL_HALO, :] = jnp.zeros((POOL_HALO, POOL_WIDTH), F32)

    h = _norm_mod(x_ref[...], vec_ref[0:1, :], vec_ref[1:2, :], vec_ref[2:3, :]).astype(BF16)

    ang = freq_ref[...] * pos_ref[...].astype(F32)

    def spread(v, e):
        lead = v.astype(BF16).astype(F32)
        return _dot_tn(lead, e) + _dot_tn(v - lead, e)

    cos_t = jnp.cos(ang)
    sin_t = jnp.sin(ang)
    cosv = spread(cos_t, exp_ref[0]) + plain_ref[...]
    sin_lo = -spread(sin_t, exp_ref[1])
    sin_hi = spread(sin_t, exp_ref[2])

    pq = _dot(h, w_ref[:, 0:A_WIDTH])
    for t in range(A_WIDTH // LANES):
        cols = slice(t * LANES, (t + 1) * LANES)
        nat[t] = _rope(pq[:, cols], cosv, sin_lo, sin_hi) * (A_HEAD_DIM ** -0.5)
    _store_folded(nat, (q1_ref, q4_ref, q16_ref))
    pk = _dot(h, w_ref[:, A_WIDTH:2 * A_WIDTH])
    for t in range(A_WIDTH // LANES):
        cols = slice(t * LANES, (t + 1) * LANES)
        nat[t] = _rope(pk[:, cols], cosv, sin_lo, sin_hi)
    _store_folded(nat, (k1_ref, k4_ref, k16_ref))
    pv = _dot(h, w_ref[:, 2 * A_WIDTH:3 * A_WIDTH])
    for t in range(A_WIDTH // LANES):
        nat[t] = pv[:, t * LANES:(t + 1) * LANES]
    _store_folded(nat, (v1_ref, v4_ref, v16_ref))

    ubuf[POOL_HALO:POOL_HALO + tm, :] = _dot(h, w_ref[:, 3 * A_WIDTH:3 * A_WIDTH + POOL_WIDTH])
    tpos = i * tm + lax.broadcasted_iota(jnp.int32, (tm, 1), 0)
    for g, win in enumerate(POOL_WINDOWS):
        cols = slice(g * POOL_GROUP_DIM, (g + 1) * POOL_GROUP_DIM)
        cur = ubuf[POOL_HALO:POOL_HALO + tm, cols]
        acc = cur
        for j in range(1, win):
            acc = acc + ubuf[POOL_HALO - j:POOL_HALO - j + tm, cols]
        cnt = jnp.minimum(tpos + 1, win).astype(F32)
        pooled = acc / cnt - cur
        mixed = _dot(pooled.astype(BF16), pw_ref[g]) * ps_ref[0:1, cols]
        ob_ref[:, cols] = mixed.astype(BF16)
    ubuf[0:POOL_HALO, :] = ubuf[tm:tm + POOL_HALO, :]


def _rope_tables(tm):
    half = ROT_DIM // 2
    inv_freq = ROPE_THETA ** (-jnp.arange(0, ROT_DIM, 2, dtype=F32) / ROT_DIM)
    freq = jnp.broadcast_to(inv_freq[:, None], (half, tm))
    lane = np.arange(LANES) % A_HEAD_DIM
    first, second = lane < half, (lane >= half) & (lane < ROT_DIM)
    uses = lane[None, :] % half == np.arange(half)[:, None]
    expand = np.stack([uses & (first | second), uses & first, uses & second]).astype(np.float32)
    plain = (lane >= ROT_DIM).astype(np.float32).reshape(1, LANES)
    return freq, jnp.asarray(expand), jnp.asarray(plain)


def _in0(x, positions, vec, w_in, pool_w, pool_scale, tm=512):
    bsz, seq, _ = x.shape
    n_out = 3 * A_WIDTH + POOL_WIDTH
    freq, expand, plain = _rope_tables(tm)
    seq_spec = lambda width: pl.BlockSpec((None, tm, width), lambda b, i: (b, i, 0))
    full = lambda shape: pl.BlockSpec(shape, lambda b, i: (0,) * len(shape))
    folded_spec = lambda d: pl.BlockSpec((None, tm // d, d * A_WIDTH), lambda b, i: (b, i, 0))
    folded_sds = lambda d: jax.ShapeDtypeStruct((bsz, seq // d, d * A_WIDTH), BF16)
    outs = pl.pallas_call(
        _in0_body,
        grid=(bsz, seq // tm),
        in_specs=[seq_spec(D_MODEL), pl.BlockSpec((None, None, 1, tm), lambda b, i: (b, i, 0, 0)),
                  pl.BlockSpec((None, 8, D_MODEL), lambda b, i: (b, 0, 0)),
                  full((D_MODEL, n_out)), full(freq.shape), full(expand.shape), full(plain.shape),
                  full((len(POOL_WINDOWS), POOL_GROUP_DIM, POOL_GROUP_DIM)), full((1, POOL_WIDTH))],
        out_specs=[folded_spec(d) for d in DILATIONS] * 3 + [seq_spec(POOL_WIDTH)],
        out_shape=[folded_sds(d) for d in DILATIONS] * 3 + [folded_sds(1)],
        scratch_shapes=[pltpu.VMEM((tm + POOL_HALO, POOL_WIDTH), F32),
                        pltpu.VMEM((A_WIDTH // LANES, tm, LANES), F32)],
        compiler_params=_params("arbitrary", "arbitrary"),
        name="l0_in_proj",
    )(x, positions.reshape(bsz, seq // tm, 1, tm), vec, w_in.astype(BF16), freq, expand, plain,
      pool_w.astype(BF16), pool_scale.reshape(1, POOL_WIDTH))
    n = len(DILATIONS)
    return outs[0:n], outs[n:2 * n], outs[2 * n:3 * n], outs[3 * n]


def _attn_body(q_ref, kp_ref, kc_ref, vp_ref, vc_ref, o_ref, lse_ref):
    i = pl.program_id(2)
    n_bands = q_ref.shape[0] // BAND
    qi = lax.broadcasted_iota(jnp.int32, (BAND, BAND), 0)
    kj = lax.broadcasted_iota(jnp.int32, (BAND, BAND), 1)
    back = kj >= qi
    mask_cur = kj <= qi
    lane = lax.broadcasted_iota(jnp.int32, (BAND, LANES), 1)
    first = lane < A_HEAD_DIM
    pairs = range(A_WIDTH // LANES)
    cols = [slice(g * LANES, (g + 1) * LANES) for g in pairs]
    rows = [slice(s * BAND, (s + 1) * BAND) for s in range(n_bands)]
    probs = [(s, g, half) for s in range(n_bands) for g in pairs for half in range(2)]

    def prev_of(ref_prev, ref_cur, s, g):
        return ref_prev[:, cols[g]] if s == 0 else ref_cur[rows[s - 1], cols[g]]

    qh = []
    for s, g, half in probs:
        qp = q_ref[rows[s], cols[g]]
        qh.append(jnp.where(first if half == 0 else jnp.logical_not(first), qp, jnp.zeros_like(qp)))
    sp = [jnp.where(back & (i > 0) if s == 0 else back, _dot_nt(q, prev_of(kp_ref, kc_ref, s, g)), NEG_INF)
          for q, (s, g, _) in zip(qh, probs)]
    sc = [jnp.where(mask_cur, _dot_nt(q, kc_ref[rows[s], cols[g]]), NEG_INF)
          for q, (s, g, _) in zip(qh, probs)]
    m = [jnp.maximum(jnp.max(a, axis=-1, keepdims=True), jnp.max(b, axis=-1, keepdims=True))
         for a, b in zip(sp, sc)]
    pp = [jnp.exp(a - mm) for a, mm in zip(sp, m)]
    pc = [jnp.exp(a - mm) for a, mm in zip(sc, m)]
    den = [jnp.sum(a, axis=-1, keepdims=True) + jnp.sum(b, axis=-1, keepdims=True)
           for a, b in zip(pp, pc)]
    outs = [(_dot(a.astype(BF16), prev_of(vp_ref, vc_ref, s, g))
             + _dot(b.astype(BF16), vc_ref[rows[s], cols[g]])) / d
            for a, b, d, (s, g, _) in zip(pp, pc, den, probs)]
    per_band = 2 * len(pairs)
    for s in range(n_bands):
        lse_tile = jnp.zeros((BAND, LANES), F32)
        for idx in range(per_band):
            n = s * per_band + idx
            lse_tile = jnp.where(lane == idx, m[n] + jnp.log(den[n]), lse_tile)
        lse_ref[rows[s], :] = lse_tile
        for g in pairs:
            n = s * per_band + 2 * g
            o_ref[rows[s], cols[g]] = jnp.where(first, outs[n], outs[n + 1]).astype(BF16)


def _banded_attention(q, k, v, dilation, bands=4):
    bsz, n_sub, _ = q.shape
    rows = bands * BAND
    cur = pl.BlockSpec((None, rows, A_WIDTH), lambda b, r, i: (b, i, r))
    prev = pl.BlockSpec((None, BAND, A_WIDTH), lambda b, r, i: (b, jnp.maximum(i * bands - 1, 0), r))
    o, lse = pl.pallas_call(
        _attn_body,
        grid=(bsz, dilation, n_sub // rows),
        in_specs=[cur, prev, cur, prev, cur],
        out_specs=[cur, pl.BlockSpec((None, rows, LANES), lambda b, r, i: (b, i, r))],
        out_shape=[jax.ShapeDtypeStruct((bsz, n_sub, dilation * A_WIDTH), BF16),
                   jax.ShapeDtypeStruct((bsz, n_sub, dilation * LANES), F32)],
        compiler_params=_params("arbitrary", "arbitrary", "arbitrary"),
        name=f"dilated_attn_d{dilation}",
    )(q, k, k, v, v)
    return o, lse


def _out0_body(o1_ref, o4_ref, o16_ref, l1_ref, l4_ref, l16_ref, ob_ref, x_ref, vec_ref, w_ref,
               out_ref, o_nat, l_nat):
    tm = x_ref.shape[0]
    for slot, (d, o_ref, l_ref) in enumerate(zip(DILATIONS, (o1_ref, o4_ref, o16_ref),
                                                  (l1_ref, l4_ref, l16_ref))):
        for r in range(d):
            rows = pl.ds(r, tm // d, stride=d) if d > 1 else slice(None)
            for t in range(A_WIDTH // LANES):
                col = r * A_WIDTH + t * LANES
                o_nat[slot, t, rows, :] = o_ref[:, col:col + LANES].astype(F32)
            l_nat[slot, rows, :] = l_ref[:, r * LANES:(r + 1) * LANES]
    la, lb, lc = l_nat[0], l_nat[1], l_nat[2]
    m = jnp.maximum(jnp.maximum(la, lb), lc)
    ea, eb, ec = jnp.exp(la - m), jnp.exp(lb - m), jnp.exp(lc - m)
    tot = ea + eb + ec
    weights = (ea / tot, eb / tot, ec / tot)
    lane = lax.broadcasted_iota(jnp.int32, (tm, LANES), 1)
    first = lane < A_HEAD_DIM
    pieces = []
    for g in range(A_WIDTH // LANES):
        cols = slice(g * LANES, (g + 1) * LANES)
        acc = jnp.zeros((tm, LANES), F32)
        for slot, wt in enumerate(weights):
            w_pair = jnp.where(first,
                               jnp.broadcast_to(wt[:, 2 * g:2 * g + 1], (tm, LANES)),
                               jnp.broadcast_to(wt[:, 2 * g + 1:2 * g + 2], (tm, LANES)))
            acc = acc + w_pair * o_nat[slot, g]
        pieces.append(acc.astype(BF16))
    o_a = jnp.concatenate(pieces, axis=-1)
    y = _dot(o_a, w_ref[0:A_WIDTH, :]) + _dot(ob_ref[...], w_ref[A_WIDTH:A_WIDTH + POOL_WIDTH, :])
    out_ref[...] = _post_residual(x_ref[...], y, vec_ref[0:1, :], vec_ref[1:2, :])


def _out0(os, lses, ob, x, vec, w_out, tm=512):
    bsz, seq, _ = x.shape
    seq_spec = lambda width: pl.BlockSpec((None, tm, width), lambda b, i: (b, i, 0))
    folded = lambda width: [pl.BlockSpec((None, tm // d, d * width), lambda b, i: (b, i, 0))
                            for d in DILATIONS]
    n = len(DILATIONS)
    return pl.pallas_call(
        _out0_body,
        grid=(bsz, seq // tm),
        in_specs=folded(A_WIDTH) + folded(LANES) + [seq_spec(POOL_WIDTH),
                  seq_spec(D_MODEL), pl.BlockSpec((None, 8, D_MODEL), lambda b, i: (b, 0, 0)),
                  pl.BlockSpec((A_WIDTH + POOL_WIDTH, D_MODEL), lambda b, i: (0, 0))],
        out_specs=seq_spec(D_MODEL),
        out_shape=jax.ShapeDtypeStruct((bsz, seq, D_MODEL), F32),
        scratch_shapes=[pltpu.VMEM((n, A_WIDTH // LANES, tm, LANES), F32),
                        pltpu.VMEM((n, tm, LANES), F32)],
        compiler_params=_params("arbitrary", "arbitrary"),
        name="l0_out_proj",
    )(*os, *lses, ob, x, vec, w_out.astype(BF16))


def _ffn_body(x_ref, vec_ref, wg_ref, wu_ref, wd_ref, out_ref, hbuf, acc):
    f = pl.program_id(1)

    @pl.when(f == 0)
    def _():
        hbuf[...] = _norm_mod(x_ref[...], vec_ref[0:1, :], vec_ref[1:2, :],
                              vec_ref[2:3, :]).astype(BF16)
        acc[...] = jnp.zeros_like(acc)

    h = hbuf[...]
    act = _silu(_dot(h, wg_ref[...].astype(BF16))) * _dot(h, wu_ref[...].astype(BF16))
    acc[...] += _dot(act.astype(BF16), wd_ref[...].astype(BF16))

    @pl.when(f == pl.num_programs(1) - 1)
    def _():
        out_ref[...] = _post_residual(x_ref[...], acc[...], vec_ref[3:4, :], vec_ref[4:5, :])


def _ffn(x, vec, w_gate, w_up, w_down, tm=1024, tf=512):
    bsz, seq, _ = x.shape
    tiles_per_seq = seq // tm
    xt = x.reshape(bsz * seq, D_MODEL)
    row = pl.BlockSpec((tm, D_MODEL), lambda i, f: (i, 0))
    out = pl.pallas_call(
        _ffn_body,
        grid=(bsz * seq // tm, FFN_DIM // tf),
        in_specs=[row, pl.BlockSpec((None, 8, D_MODEL), lambda i, f: (i // tiles_per_seq, 0, 0)),
                  pl.BlockSpec((D_MODEL, tf), lambda i, f: (0, f)),
                  pl.BlockSpec((D_MODEL, tf), lambda i, f: (0, f)),
                  pl.BlockSpec((tf, D_MODEL), lambda i, f: (f, 0))],
        out_specs=row,
        out_shape=jax.ShapeDtypeStruct((bsz * seq, D_MODEL), F32),
        scratch_shapes=[pltpu.VMEM((tm, D_MODEL), BF16), pltpu.VMEM((tm, D_MODEL), F32)],
        compiler_params=_params("arbitrary", "arbitrary"),
        name="l0_swiglu",
    )(xt, vec, w_gate, w_up, w_down)
    return out.reshape(bsz, seq, D_MODEL)


def _split3(a):
    a1 = a.astype(BF16)
    r1 = a - a1.astype(F32)
    a2 = r1.astype(BF16)
    a3 = (r1 - a2.astype(F32)).astype(BF16)
    return a1, a2, a3


def _dot_split(a, b):
    a_hi = a.astype(BF16)
    a_lo = (a - a_hi.astype(F32)).astype(BF16)
    b_hi = b.astype(BF16)
    b_lo = (b - b_hi.astype(F32)).astype(BF16)
    return _dot(a_hi, b_hi) + (_dot(a_hi, b_lo) + _dot(a_lo, b_hi))


def _in1_body(x_ref, vec_ref, w_ref, wba_ref, cw_ref, hp_ref,
              q_ref, k_ref, v_ref, gate_ref, bg_ref, cbuf):
    i = pl.program_id(1)
    tm = x_ref.shape[0]

    @pl.when(i == 0)
    def _():
        cbuf[:, 0:CONV_HALO, :] = jnp.zeros((3, CONV_HALO, DN_WIDTH), F32)

    hf = _norm_mod(x_ref[...], vec_ref[0:1, :], vec_ref[1:2, :], vec_ref[2:3, :])
    h = hf.astype(BF16)

    for idx, dst in enumerate((q_ref, k_ref, v_ref)):
        cols = slice(idx * DN_WIDTH, (idx + 1) * DN_WIDTH)
        cbuf[idx, CONV_HALO:CONV_HALO + tm, :] = _dot(h, w_ref[:, cols])
        y = jnp.zeros((tm, DN_WIDTH), F32)
        for j in range(CONV_WIDTH):
            off = CONV_HALO - (CONV_WIDTH - 1) + j
            y = y + cw_ref[j:j + 1, cols] * cbuf[idx, off:off + tm, :]
        y = _silu(y)
        cbuf[idx, 0:CONV_HALO, :] = cbuf[idx, tm:tm + CONV_HALO, :]
        if idx < 2:
            scale = DN_HEAD_DIM ** -0.5 if idx == 0 else 1.0
            for hd in range(DN_HEADS):
                hc = slice(hd * DN_HEAD_DIM, (hd + 1) * DN_HEAD_DIM)
                blk = y[:, hc]
                ss = jnp.sum(blk * blk, axis=-1, keepdims=True)
                dst[:, hc] = (blk * (lax.rsqrt(ss + EPS) * scale)).astype(BF16)
        else:
            dst[...] = y.astype(BF16)

    gate_ref[...] = _silu(_dot(h, w_ref[:, 3 * DN_WIDTH:4 * DN_WIDTH])).astype(BF16)

    ba = _dot_split(hf, wba_ref[...])
    beta = jax.nn.sigmoid(ba)
    z = ba + hp_ref[1:2, :]
    softplus = jnp.maximum(z, 0.0) + jnp.log1p(jnp.exp(-jnp.abs(z)))
    g = -jnp.exp(hp_ref[0:1, :]) * softplus
    r = lax.broadcasted_iota(jnp.int32, (tm, tm), 0)
    c = lax.broadcasted_iota(jnp.int32, (tm, tm), 1)
    tri = jnp.where((r // CHUNK == c // CHUNK) & (c <= r), 1.0, 0.0).astype(BF16)
    g1, g2, g3 = _split3(g)
    gc = _dot(tri, g1) + _dot(tri, g2) + _dot(tri, g3)
    lane = lax.broadcasted_iota(jnp.int32, (tm, LANES), 1)
    bg_ref[...] = jnp.where(lane < DN_HEADS, beta, gc)


def _in1(x, vec, w_in, conv_w, a_log, dt_bias, tm=256):
    bsz, seq, _ = x.shape
    w_main = w_in[:, :4 * DN_WIDTH].astype(BF16)
    w_ba = jnp.zeros((D_MODEL, LANES), F32).at[:, :2 * DN_HEADS].set(w_in[:, 4 * DN_WIDTH:])
    hp = jnp.zeros((8, LANES), F32)
    hp = hp.at[0, DN_HEADS:2 * DN_HEADS].set(a_log).at[1, DN_HEADS:2 * DN_HEADS].set(dt_bias)
    seq_spec = lambda width: pl.BlockSpec((None, tm, width), lambda b, i: (b, i, 0))
    full = lambda shape: pl.BlockSpec(shape, lambda b, i: (0,) * len(shape))
    wide = jax.ShapeDtypeStruct((bsz, seq, DN_WIDTH), BF16)
    return pl.pallas_call(
        _in1_body,
        grid=(bsz, seq // tm),
        in_specs=[seq_spec(D_MODEL), pl.BlockSpec((None, 8, D_MODEL), lambda b, i: (b, 0, 0)),
                  full((D_MODEL, 4 * DN_WIDTH)), full((D_MODEL, LANES)),
                  full((CONV_WIDTH, 3 * DN_WIDTH)), full((8, LANES))],
        out_specs=[seq_spec(DN_WIDTH)] * 4 + [seq_spec(LANES)],
        out_shape=[wide] * 4 + [jax.ShapeDtypeStruct((bsz, seq, LANES), F32)],
        scratch_shapes=[pltpu.VMEM((3, tm + CONV_HALO, DN_WIDTH), F32)],
        compiler_params=_params("arbitrary", "arbitrary"),
        name="l1_in_proj",
    )(x, vec, w_main, w_ba, conv_w, hp)


def _unit_lower_inverses(lmats):
    n = lmats[0].shape[0]
    r = lax.broadcasted_iota(jnp.int32, (n, n), 0)
    c = lax.broadcasted_iota(jnp.int32, (n, n), 1)

    def below(size):
        return (r // (2 * size) == c // (2 * size)) & (r % (2 * size) >= size) & (c % (2 * size) < size)

    eye = jnp.where(r == c, 1.0, 0.0).astype(F32)
    first = below(1)
    invs = [eye - jnp.where(first, lm, 0.0) for lm in lmats]
    size = 2
    while size < n:
        mask = below(size)
        offs = [jnp.where(mask, lm, 0.0).astype(BF16) for lm in lmats]
        inv16 = [x.astype(BF16) for x in invs]
        xc = [_dot(x, o).astype(BF16) for x, o in zip(inv16, offs)]
        invs = [x - _dot(t, x16) for x, t, x16 in zip(invs, xc, inv16)]
        size *= 2
    return invs


def _delta_body(q_ref, k_ref, v_ref, gate_ref, bg_ref, gcr_ref, ng_ref, wa_ref, wb_ref, wc_ref,
                o_ref, wa_out, wb_out, wc_out, state, p_s, n_s, qp_s, op_s, dec_s):
    for src, dst in ((wa_ref, wa_out), (wb_ref, wb_out), (wc_ref, wc_out)):
        dst[...] = src[...].astype(BF16)

    i = pl.program_id(1)
    n_chunks = q_ref.shape[0] // CHUNK
    heads = range(DN_HEADS)
    head_cols = [slice(hd * DN_HEAD_DIM, (hd + 1) * DN_HEAD_DIM) for hd in heads]

    @pl.when(i == 0)
    def _():
        state[...] = jnp.zeros_like(state)

    r = lax.broadcasted_iota(jnp.int32, (CHUNK, CHUNK), 0)
    c = lax.broadcasted_iota(jnp.int32, (CHUNK, CHUNK), 1)
    causal = r >= c
    strict = r > c

    def prepare(step, carry):
        chunks = [step * PREP_CHUNKS + n for n in range(PREP_CHUNKS)]
        rows = [pl.ds(pl.multiple_of(ci * CHUNK, CHUNK), CHUNK) for ci in chunks]
        bg = [bg_ref[r, :] for r in rows]
        gcr_all = [gcr_ref[ci] for ci in chunks]
        probs = [(n, hd) for n in range(PREP_CHUNKS) for hd in heads]
        k16 = [k_ref[rows[n], head_cols[hd]] for n, hd in probs]
        q16 = [q_ref[rows[n], head_cols[hd]] for n, hd in probs]
        kf = [x.astype(F32) for x in k16]
        beta = [bg[n][:, hd:hd + 1] for n, hd in probs]
        gcc = [bg[n][:, DN_HEADS + hd:DN_HEADS + hd + 1] for n, hd in probs]
        gcr = [gcr_all[n][hd:hd + 1, :] for n, hd in probs]
        kb = [x * b for x, b in zip(kf, beta)]
        both = [_dot_nt(jnp.concatenate([a.astype(BF16), b], axis=0), x)
                for a, b, x in zip(kb, q16, k16)]
        decay = [jnp.exp(jnp.where(causal, a - b, NEG_INF)) for a, b in zip(gcc, gcr)]
        lmats = [jnp.where(strict, m[0:CHUNK] * d, 0.0) for m, d in zip(both, decay)]
        attn = [(m[CHUNK:2 * CHUNK] * d).astype(BF16) for m, d in zip(both, decay)]
        invs = _unit_lower_inverses(lmats)
        eg = [jnp.exp(x) for x in gcc]
        rhs = [jnp.concatenate([(v_ref[rows[n], head_cols[hd]].astype(F32) * b).astype(BF16),
                                (a * e).astype(BF16)], axis=1)
               for (n, hd), b, a, e in zip(probs, beta, kb, eg)]
        sol = [_dot(x.astype(BF16), y).astype(BF16) for x, y in zip(invs, rhs)]
        au = [_dot(a, s) for a, s in zip(attn, sol)]
        g_last = [x[CHUNK - 1:CHUNK, :] for x in gcc]
        kg = [(x * jnp.exp(gl - g)).astype(BF16) for x, gl, g in zip(kf, g_last, gcc)]
        kn = [_dot_tn(a, s) for a, s in zip(kg, sol)]
        for idx, (n, hd) in enumerate(probs):
            ci = chunks[n]
            n_s[ci, hd] = kn[idx][:, 0:DN_HEAD_DIM]
            p_s[ci, hd] = kn[idx][:, DN_HEAD_DIM:2 * DN_HEAD_DIM].astype(BF16)
            op_s[ci, hd] = au[idx][:, 0:DN_HEAD_DIM]
            qp_s[ci, hd] = (q16[idx].astype(F32) * eg[idx]
                            - au[idx][:, DN_HEAD_DIM:2 * DN_HEAD_DIM]).astype(BF16)
            dec_s[ci, hd] = jnp.broadcast_to(jnp.exp(g_last[idx]), (8, DN_HEAD_DIM))
        return carry

    def scan(ci, carry):
        rows = pl.ds(pl.multiple_of(ci * CHUNK, CHUNK), CHUNK)
        for hd, hc in zip(heads, head_cols):
            s = state[hd]
            s16 = s.astype(BF16)
            o = _dot(qp_s[ci, hd], s16) + op_s[ci, hd]
            state[hd] = s * dec_s[ci, hd][0:1, :] + n_s[ci, hd] - _dot(p_s[ci, hd], s16)
            ms = jnp.mean(o * o, axis=-1, keepdims=True)
            o = o * lax.rsqrt(ms + EPS) * ng_ref[0:1, :] * gate_ref[rows, hc].astype(F32)
            o_ref[rows, hc] = o.astype(BF16)
        return carry

    lax.fori_loop(0, n_chunks // PREP_CHUNKS, prepare, 0)
    lax.fori_loop(0, n_chunks, scan, 0, unroll=4)


def _delta(q, k, v, gate, bg, norm_g, to_narrow, block=256):
    bsz, seq, _ = q.shape
    n = seq // CHUNK
    nc = block // CHUNK
    steps = bsz * (seq // block)
    gc_rows = bg[:, :, DN_HEADS:2 * DN_HEADS].reshape(bsz, n, CHUNK, DN_HEADS).transpose(0, 1, 3, 2)
    seq_spec = lambda width: pl.BlockSpec((None, block, width), lambda b, i: (b, i, 0))
    per_head = lambda rows, dtype: pltpu.VMEM((nc, DN_HEADS, rows, DN_HEAD_DIM), dtype)
    per_expert = steps // N_EXPERTS
    assert steps == per_expert * N_EXPERTS
    assert all(w.shape[0] == N_EXPERTS and w.shape[1] % (16 * per_expert) == 0 for w in to_narrow)

    def slab_spec(w):
        step = lambda b, i: b * (seq // block) + i
        return pl.BlockSpec((None, w.shape[1] // per_expert, w.shape[2]),
                            lambda b, i: (step(b, i) // per_expert, step(b, i) % per_expert, 0))

    slab_specs = [slab_spec(w) for w in to_narrow]
    outs = pl.pallas_call(
        _delta_body,
        grid=(bsz, seq // block),
        in_specs=[seq_spec(DN_WIDTH)] * 4 + [seq_spec(LANES),
                  pl.BlockSpec((None, nc, DN_HEADS, CHUNK), lambda b, i: (b, i, 0, 0)),
                  pl.BlockSpec((1, DN_HEAD_DIM), lambda b, i: (0, 0))] + slab_specs,
        out_specs=[seq_spec(DN_WIDTH)] + slab_specs,
        out_shape=[jax.ShapeDtypeStruct((bsz, seq, DN_WIDTH), BF16)]
                  + [jax.ShapeDtypeStruct(w.shape, BF16) for w in to_narrow],
        scratch_shapes=[pltpu.VMEM((DN_HEADS, DN_HEAD_DIM, DN_HEAD_DIM), F32),
                        per_head(DN_HEAD_DIM, BF16), per_head(DN_HEAD_DIM, F32),
                        per_head(CHUNK, BF16), per_head(CHUNK, F32), per_head(8, F32)],
        compiler_params=_params("arbitrary", "arbitrary"),
        name="gated_delta",
    )(q, k, v, gate, bg, gc_rows, norm_g.reshape(1, DN_HEAD_DIM), *to_narrow)
    return outs[0], outs[1:]


def _route(hf, rw):
    tm = hf.shape[0]
    lane = lax.broadcasted_iota(jnp.int32, (tm, LANES), 1)
    logits = jnp.where(lane < N_EXPERTS, _dot_split(hf, rw), NEG_INF)
    m1 = jnp.max(logits, axis=-1, keepdims=True)
    i1 = jnp.min(jnp.where(logits == m1, lane, LANES), axis=-1, keepdims=True)
    rest = jnp.where(lane == i1, NEG_INF, logits)
    m2 = jnp.max(rest, axis=-1, keepdims=True)
    i2 = jnp.min(jnp.where(rest == m2, lane, LANES), axis=-1, keepdims=True)
    e2 = jnp.exp(m2 - m1)
    w1 = 1.0 / (1.0 + e2)
    w2 = e2 / (1.0 + e2)
    combine = jnp.where(lane == i1, w1, 0.0) + jnp.where(lane == i2, w2, 0.0)
    chosen = jnp.where((lane == i1) | (lane == i2), 1.0, 0.0)
    return combine, chosen


def _out1_body(a_ref, x_ref, vec_ref, w_ref, rw_ref,
               x_out, h_out, cw_out, rank_out, cnt_out, running, *, tiles_per_block):
    i = pl.program_id(1)
    tm = x_ref.shape[0]

    @pl.when(i % tiles_per_block == 0)
    def _():
        running[...] = jnp.zeros_like(running)

    rows = [slice(n * ROUTE_ROWS, (n + 1) * ROUTE_ROWS) for n in range(tm // ROUTE_ROWS)]
    y = [_dot(a_ref[rs, :], w_ref[...]) for rs in rows]
    x2 = [_post_residual(x_ref[rs, :], yy, vec_ref[0:1, :], vec_ref[1:2, :]) for rs, yy in zip(rows, y)]
    hf = [_norm_mod(xx, vec_ref[2:3, :], vec_ref[3:4, :], vec_ref[4:5, :]) for xx in x2]
    routed = [_route(hh, rw_ref[...]) for hh in hf]
    r = lax.broadcasted_iota(jnp.int32, (ROUTE_ROWS, ROUTE_ROWS), 0)
    c = lax.broadcasted_iota(jnp.int32, (ROUTE_ROWS, ROUTE_ROWS), 1)
    before = jnp.where(c < r, 1.0, 0.0).astype(BF16)
    local = [_dot(before, chosen.astype(BF16)) for _, chosen in routed]
    offset = running[0:1, :]
    for rs, xx, hh, (combine, chosen), loc in zip(rows, x2, hf, routed, local):
        x_out[rs, :] = xx
        h_out[rs, :] = hh.astype(BF16)
        cw_out[rs, :] = combine
        rank_out[rs, :] = jnp.where(chosen > 0.0, loc + offset, -1.0)
        offset = offset + jnp.sum(chosen, axis=0, keepdims=True)
    running[...] = jnp.broadcast_to(offset, running.shape)
    cnt_out[...] = running[...]


def _out1(a, x, vec, w_out, router_w, tm=512, route_block=ROUTE_BLOCK):
    bsz, seq, _ = x.shape
    tiles_per_block = route_block // tm
    blocks_per_seq = seq // route_block
    rw = jnp.zeros((D_MODEL, LANES), F32).at[:, :N_EXPERTS].set(router_w)
    seq_spec = lambda width: pl.BlockSpec((None, tm, width), lambda b, i: (b, i, 0))
    return pl.pallas_call(
        functools.partial(_out1_body, tiles_per_block=tiles_per_block),
        grid=(bsz, seq // tm),
        in_specs=[seq_spec(DN_WIDTH), seq_spec(D_MODEL),
                  pl.BlockSpec((None, 8, D_MODEL), lambda b, i: (b, 0, 0)),
                  pl.BlockSpec((DN_WIDTH, D_MODEL), lambda b, i: (0, 0)),
                  pl.BlockSpec((D_MODEL, LANES), lambda b, i: (0, 0))],
        out_specs=[seq_spec(D_MODEL), seq_spec(D_MODEL), seq_spec(LANES), seq_spec(LANES),
                   pl.BlockSpec((None, 8, LANES),
                                lambda b, i: (b * blocks_per_seq + i // tiles_per_block, 0, 0))],
        out_shape=[jax.ShapeDtypeStruct((bsz, seq, D_MODEL), F32),
                   jax.ShapeDtypeStruct((bsz, seq, D_MODEL), BF16),
                   jax.ShapeDtypeStruct((bsz, seq, LANES), F32),
                   jax.ShapeDtypeStruct((bsz, seq, LANES), F32),
                   jax.ShapeDtypeStruct((bsz * blocks_per_seq, 8, LANES), F32)],
        scratch_shapes=[pltpu.VMEM((8, LANES), F32)],
        compiler_params=_params("arbitrary", "arbitrary"),
        name="l1_out_proj_route",
    )(a, x, vec, w_out.astype(BF16), rw)


def _moe_body(cnt_ref, h_ref, rrow_ref, wrow_ref, rcol_ref, x_ref, vec_ref, wg_ref, wu_ref, wd_ref,
              out_ref, xs, ys, wslot):
    b, e, f = pl.program_id(0), pl.program_id(1), pl.program_id(2)
    tb = h_ref.shape[0]

    @pl.when((e == 0) & (f == 0))
    def _():
        out_ref[...] = jnp.zeros_like(out_ref)

    count = cnt_ref[b * N_EXPERTS + e]
    padded = (count + SEGMENTS[-1] - 1) // SEGMENTS[-1] * SEGMENTS[-1]
    n_large = padded // SEGMENTS[0]

    def for_each_segment(fn):
        def large(t, carry):
            fn(pl.multiple_of(t * SEGMENTS[0], SEGMENTS[0]), SEGMENTS[0])
            return carry

        lax.fori_loop(0, n_large, large, 0)
        done = n_large * SEGMENTS[0]
        for size in SEGMENTS[1:]:
            take = ((padded - done) // size) > 0

            @pl.when(take)
            def _(done=done, size=size):
                fn(pl.multiple_of(done, SEGMENTS[-1]), size)

            done = done + jnp.where(take, size, 0)

    @pl.when(f == 0)
    def _():
        rrow = rrow_ref[pl.ds(e, 1), :]
        wrow = wrow_ref[pl.ds(e, 1), :]

        def pack(base, size):
            rows = pl.ds(base, size)
            slot = (base + lax.broadcasted_iota(jnp.int32, (size, 1), 0)).astype(F32)
            hit = rrow == slot
            xs[rows, :] = _dot(jnp.where(hit, 1.0, 0.0).astype(BF16), h_ref[...]).astype(BF16)
            wslot[rows, :] = jnp.sum(jnp.where(hit, wrow, 0.0), axis=-1, keepdims=True)
            ys[rows, :] = jnp.zeros((size, D_MODEL), F32)

        for_each_segment(pack)

    def expert(base, size):
        rows = pl.ds(base, size)
        x = xs[rows, :]
        act = _silu(_dot(x, wg_ref[...])) * _dot(x, wu_ref[...])
        ys[rows, :] += _dot(act.astype(BF16), wd_ref[...])

    for_each_segment(expert)

    @pl.when(f == pl.num_programs(2) - 1)
    def _():
        lane = lax.broadcasted_iota(jnp.int32, (tb, LANES), 1)
        rcol = jnp.sum(jnp.where(lane == e, rcol_ref[...], 0.0), axis=-1, keepdims=True)

        def unpack(base, size):
            rows = pl.ds(base, size)
            y = (ys[rows, :] * wslot[rows, :]).astype(BF16)
            slot = (base + lax.broadcasted_iota(jnp.int32, (1, size), 1)).astype(F32)
            back = jnp.where(rcol == slot, 1.0, 0.0).astype(BF16)
            out_ref[...] += _dot(back, y)

        for_each_segment(unpack)

        @pl.when(e == pl.num_programs(1) - 1)
        def _():
            out_ref[...] = _post_residual(x_ref[...], out_ref[...], vec_ref[0:1, :], vec_ref[1:2, :])


def _moe(x, vec, h, combine, rank, counts, w_gate, w_up, w_down, route_block=ROUTE_BLOCK, tf=512):
    bsz, seq, _ = h.shape
    tokens = bsz * seq
    n_blocks = tokens // route_block
    per_block_rows = lambda t: (t.reshape(n_blocks, route_block, LANES)[:, :, :N_EXPERTS]
                                .transpose(0, 2, 1))
    cnt = counts[:, 0, :N_EXPERTS].astype(jnp.int32).reshape(n_blocks * N_EXPERTS)
    assert route_block % SEGMENTS[-1] == 0
    blocks_per_seq = seq // route_block
    once = pl.Buffered(1)
    per_block = lambda width: pl.BlockSpec((route_block, width), lambda b, e, f, cnt: (b, 0),
                                           pipeline_mode=once)
    per_expert = pl.BlockSpec((None, N_EXPERTS, route_block), lambda b, e, f, cnt: (b, 0, 0),
                              pipeline_mode=once)
    grid_spec = pltpu.PrefetchScalarGridSpec(
        num_scalar_prefetch=1,
        grid=(n_blocks, N_EXPERTS, FFN_DIM // tf),
        in_specs=[per_block(D_MODEL), per_expert, per_expert, per_block(LANES), per_block(D_MODEL),
                  pl.BlockSpec((None, 8, D_MODEL), lambda b, e, f, cnt: (b // blocks_per_seq, 0, 0)),
                  pl.BlockSpec((None, D_MODEL, tf), lambda b, e, f, cnt: (e, 0, f)),
                  pl.BlockSpec((None, D_MODEL, tf), lambda b, e, f, cnt: (e, 0, f)),
                  pl.BlockSpec((None, tf, D_MODEL), lambda b, e, f, cnt: (e, f, 0))],
        out_specs=per_block(D_MODEL),
        scratch_shapes=[pltpu.VMEM((route_block, D_MODEL), BF16), pltpu.VMEM((route_block, D_MODEL), F32),
                        pltpu.VMEM((route_block, 1), F32)])
    out = pl.pallas_call(
        _moe_body,
        grid_spec=grid_spec,
        out_shape=jax.ShapeDtypeStruct((tokens, D_MODEL), F32),
        compiler_params=_params("arbitrary", "arbitrary", "arbitrary"),
        name="l1_moe",
    )(cnt, h.reshape(tokens, D_MODEL), per_block_rows(rank), per_block_rows(combine),
      rank.reshape(tokens, LANES), x.reshape(tokens, D_MODEL), vec, w_gate, w_up, w_down)
    return out.reshape(bsz, seq, D_MODEL)


def kernel(x, c, positions, ada_w, ada_b, mix_pre_g, mix_post_g, ffn_pre_g, ffn_post_g, even_w_in, even_pool_w, even_pool_scale, even_w_out, even_ffn_w_gate, even_ffn_w_up, even_ffn_w_down, odd_w_in, odd_conv_w, odd_a_log, odd_dt_bias, odd_norm_g, odd_w_out, odd_router_w, odd_moe_w_gate, odd_moe_w_up, odd_moe_w_down):
    bsz = x.shape[0]
    mod = _adaln_mod(c, ada_w, ada_b)
    sh1, sc1, gt1, sh2, sc2, gt2 = (mod[:, :, n * D_MODEL:(n + 1) * D_MODEL] for n in range(6))

    vec = _vec_rows([mix_pre_g[0], sc1[0], sh1[0]], bsz)
    qs, ks, vs, o_b = _in0(x, positions, vec, even_w_in[0], even_pool_w[0], even_pool_scale[0])
    branches = [_banded_attention(q, k, v, d) for q, k, v, d in zip(qs, ks, vs, DILATIONS)]
    vec = _vec_rows([mix_post_g[0], gt1[0]], bsz)
    x = _out0([o for o, _ in branches], [l for _, l in branches], o_b, x, vec, even_w_out[0])
    vec = _vec_rows([ffn_pre_g[0], sc2[0], sh2[0], ffn_post_g[0], gt2[0]], bsz)
    x = _ffn(x, vec, even_ffn_w_gate[0], even_ffn_w_up[0], even_ffn_w_down[0])

    vec = _vec_rows([mix_pre_g[1], sc1[1], sh1[1]], bsz)
    q, k, v, gate, bg = _in1(x, vec, odd_w_in[0], odd_conv_w[0], odd_a_log[0], odd_dt_bias[0])
    o, expert_w = _delta(q, k, v, gate, bg, odd_norm_g[0],
                         (odd_moe_w_gate[0], odd_moe_w_up[0], odd_moe_w_down[0]))
    vec = _vec_rows([mix_post_g[1], gt1[1], ffn_pre_g[1], sc2[1], sh2[1]], bsz)
    x, h, combine, rank, counts = _out1(o, x, vec, odd_w_out[0], odd_router_w[0])
    vec = _vec_rows([ffn_post_g[1], gt2[1]], bsz)
    return _moe(x, vec, h, combine, rank, counts, *expert_w)
```
